```python
import math
import jax, jax.numpy as jnp
from jax import lax
import numpy as np

D_MODEL = 1024
BATCH = 8
SEQ = 8192
DEPTH = 1

CHUNK = 64
D_MIX = D_MODEL
S5_WIDTH = 512
S5_GROUP = 16
S5_GROUPS = S5_WIDTH // S5_GROUP
S5_STATE = 64
S5_DT_MIN = 1e-3
S5_DT_MAX = 1e-1
MLA_WIDTH = D_MIX - S5_WIDTH
MLA_HEADS = 8
MLA_NOPE = 64
MLA_ROPE = 32
MLA_V = MLA_WIDTH // MLA_HEADS
Q_LORA = 384
KV_LORA = 256
ROPE_THETA = 10000.0
Q_BLOCK = 128
D_IN = S5_WIDTH + Q_LORA + KV_LORA + MLA_ROPE
PEER_HEADS = 8
PEER_NKEYS = 128
PEER_EXPERTS = PEER_NKEYS * PEER_NKEYS
PEER_QDIM = 256
PEER_HALF = PEER_QDIM // 2
PEER_TOPK = 16
PEER_BLOCK = 128
DEEPNORM_ALPHA = (2.0 * DEPTH) ** 0.25
DEEPNORM_BETA = (8.0 * DEPTH) ** -0.25
NORM_EPS = 1e-6
MASK_VALUE = -1e30

kernel_name = "hybrid_s5_mla_peer_block"


def _layernorm(x, g, b):
    xf = x.astype(jnp.float32)
    mu = jnp.mean(xf, axis=-1, keepdims=True)
    var = jnp.mean(jnp.square(xf - mu), axis=-1, keepdims=True)
    y = (xf - mu) * lax.rsqrt(var + NORM_EPS) * g.astype(jnp.float32) + b.astype(jnp.float32)
    return y.astype(x.dtype)


def _rmsnorm(x, g):
    xf = x.astype(jnp.float32)
    y = xf * lax.rsqrt(jnp.mean(xf * xf, axis=-1, keepdims=True) + NORM_EPS) * g.astype(jnp.float32)
    return y.astype(x.dtype)


def _rope_angles(positions):
    inv_freq = ROPE_THETA ** (-jnp.arange(0, MLA_ROPE, 2, dtype=jnp.float32) / MLA_ROPE)
    ang = positions.astype(jnp.float32)[..., None] * inv_freq
    return jnp.cos(ang), jnp.sin(ang)


def _apply_rope(x, cos, sin):
    x1, x2 = jnp.split(x, 2, axis=-1)
    cos = cos.astype(x.dtype)
    sin = sin.astype(x.dtype)
    return jnp.concatenate([x1 * cos - x2 * sin, x2 * cos + x1 * sin], axis=-1)


def _complex_affine_combine(left, right):
    a1r, a1i, b1r, b1i = left
    a2r, a2i, b2r, b2i = right
    ar = a2r * a1r - a2i * a1i
    ai = a2r * a1i + a2i * a1r
    br = a2r * b1r - a2i * b1i + b2r
    bi = a2r * b1i + a2i * b1r + b2i
    return (ar, ai, br, bi)


def _s5_mixer(u, lam_re, lam_im, log_dt, b_re, b_im, c_re, c_im, d, w_glu, b_glu):
    bsz, L, _ = u.shape
    ug = u.reshape(bsz, L, S5_GROUPS, S5_GROUP)
    dt = jnp.exp(log_dt)[:, None]
    mag = jnp.exp(lam_re * dt)
    ab_re = mag * jnp.cos(lam_im * dt)
    ab_im = mag * jnp.sin(lam_im * dt)
    den = lam_re * lam_re + lam_im * lam_im
    nr = ab_re - 1.0
    coef_re = (nr * lam_re + ab_im * lam_im) / den
    coef_im = (ab_im * lam_re - nr * lam_im) / den
    bb_re = coef_re[..., None] * b_re - coef_im[..., None] * b_im
    bb_im = coef_re[..., None] * b_im + coef_im[..., None] * b_re
    bu_re = jnp.einsum('blgh,gph->blgp', ug, bb_re)
    bu_im = jnp.einsum('blgh,gph->blgp', ug, bb_im)
    a_re = jnp.broadcast_to(ab_re, (1, L) + ab_re.shape)
    a_im = jnp.broadcast_to(ab_im, (1, L) + ab_im.shape)
    _, _, s_re, s_im = lax.associative_scan(_complex_affine_combine, (a_re, a_im, bu_re, bu_im), axis=1)
    y = jnp.einsum('blgp,ghp->blgh', s_re, c_re) - jnp.einsum('blgp,ghp->blgh', s_im, c_im)
    y = y.reshape(bsz, L, S5_WIDTH) + d * u
    y = jax.nn.gelu(y)
    return y * jax.nn.sigmoid(y @ w_glu + b_glu)


def _mla_mixer(cq, ckv, kr, cos, sin, q_norm, w_uq, kv_norm, w_ukv):
    bsz, L, _ = cq.shape
    q = (_rmsnorm(cq, q_norm) @ w_uq).reshape(bsz, L, MLA_HEADS, MLA_NOPE + MLA_ROPE)
    q_nope = q[..., :MLA_NOPE]
    q_pe = _apply_rope(q[..., MLA_NOPE:], cos[:, :, None, :], sin[:, :, None, :])
    kv = (_rmsnorm(ckv, kv_norm) @ w_ukv).reshape(bsz, L, MLA_HEADS, MLA_NOPE + MLA_V)
    k_nope = kv[..., :MLA_NOPE]
    v = kv[..., MLA_NOPE:]
    k_pe = _apply_rope(kr, cos, sin)
    scale = (MLA_NOPE + MLA_ROPE) ** -0.5
    n_blk = L // Q_BLOCK
    k_chunk = jnp.arange(L) // CHUNK

    def q_block(args):
        qn, qp, blk = args
        s = jnp.einsum('bqhd,bkhd->bhqk', qn, k_nope) + jnp.einsum('bqhr,bkr->bhqk', qp, k_pe)
        s = s.astype(jnp.float32) * scale
        q_chunk = (blk * Q_BLOCK + jnp.arange(Q_BLOCK)) // CHUNK
        mask = k_chunk[None, :] <= q_chunk[:, None]
        s = jnp.where(mask[None, None], s, MASK_VALUE)
        p = jax.nn.softmax(s, axis=-1).astype(v.dtype)
        return jnp.einsum('bhqk,bkhd->bqhd', p, v)

    qn_b = q_nope.reshape(bsz, n_blk, Q_BLOCK, MLA_HEADS, MLA_NOPE).transpose(1, 0, 2, 3, 4)
    qp_b = q_pe.reshape(bsz, n_blk, Q_BLOCK, MLA_HEADS, MLA_ROPE).transpose(1, 0, 2, 3, 4)
    out = lax.map(q_block, (qn_b, qp_b, jnp.arange(n_blk)))
    return out.transpose(1, 0, 2, 3, 4).reshape(bsz, L, MLA_WIDTH)


def _peer(h, w_query, keys1, keys2, u_tab, v_tab):
    bsz, L, dm = h.shape
    hb = h.reshape((bsz * L) // PEER_BLOCK, PEER_BLOCK, dm)

    def token_block(ht):
        q = (ht @ w_query).reshape(PEER_BLOCK, PEER_HEADS, PEER_QDIM)
        s1 = jnp.einsum('thd,hkd->thk', q[..., :PEER_HALF], keys1)
        s2 = jnp.einsum('thd,hkd->thk', q[..., PEER_HALF:], keys2)
        v1, i1 = lax.top_k(s1, PEER_TOPK)
        v2, i2 = lax.top_k(s2, PEER_TOPK)
        cand = (v1[..., :, None] + v2[..., None, :]).reshape(PEER_BLOCK, PEER_HEADS, PEER_TOPK * PEER_TOPK)
        top_s, top_c = lax.top_k(cand, PEER_TOPK)
        e1 = jnp.take_along_axis(i1, top_c // PEER_TOPK, axis=-1)
        e2 = jnp.take_along_axis(i2, top_c % PEER_TOPK, axis=-1)
        expert = e1 * PEER_NKEYS + e2
        g = jax.nn.softmax(top_s.astype(jnp.float32), axis=-1).astype(ht.dtype)
        act = jax.nn.gelu(jnp.einsum('thkd,td->thk', u_tab[expert], ht))
        return jnp.einsum('thk,thkd->td', g * act, v_tab[expert])

    return lax.map(token_block, hb).reshape(bsz, L, dm)


def setup_inputs(seed: int = 0) -> dict:
    key = jax.random.key(seed)
    ks = jax.random.split(key, 40)
    nrm = jax.random.normal
    f32 = jnp.float32
    x = nrm(ks[0], (BATCH, SEQ, D_MODEL), f32)
    c = nrm(ks[1], (BATCH, D_MODEL), f32)
    offset = jax.random.randint(ks[2], (BATCH, 1), 0, 4096, dtype=jnp.int32)
    positions = (offset + jnp.arange(SEQ, dtype=jnp.int32)[None, :]).astype(jnp.int32)
    w_ada = nrm(ks[3], (DEPTH, D_MODEL, 6 * D_MODEL), f32) * (0.5 * D_MODEL ** -0.5)
    b_ada = 0.02 * nrm(ks[4], (DEPTH, 6 * D_MODEL), f32)
    w_in = nrm(ks[5], (DEPTH, D_MODEL, D_IN), f32) * D_MODEL ** -0.5
    s5_lambda_re = -0.5 + 0.01 * nrm(ks[6], (DEPTH, S5_GROUPS, S5_STATE), f32)
    s5_lambda_im = jnp.broadcast_to(math.pi * jnp.arange(S5_STATE, dtype=f32), (DEPTH, S5_GROUPS, S5_STATE)) + 0.0
    s5_log_dt = jax.random.uniform(ks[7], (DEPTH, S5_GROUPS), f32, math.log(S5_DT_MIN), math.log(S5_DT_MAX))
    s5_b_re = nrm(ks[8], (DEPTH, S5_GROUPS, S5_STATE, S5_GROUP), f32) * (2.0 * S5_GROUP) ** -0.5
    s5_b_im = nrm(ks[9], (DEPTH, S5_GROUPS, S5_STATE, S5_GROUP), f32) * (2.0 * S5_GROUP) ** -0.5
    s5_c_re = nrm(ks[10], (DEPTH, S5_GROUPS, S5_GROUP, S5_STATE), f32) * (2.0 * S5_STATE) ** -0.5
    s5_c_im = nrm(ks[11], (DEPTH, S5_GROUPS, S5_GROUP, S5_STATE), f32) * (2.0 * S5_STATE) ** -0.5
    s5_d = nrm(ks[12], (DEPTH, S5_WIDTH), f32)
    s5_w_glu = nrm(ks[13], (DEPTH, S5_WIDTH, S5_WIDTH), f32) * S5_WIDTH ** -0.5
    s5_b_glu = 0.02 * nrm(ks[14], (DEPTH, S5_WIDTH), f32)
    mla_q_norm = 1.0 + 0.02 * nrm(ks[15], (DEPTH, Q_LORA), f32)
    mla_w_uq = nrm(ks[16], (DEPTH, Q_LORA, MLA_HEADS * (MLA_NOPE + MLA_ROPE)), f32) * Q_LORA ** -0.5
    mla_kv_norm = 1.0 + 0.02 * nrm(ks[17], (DEPTH, KV_LORA), f32)
    mla_w_ukv = nrm(ks[18], (DEPTH, KV_LORA, MLA_HEADS * (MLA_NOPE + MLA_V)), f32) * KV_LORA ** -0.5
    gn_s5 = 1.0 + 0.02 * nrm(ks[19], (DEPTH, S5_WIDTH), f32)
    gn_mla = 1.0 + 0.02 * nrm(ks[20], (DEPTH, MLA_WIDTH), f32)
    w_out = nrm(ks[21], (DEPTH, D_MIX, D_MODEL), f32) * (D_MIX ** -0.5 * DEEPNORM_BETA)
    ln1_g = 1.0 + 0.02 * nrm(ks[22], (DEPTH, D_MODEL), f32)
    ln1_b = 0.02 * nrm(ks[23], (DEPTH, D_MODEL), f32)
    peer_w_query = nrm(ks[24], (DEPTH, D_MODEL, PEER_HEADS * PEER_QDIM), f32) * D_MODEL ** -0.5
    peer_keys1 = nrm(ks[25], (DEPTH, PEER_HEADS, PEER_NKEYS, PEER_HALF), f32) * PEER_HALF ** -0.5
    peer_keys2 = nrm(ks[26], (DEPTH, PEER_HEADS, PEER_NKEYS, PEER_HALF), f32) * PEER_HALF ** -0.5
    peer_u = nrm(ks[27], (DEPTH, PEER_EXPERTS, D_MODEL), f32) * D_MODEL ** -0.5
    peer_v = nrm(ks[28], (DEPTH, PEER_EXPERTS, D_MODEL), f32) * DEEPNORM_BETA
    ln2_g = 1.0 + 0.02 * nrm(ks[29], (DEPTH, D_MODEL), f32)
    ln2_b = 0.02 * nrm(ks[30], (DEPTH, D_MODEL), f32)
    return {"x": x, "c": c, "positions": positions, "w_ada": w_ada, "b_ada": b_ada, "w_in": w_in,
            "s5_lambda_re": s5_lambda_re, "s5_lambda_im": s5_lambda_im, "s5_log_dt": s5_log_dt,
            "s5_b_re": s5_b_re, "s5_b_im": s5_b_im, "s5_c_re": s5_c_re, "s5_c_im": s5_c_im,
            "s5_d": s5_d, "s5_w_glu": s5_w_glu, "s5_b_glu": s5_b_glu,
            "mla_q_norm": mla_q_norm, "mla_w_uq": mla_w_uq, "mla_kv_norm": mla_kv_norm, "mla_w_ukv": mla_w_ukv,
            "gn_s5": gn_s5, "gn_mla": gn_mla, "w_out": w_out, "ln1_g": ln1_g, "ln1_b": ln1_b,
            "peer_w_query": peer_w_query, "peer_keys1": peer_keys1, "peer_keys2": peer_keys2,
            "peer_u": peer_u, "peer_v": peer_v, "ln2_g": ln2_g, "ln2_b": ln2_b}


def reference(x, c, positions, w_ada, b_ada, w_in,
              s5_lambda_re, s5_lambda_im, s5_log_dt, s5_b_re, s5_b_im, s5_c_re, s5_c_im,
              s5_d, s5_w_glu, s5_b_glu,
              mla_q_norm, mla_w_uq, mla_kv_norm, mla_w_ukv,
              gn_s5, gn_mla, w_out, ln1_g, ln1_b,
              peer_w_query, peer_keys1, peer_keys2, peer_u, peer_v, ln2_g, ln2_b):
    cos, sin = _rope_angles(positions)
    c_act = jax.nn.silu(c)
    splits = [S5_WIDTH, S5_WIDTH + Q_LORA, S5_WIDTH + Q_LORA + KV_LORA]
    for l in range(DEPTH):
        mod = c_act @ w_ada[l] + b_ada[l]
        sh1, sc1, g1, sh2, sc2, g2 = jnp.split(mod[:, None, :], 6, axis=-1)
        h = x * (1.0 + sc1) + sh1
        proj = h @ w_in[l]
        u, cq, ckv, kr = jnp.split(proj, splits, axis=-1)
        y_s5 = _s5_mixer(u, s5_lambda_re[l], s5_lambda_im[l], s5_log_dt[l], s5_b_re[l], s5_b_im[l],
                         s5_c_re[l], s5_c_im[l], s5_d[l], s5_w_glu[l], s5_b_glu[l])
        y_mla = _mla_mixer(cq, ckv, kr, cos, sin, mla_q_norm[l], mla_w_uq[l], mla_kv_norm[l], mla_w_ukv[l])
        y_mix = jnp.concatenate([_rmsnorm(y_s5, gn_s5[l]), _rmsnorm(y_mla, gn_mla[l])], axis=-1) @ w_out[l]
        x = _layernorm(DEEPNORM_ALPHA * x + (1.0 + g1) * y_mix, ln1_g[l], ln1_b[l])
        h = x * (1.0 + sc2) + sh2
        y_ff = _peer(h, peer_w_query[l], peer_keys1[l], peer_keys2[l], peer_u[l], peer_v[l])
        x = _layernorm(DEEPNORM_ALPHA * x + (1.0 + g2) * y_ff, ln2_g[l], ln2_b[l])
    return x
```

```python
import functools
import math

import jax
import jax.numpy as jnp
from jax import lax
from jax.experimental import pallas as pl
from jax.experimental.pallas import tpu as pltpu

F32 = jnp.float32
BF16 = jnp.bfloat16

D_MODEL = 1024
CHUNK = 64
S5_WIDTH = 512
S5_GROUP = 16
S5_GROUPS = 32
S5_STATE = 64
S5_COLS = S5_GROUPS * S5_STATE
MLA_WIDTH = 512
MLA_HEADS = 8
MLA_NOPE = 64
MLA_ROPE = 32
MLA_V = 64
Q_LORA = 384
KV_LORA = 256
ROPE_THETA = 10000.0
HEAD_PAD = 128
PEER_HEADS = 8
PEER_NKEYS = 128
PEER_QDIM = 256
PEER_HALF = 128
PEER_TOPK = 16
PEER_EXPERTS = PEER_NKEYS * PEER_NKEYS
DEEPNORM_ALPHA = 2.0 ** 0.25
NORM_EPS = 1e-6
MASK_VALUE = -1e30
NEG_BIG = -3.0e38

LANES = 128
SUBLANES = 8
VMEM_LIMIT = 56 * 1024 * 1024


def _gelu(x):
    return 0.5 * x * (1.0 + jnp.tanh(0.7978845608028654 * (x + 0.044715 * (x * x * x))))


def _sigmoid(x):
    return 1.0 / (1.0 + jnp.exp(-x))


def _params(sem):
    return pltpu.CompilerParams(dimension_semantics=sem, vmem_limit_bytes=VMEM_LIMIT)


def _ada_kernel(c_ref, w_ref, b_ref, o_ref):
    c = c_ref[...]
    ca = c * _sigmoid(c)
    o_ref[...] = jnp.dot(ca, w_ref[...], preferred_element_type=F32,
                         precision=lax.Precision.HIGHEST) + b_ref[...]


def _ada(c, w_ada, b_ada):
    bsz, dm = c.shape
    n = w_ada.shape[1]
    tn = 1024
    return pl.pallas_call(
        _ada_kernel,
        grid=(n // tn,),
        in_specs=[pl.BlockSpec((bsz, dm), lambda j: (0, 0)),
                  pl.BlockSpec((dm, tn), lambda j: (0, j)),
                  pl.BlockSpec((1, tn), lambda j: (0, j))],
        out_specs=pl.BlockSpec((bsz, tn), lambda j: (0, j)),
        out_shape=jax.ShapeDtypeStruct((bsz, n), F32),
        compiler_params=_params(("arbitrary",)),
        name="ada",
    )(c, w_ada, b_ada.reshape(1, n))


def _proj_kernel(x_ref, pos_ref, sc_ref, sh_ref, win_ref, wqa_ref, wqb_ref, wk_ref, wv_ref,
                 invf_ref, sgn_ref, u_ref, q_ref, k_ref, v_ref):
    x = x_ref[0]
    h = (x * (1.0 + sc_ref[0]) + sh_ref[0]).astype(BF16)
    proj = jnp.dot(h, win_ref[...], preferred_element_type=F32)
    o = S5_WIDTH
    u_ref[0] = proj[:, :o]
    cq = proj[:, o:o + Q_LORA]
    o += Q_LORA
    ckv = proj[:, o:o + KV_LORA]
    o += KV_LORA
    kra = proj[:, o:o + HEAD_PAD]
    krb = proj[:, o + HEAD_PAD:o + 2 * HEAD_PAD]
    ang = pos_ref[0] * invf_ref[...]
    cs = jnp.cos(ang)
    sn = jnp.sin(ang) * sgn_ref[...]
    cqn = (cq * lax.rsqrt(jnp.mean(cq * cq, axis=-1, keepdims=True) + NORM_EPS)).astype(BF16)
    qa = jnp.dot(cqn, wqa_ref[...], preferred_element_type=F32)
    qb = jnp.dot(cqn, wqb_ref[...], preferred_element_type=F32)
    ckvn = (ckv * lax.rsqrt(jnp.mean(ckv * ckv, axis=-1, keepdims=True) + NORM_EPS)).astype(BF16)
    kn = jnp.dot(ckvn, wk_ref[...], preferred_element_type=F32)
    kp = kra * cs + krb * sn
    for hh in range(MLA_HEADS):
        sl = slice(hh * HEAD_PAD, (hh + 1) * HEAD_PAD)
        q_ref[0, :, sl] = (qa[:, sl] * cs + qb[:, sl] * sn).astype(BF16)
        k_ref[0, :, sl] = (kn[:, sl] + kp).astype(BF16)
    v_ref[0] = jnp.dot(ckvn, wv_ref[...], preferred_element_type=F32).astype(BF16)


def _proj(x, posf, sc1, sh1, win, wqa, wqb, wk, wv, invf, sgn, tt):
    bsz, L, dm = x.shape
    hp = MLA_HEADS * HEAD_PAD
    full = lambda a: pl.BlockSpec(a.shape, lambda b, i: (0,) * a.ndim)
    return pl.pallas_call(
        _proj_kernel,
        grid=(bsz, L // tt),
        in_specs=[pl.BlockSpec((1, tt, dm), lambda b, i: (b, i, 0)),
                  pl.BlockSpec((1, tt, 1), lambda b, i: (b, i, 0)),
                  pl.BlockSpec((1, 1, dm), lambda b, i: (b, 0, 0)),
                  pl.BlockSpec((1, 1, dm), lambda b, i: (b, 0, 0)),
                  full(win), full(wqa), full(wqb), full(wk), full(wv), full(invf), full(sgn)],
        out_specs=[pl.BlockSpec((1, tt, S5_WIDTH), lambda b, i: (b, i, 0)),
                   pl.BlockSpec((1, tt, hp), lambda b, i: (b, i, 0)),
                   pl.BlockSpec((1, tt, hp), lambda b, i: (b, i, 0)),
                   pl.BlockSpec((1, tt, MLA_WIDTH), lambda b, i: (b, i, 0))],
        out_shape=[jax.ShapeDtypeStruct((bsz, L, S5_WIDTH), F32),
                   jax.ShapeDtypeStruct((bsz, L, hp), BF16),
                   jax.ShapeDtypeStruct((bsz, L, hp), BF16),
                   jax.ShapeDtypeStruct((bsz, L, MLA_WIDTH), BF16)],
        compiler_params=_params(("parallel", "parallel")),
        name="proj",
    )(x, posf, sc1, sh1, win, wqa, wqb, wk, wv, invf, sgn)


def _s5_kernel(u_ref, wb_ref, wc_ref, are_ref, aim_ref, d_ref, wglu_ref, bglu_ref, gn_ref,
               y_ref, utm, st, yt, carry, *, tc, bsz):
    @pl.when(pl.program_id(0) == 0)
    def _():
        carry[...] = jnp.zeros_like(carry)

    nslab = S5_WIDTH // LANES
    for b in range(bsz):
        for j in range(nslab):
            utm[j, pl.ds(b, tc, stride=bsz), :] = u_ref[b, :, j * LANES:(j + 1) * LANES]
    u = jnp.concatenate([utm[j] for j in range(nslab)], axis=-1)
    st[...] = jnp.dot(u.astype(BF16), wb_ref[...], preferred_element_type=F32)

    cb = 512
    for c0 in range(0, S5_COLS, cb):
        cre = slice(c0, c0 + cb)
        cim = slice(S5_COLS + c0, S5_COLS + c0 + cb)
        are = are_ref[:, cre]
        aim = aim_ref[:, cre]

        def step(t, sc, cre=cre, cim=cim, are=are, aim=aim):
            sre, sim = sc
            r = pl.multiple_of(t * bsz, bsz)
            nre = are * sre - aim * sim + st[pl.ds(r, bsz), cre]
            nim = are * sim + aim * sre + st[pl.ds(r, bsz), cim]
            st[pl.ds(r, bsz), cre] = nre
            st[pl.ds(r, bsz), cim] = nim
            return nre, nim

        sre, sim = lax.fori_loop(0, tc, step, (carry[:, cre], carry[:, cim]), unroll=4)
        carry[:, cre] = sre
        carry[:, cim] = sim

    y = jnp.dot(st[...].astype(BF16), wc_ref[...], preferred_element_type=F32) + d_ref[...] * u
    y = _gelu(y)
    z = jnp.dot(y.astype(BF16), wglu_ref[...], preferred_element_type=F32) + bglu_ref[...]
    y = y * _sigmoid(z)
    y = y * lax.rsqrt(jnp.mean(y * y, axis=-1, keepdims=True) + NORM_EPS) * gn_ref[...]
    for j in range(nslab):
        yt[j] = y[:, j * LANES:(j + 1) * LANES]
    for b in range(bsz):
        for j in range(nslab):
            y_ref[b, :, j * LANES:(j + 1) * LANES] = yt[j, pl.ds(b, tc, stride=bsz), :].astype(BF16)


def _s5(u, wb, wc, are, aim, d, wglu, bglu, gn, tc):
    bsz, L, w = u.shape
    full = lambda a: pl.BlockSpec(a.shape, lambda i: (0,) * a.ndim)
    return pl.pallas_call(
        functools.partial(_s5_kernel, tc=tc, bsz=bsz),
        grid=(L // tc,),
        in_specs=[pl.BlockSpec((bsz, tc, w), lambda i: (0, i, 0)),
                  full(wb), full(wc), full(are), full(aim), full(d), full(wglu), full(bglu), full(gn)],
        out_specs=pl.BlockSpec((bsz, tc, w), lambda i: (0, i, 0)),
        out_shape=jax.ShapeDtypeStruct((bsz, L, w), BF16),
        scratch_shapes=[pltpu.VMEM((w // LANES, tc * bsz, LANES), F32),
                        pltpu.VMEM((tc * bsz, 2 * S5_COLS), F32),
                        pltpu.VMEM((w // LANES, tc * bsz, LANES), F32),
                        pltpu.VMEM((bsz, 2 * S5_COLS), F32)],
        compiler_params=_params(("arbitrary",)),
        name="s5",
    )(u, wb, wc, are, aim, d, wglu, bglu, gn)


def _attn_kernel(q_ref, k_ref, v_ref, o_ref, *, tq):
    qi = pl.program_id(2)
    nt = (((1,), (1,)), ((), ()))
    row = lax.broadcasted_iota(jnp.int32, (tq, tq), 0) // CHUNK
    col = lax.broadcasted_iota(jnp.int32, (tq, tq), 1) // CHUNK
    diag_mask = col <= row
    outs = []
    for hh in range(2):
        hs = slice(hh * HEAD_PAD, (hh + 1) * HEAD_PAD)
        q = q_ref[0, :, hs]

        def block(kb, carry, masked, q=q, hs=hs):
            m, l, acc = carry
            ks = pl.multiple_of(kb * tq, tq)
            k = k_ref[0, pl.ds(ks, tq), hs]
            v = v_ref[0, pl.ds(ks, tq), :]
            s = lax.dot_general(q, k, nt, preferred_element_type=F32)
            if masked:
                s = jnp.where(diag_mask, s, MASK_VALUE)
            m_new = jnp.maximum(m, jnp.max(s, axis=-1, keepdims=True))
            alpha = jnp.exp(m - m_new)
            p = jnp.exp(s - m_new)
            l = alpha * l + jnp.sum(p, axis=-1, keepdims=True)
            acc = alpha * acc + jnp.dot(p.astype(BF16), v, preferred_element_type=F32)
            return m_new, l, acc

        init = (jnp.full((tq, 1), MASK_VALUE, F32), jnp.zeros((tq, 1), F32),
                jnp.zeros((tq, 2 * MLA_V), F32))
        carry = lax.fori_loop(0, qi, functools.partial(block, masked=False), init)
        m, l, acc = block(qi, carry, True)
        outs.append(acc / l)
    lane = lax.broadcasted_iota(jnp.int32, (tq, 2 * MLA_V), 1)
    o_ref[0] = jnp.where(lane < MLA_V, outs[0], outs[1]).astype(BF16)


def _attn(q, k, v, tq):
    bsz, L, hp = q.shape
    return pl.pallas_call(
        functools.partial(_attn_kernel, tq=tq),
        grid=(bsz, MLA_HEADS // 2, L // tq),
        in_specs=[pl.BlockSpec((1, tq, 2 * HEAD_PAD), lambda b, h, i: (b, i, h)),
                  pl.BlockSpec((1, L, 2 * HEAD_PAD), lambda b, h, i: (b, 0, h)),
                  pl.BlockSpec((1, L, 2 * MLA_V), lambda b, h, i: (b, 0, h))],
        out_specs=pl.BlockSpec((1, tq, 2 * MLA_V), lambda b, h, i: (b, i, h)),
        out_shape=jax.ShapeDtypeStruct((bsz, L, MLA_WIDTH), BF16),
        compiler_params=_params(("parallel", "parallel", "arbitrary")),
        name="attn",
    )(q, k, v)


def _top16_rows(s):
    work = s
    rank = jnp.full(s.shape, 127.0, F32)
    ridx = lax.broadcasted_iota(jnp.int32, (PEER_TOPK, s.shape[1]), 0)
    vals = jnp.zeros((PEER_TOPK, s.shape[1]), F32)
    for r in range(PEER_TOPK):
        m = jnp.max(work, axis=0, keepdims=True)
        hit = work == m
        rank = jnp.where(hit, float(r), rank)
        work = jnp.where(hit, NEG_BIG, work)
        vals = jnp.where(ridx == r, m, vals)
    return vals, rank


def _mix_kernel(x_ref, ys_ref, ym_ref, g1_ref, sc2_ref, sh2_ref, gnm_ref, wo1_ref, wo2_ref,
                l1g_ref, l1b_ref, wq_ref, k1_ref, k2_ref,
                x1_ref, h2t_ref, cnt_ref, e1_ref, rk2_ref, e2_ref):
    x = x_ref[0]
    ym = ym_ref[0].astype(F32)
    ym = (ym * lax.rsqrt(jnp.mean(ym * ym, axis=-1, keepdims=True) + NORM_EPS) * gnm_ref[...]).astype(BF16)
    ymix = (jnp.dot(ys_ref[0], wo1_ref[...], preferred_element_type=F32)
            + jnp.dot(ym, wo2_ref[...], preferred_element_type=F32))
    r = DEEPNORM_ALPHA * x + (1.0 + g1_ref[0]) * ymix
    mu = jnp.mean(r, axis=-1, keepdims=True)
    rc = r - mu
    var = jnp.mean(rc * rc, axis=-1, keepdims=True)
    x1 = rc * lax.rsqrt(var + NORM_EPS) * l1g_ref[...] + l1b_ref[...]
    x1_ref[0] = x1
    h2 = x1 * (1.0 + sc2_ref[0]) + sh2_ref[0]
    h2t = h2.T.astype(BF16)
    h2t_ref[...] = h2t
    qt = jnp.dot(wq_ref[...], h2t, preferred_element_type=F32)
    for hh in range(PEER_HEADS):
        q1 = qt[hh * PEER_QDIM:hh * PEER_QDIM + PEER_HALF]
        q2 = qt[hh * PEER_QDIM + PEER_HALF:(hh + 1) * PEER_QDIM]
        s1 = jnp.dot(k1_ref[hh], q1, preferred_element_type=F32, precision=lax.Precision.HIGHEST)
        s2 = jnp.dot(k2_ref[hh], q2, preferred_element_type=F32, precision=lax.Precision.HIGHEST)
        v1, rk1 = _top16_rows(s1)
        v2, rk2 = _top16_rows(s2)
        cands = [v1[0:1] + v2]
        for a in range(1, 8):
            cands.append(v1[a:a + 1] + v2[0:8])
        cands.append(v1[8:16] + v2[0:1])
        top, _ = _top16_rows(jnp.concatenate(cands, axis=0))
        theta = top[PEER_TOPK - 1:PEER_TOPK]
        z = jnp.sum(jnp.exp(top - top[0:1]), axis=0, keepdims=True)
        cnt = jnp.zeros_like(s1)
        for b in range(PEER_TOPK):
            cnt = cnt + jnp.where(s1 + v2[b:b + 1] >= theta, 1.0, 0.0)
        cnt_ref[hh] = jnp.where(rk1 < float(PEER_TOPK), cnt, 0.0)
        e1_ref[hh] = jnp.exp(s1 - v1[0:1])
        rk2_ref[hh] = rk2
        e2_ref[hh] = jnp.exp(s2 - v2[0:1]) / z


def _mix(x, ys, ym, g1, sc2, sh2, gnm, wo1, wo2, l1g, l1b, wqt, k1, k2, tt):
    bsz, L, dm = x.shape
    n = bsz * L
    nl = L // tt
    full = lambda a: pl.BlockSpec(a.shape, lambda b, i: (0,) * a.ndim)
    tok = lambda w: pl.BlockSpec((1, tt, w), lambda b, i: (b, i, 0))
    vec = pl.BlockSpec((1, 1, dm), lambda b, i: (b, 0, 0))
    tab = pl.BlockSpec((PEER_HEADS, PEER_NKEYS, tt), lambda b, i: (0, 0, b * nl + i))
    tab_shape = jax.ShapeDtypeStruct((PEER_HEADS, PEER_NKEYS, n), F32)
    return pl.pallas_call(
        _mix_kernel,
        grid=(bsz, nl),
        in_specs=[tok(dm), tok(S5_WIDTH), tok(MLA_WIDTH), vec, vec, vec,
                  full(gnm), full(wo1), full(wo2), full(l1g), full(l1b), full(wqt), full(k1), full(k2)],
        out_specs=[tok(dm), pl.BlockSpec((dm, tt), lambda b, i: (0, b * nl + i)), tab, tab, tab, tab],
        out_shape=[jax.ShapeDtypeStruct((bsz, L, dm), F32), jax.ShapeDtypeStruct((dm, n), BF16),
                   tab_shape, tab_shape, tab_shape, tab_shape],
        compiler_params=_params(("parallel", "parallel")),
        name="mix",
    )(x, ys, ym, g1, sc2, sh2, gnm, wo1, wo2, l1g, l1b, wqt, k1, k2)


def _peer_kernel(h2t_ref, cnt_ref, e1_ref, rk2_ref, e2_ref, u_ref, vt_ref, x1_ref, g2_ref, l2g_ref, l2b_ref,
                 o_ref, acc, wa, *, ni):
    eb = pl.program_id(1)

    @pl.when(eb == 0)
    def _():
        acc[...] = jnp.zeros_like(acc)

    sc = jnp.dot(u_ref[...], h2t_ref[...], preferred_element_type=F32)
    for ii in range(ni):
        i = eb * ni + ii
        w = None
        for hh in range(PEER_HEADS):
            c = cnt_ref[hh, pl.ds(i, 1), :]
            e = e1_ref[hh, pl.ds(i, 1), :]
            term = jnp.where(rk2_ref[hh] < c, e2_ref[hh], 0.0) * e
            w = term if w is None else w + term
        rows = slice(ii * PEER_NKEYS, (ii + 1) * PEER_NKEYS)
        wa[rows, :] = (w * _gelu(sc[rows])).astype(BF16)
    acc[...] += jnp.dot(vt_ref[...], wa[...], preferred_element_type=F32)

    @pl.when(eb == pl.num_programs(1) - 1)
    def _():
        yff = acc[...].T
        r = DEEPNORM_ALPHA * x1_ref[0] + (1.0 + g2_ref[0]) * yff
        mu = jnp.mean(r, axis=-1, keepdims=True)
        rc = r - mu
        var = jnp.mean(rc * rc, axis=-1, keepdims=True)
        o_ref[0] = rc * lax.rsqrt(var + NORM_EPS) * l2g_ref[...] + l2b_ref[...]


def _peer(h2t, cnt, e1, rk2, e2, ub, vt, x1, g2, l2g, l2b, tt, ni):
    bsz, L, dm = x1.shape
    nl = L // tt
    eb = ni * PEER_NKEYS
    tab = pl.BlockSpec((PEER_HEADS, PEER_NKEYS, tt), lambda t, e: (0, 0, t))
    full = lambda a: pl.BlockSpec(a.shape, lambda t, e: (0,) * a.ndim)
    return pl.pallas_call(
        functools.partial(_peer_kernel, ni=ni),
        grid=(bsz * nl, PEER_NKEYS // ni),
        in_specs=[pl.BlockSpec((dm, tt), lambda t, e: (0, t)), tab, tab, tab, tab,
                  pl.BlockSpec((eb, dm), lambda t, e: (e, 0)),
                  pl.BlockSpec((dm, eb), lambda t, e: (0, e)),
                  pl.BlockSpec((1, tt, dm), lambda t, e: (t // nl, t % nl, 0)),
                  pl.BlockSpec((1, 1, dm), lambda t, e: (t // nl, 0, 0)),
                  full(l2g), full(l2b)],
        out_specs=pl.BlockSpec((1, tt, dm), lambda t, e: (t // nl, t % nl, 0)),
        out_shape=jax.ShapeDtypeStruct((bsz, L, dm), F32),
        scratch_shapes=[pltpu.VMEM((dm, tt), F32), pltpu.VMEM((eb, tt), BF16)],
        compiler_params=_params(("parallel", "arbitrary")),
        name="peer",
    )(h2t, cnt, e1, rk2, e2, ub, vt, x1, g2, l2g, l2b)


def _s5_discretize(lam_re, lam_im, log_dt, b_re, b_im, c_re, c_im):
    dt = jnp.exp(log_dt)[:, None]
    mag = jnp.exp(lam_re * dt)
    ab_re = mag * jnp.cos(lam_im * dt)
    ab_im = mag * jnp.sin(lam_im * dt)
    den = lam_re * lam_re + lam_im * lam_im
    nr = ab_re - 1.0
    coef_re = (nr * lam_re + ab_im * lam_im) / den
    coef_im = (ab_im * lam_re - nr * lam_im) / den
    bb_re = coef_re[..., None] * b_re - coef_im[..., None] * b_im
    bb_im = coef_re[..., None] * b_im + coef_im[..., None] * b_re
    eye = jnp.eye(S5_GROUPS, dtype=F32)
    wb_re = jnp.einsum("gph,gk->ghkp", bb_re, eye).reshape(S5_WIDTH, S5_COLS)
    wb_im = jnp.einsum("gph,gk->ghkp", bb_im, eye).reshape(S5_WIDTH, S5_COLS)
    wb = jnp.concatenate([wb_re, wb_im], axis=1).astype(BF16)
    wc_re = jnp.einsum("ghp,gk->gpkh", c_re, eye).reshape(S5_COLS, S5_WIDTH)
    wc_im = jnp.einsum("ghp,gk->gpkh", c_im, eye).reshape(S5_COLS, S5_WIDTH)
    wc = jnp.concatenate([wc_re, -wc_im], axis=0).astype(BF16)
    return wb, wc, ab_re.reshape(1, S5_COLS), ab_im.reshape(1, S5_COLS)


def _pad_heads(w, width, offset):
    kdim = w.shape[0]
    w = w.reshape(kdim, MLA_HEADS, width)
    out = jnp.zeros((kdim, MLA_HEADS, HEAD_PAD), w.dtype)
    out = out.at[:, :, offset:offset + width].set(w)
    return out.reshape(kdim, MLA_HEADS * HEAD_PAD)


def _swap_halves(w):
    half = w.shape[-1] // 2
    return jnp.concatenate([w[..., half:], w[..., :half]], axis=-1)


def kernel(x, c, positions, w_ada, b_ada, w_in, s5_lambda_re, s5_lambda_im, s5_log_dt, s5_b_re, s5_b_im, s5_c_re, s5_c_im, s5_d, s5_w_glu, s5_b_glu, mla_q_norm, mla_w_uq, mla_kv_norm, mla_w_ukv, gn_s5, gn_mla, w_out, ln1_g, ln1_b, peer_w_query, peer_keys1, peer_keys2, peer_u, peer_v, ln2_g, ln2_b):
    bsz, L, dm = x.shape
    depth = w_ada.shape[0]
    t_proj = min(512, L)
    t_s5 = min(128, L)
    t_attn = min(256, L)
    t_mix = min(256, L)
    t_peer = min(512, L)
    peer_ni = 4

    inv_freq = ROPE_THETA ** (-jnp.arange(0, MLA_ROPE, 2, dtype=F32) / MLA_ROPE)
    half = MLA_ROPE // 2
    invf = jnp.zeros((1, HEAD_PAD), F32).at[0, MLA_NOPE:MLA_NOPE + half].set(inv_freq)
    invf = invf.at[0, MLA_NOPE + half:MLA_NOPE + MLA_ROPE].set(inv_freq)
    sgn = jnp.zeros((1, HEAD_PAD), F32).at[0, MLA_NOPE:MLA_NOPE + half].set(-1.0)
    sgn = sgn.at[0, MLA_NOPE + half:MLA_NOPE + MLA_ROPE].set(1.0)
    posf = positions.astype(F32)[..., None]

    for l in range(depth):
        mod = _ada(c, w_ada[l], b_ada[l])
        sh1, sc1, g1, sh2, sc2, g2 = [m[:, None, :] for m in jnp.split(mod, 6, axis=-1)]

        wi = w_in[l]
        o1, o2, o3 = S5_WIDTH, S5_WIDTH + Q_LORA, S5_WIDTH + Q_LORA + KV_LORA
        w_kr = wi[:, o3:]
        zpad = lambda w: jnp.zeros((dm, HEAD_PAD), F32).at[:, MLA_NOPE:MLA_NOPE + MLA_ROPE].set(w)
        win = jnp.concatenate([wi[:, :o3], zpad(w_kr), zpad(_swap_halves(w_kr))], axis=1).astype(BF16)
        scale = (MLA_NOPE + MLA_ROPE) ** -0.5
        wuq = (mla_w_uq[l] * mla_q_norm[l][:, None] * scale).reshape(Q_LORA, MLA_HEADS, MLA_NOPE + MLA_ROPE)
        wuq_main = wuq.reshape(Q_LORA, MLA_HEADS * (MLA_NOPE + MLA_ROPE))
        wqa = _pad_heads(wuq_main, MLA_NOPE + MLA_ROPE, 0).astype(BF16)
        wuq_sw = _swap_halves(wuq[:, :, MLA_NOPE:]).reshape(Q_LORA, MLA_HEADS * MLA_ROPE)
        wqb = _pad_heads(wuq_sw, MLA_ROPE, MLA_NOPE).astype(BF16)
        wukv = (mla_w_ukv[l] * mla_kv_norm[l][:, None]).reshape(KV_LORA, MLA_HEADS, MLA_NOPE + MLA_V)
        wk = _pad_heads(wukv[:, :, :MLA_NOPE].reshape(KV_LORA, MLA_HEADS * MLA_NOPE), MLA_NOPE, 0).astype(BF16)
        wv = wukv[:, :, MLA_NOPE:].reshape(KV_LORA, MLA_HEADS * MLA_V).astype(BF16)

        u, q, k, v = _proj(x, posf, sc1, sh1, win, wqa, wqb, wk, wv, invf, sgn, t_proj)

        wb, wc, are, aim = _s5_discretize(s5_lambda_re[l], s5_lambda_im[l], s5_log_dt[l], s5_b_re[l],
                                          s5_b_im[l], s5_c_re[l], s5_c_im[l])
        are = jnp.broadcast_to(are, (bsz, S5_COLS))
        aim = jnp.broadcast_to(aim, (bsz, S5_COLS))
        y_s5 = _s5(u, wb, wc, are, aim, s5_d[l][None], s5_w_glu[l].astype(BF16), s5_b_glu[l][None],
                   gn_s5[l][None], t_s5)

        y_mla = _attn(q, k, v, t_attn)

        wo = w_out[l].astype(BF16)
        wqt = peer_w_query[l].T.astype(BF16)
        x1, h2t, cnt, e1, rk2, e2 = _mix(x, y_s5, y_mla, g1, sc2, sh2, gn_mla[l][None], wo[:S5_WIDTH],
                                         wo[S5_WIDTH:], ln1_g[l][None], ln1_b[l][None], wqt,
                                         peer_keys1[l], peer_keys2[l], t_mix)

        x = _peer(h2t, cnt, e1, rk2, e2, peer_u[l].astype(BF16), peer_v[l].T.astype(BF16), x1, g2,
                  ln2_g[l][None], ln2_b[l][None], t_peer, peer_ni)
    return x
```

```python
import functools
import math

import jax
import jax.numpy as jnp
from jax import lax
from jax.experimental import pallas as pl
from jax.experimental.pallas import tpu as pltpu

F32 = jnp.float32
BF16 = jnp.bfloat16

D_MODEL = 1024
CHUNK = 64
S5_WIDTH = 512
S5_GROUP = 16
S5_GROUPS = 32
S5_STATE = 64
S5_COLS = S5_GROUPS * S5_STATE
MLA_WIDTH = 512
MLA_HEADS = 8
MLA_NOPE = 64
MLA_ROPE = 32
MLA_V = 64
Q_LORA = 384
KV_LORA = 256
ROPE_THETA = 10000.0
HEAD_PAD = 128
PEER_HEADS = 8
PEER_NKEYS = 128
PEER_QDIM = 256
PEER_HALF = 128
PEER_TOPK = 16
PEER_EXPERTS = PEER_NKEYS * PEER_NKEYS
DEEPNORM_ALPHA = 2.0 ** 0.25
NORM_EPS = 1e-6
MASK_VALUE = -1e30
NEG_BIG = -3.0e38

LANES = 128
SUBLANES = 8
VMEM_LIMIT = 56 * 1024 * 1024


def _gelu(x):
    return 0.5 * x * (1.0 + jnp.tanh(0.7978845608028654 * (x + 0.044715 * (x * x * x))))


def _sigmoid(x):
    return 1.0 / (1.0 + jnp.exp(-x))


def _params(sem):
    return pltpu.CompilerParams(dimension_semantics=sem, vmem_limit_bytes=VMEM_LIMIT)


def _ada_kernel(c_ref, w_ref, b_ref, o_ref):
    c = c_ref[...]
    ca = c * _sigmoid(c)
    o_ref[...] = jnp.dot(ca, w_ref[...], preferred_element_type=F32,
                         precision=lax.Precision.HIGHEST) + b_ref[...]


def _ada(c, w_ada, b_ada):
    bsz, dm = c.shape
    n = w_ada.shape[1]
    tn = 1024
    return pl.pallas_call(
        _ada_kernel,
        grid=(n // tn,),
        in_specs=[pl.BlockSpec((bsz, dm), lambda j: (0, 0)),
                  pl.BlockSpec((dm, tn), lambda j: (0, j)),
                  pl.BlockSpec((1, tn), lambda j: (0, j))],
        out_specs=pl.BlockSpec((bsz, tn), lambda j: (0, j)),
        out_shape=jax.ShapeDtypeStruct((bsz, n), F32),
        compiler_params=_params(("arbitrary",)),
        name="ada",
    )(c, w_ada, b_ada.reshape(1, n))


def _proj_kernel(x_ref, pos_ref, sc_ref, sh_ref, win_ref, wqa_ref, wqb_ref, wk_ref, wv_ref,
                 invf_ref, sgn_ref, u_ref, q_ref, k_ref, v_ref):
    x = x_ref[0]
    h = (x * (1.0 + sc_ref[0]) + sh_ref[0]).astype(BF16)
    proj = jnp.dot(h, win_ref[...], preferred_element_type=F32)
    o = S5_WIDTH
    u_ref[0] = proj[:, :o]
    cq = proj[:, o:o + Q_LORA]
    o += Q_LORA
    ckv = proj[:, o:o + KV_LORA]
    o += KV_LORA
    kra = proj[:, o:o + HEAD_PAD]
    krb = proj[:, o + HEAD_PAD:o + 2 * HEAD_PAD]
    ang = pos_ref[0] * invf_ref[...]
    cs = jnp.cos(ang)
    sn = jnp.sin(ang) * sgn_ref[...]
    cqn = (cq * lax.rsqrt(jnp.mean(cq * cq, axis=-1, keepdims=True) + NORM_EPS)).astype(BF16)
    qa = jnp.dot(cqn, wqa_ref[...], preferred_element_type=F32)
    qb = jnp.dot(cqn, wqb_ref[...], preferred_element_type=F32)
    ckvn = (ckv * lax.rsqrt(jnp.mean(ckv * ckv, axis=-1, keepdims=True) + NORM_EPS)).astype(BF16)
    kn = jnp.dot(ckvn, wk_ref[...], preferred_element_type=F32)
    kp = kra * cs + krb * sn
    for hh in range(MLA_HEADS):
        sl = slice(hh * HEAD_PAD, (hh + 1) * HEAD_PAD)
        q_ref[0, :, sl] = (qa[:, sl] * cs + qb[:, sl] * sn).astype(BF16)
        k_ref[0, :, sl] = (kn[:, sl] + kp).astype(BF16)
    v_ref[0] = lax.dot_general(wv_ref[...], ckvn, (((1,), (1,)), ((), ())),
                               preferred_element_type=F32).astype(BF16)


def _proj(x, posf, sc1, sh1, win, wqa, wqb, wk, wv, invf, sgn, tt):
    bsz, L, dm = x.shape
    hp = MLA_HEADS * HEAD_PAD
    full = lambda a: pl.BlockSpec(a.shape, lambda b, i: (0,) * a.ndim)
    return pl.pallas_call(
        _proj_kernel,
        grid=(bsz, L // tt),
        in_specs=[pl.BlockSpec((1, tt, dm), lambda b, i: (b, i, 0)),
                  pl.BlockSpec((1, tt, 1), lambda b, i: (b, i, 0)),
                  pl.BlockSpec((1, 1, dm), lambda b, i: (b, 0, 0)),
                  pl.BlockSpec((1, 1, dm), lambda b, i: (b, 0, 0)),
                  full(win), full(wqa), full(wqb), full(wk), full(wv), full(invf), full(sgn)],
        out_specs=[pl.BlockSpec((1, tt, S5_WIDTH), lambda b, i: (b, i, 0)),
                   pl.BlockSpec((1, tt, hp), lambda b, i: (b, i, 0)),
                   pl.BlockSpec((1, tt, hp), lambda b, i: (b, i, 0)),
                   pl.BlockSpec((1, MLA_WIDTH, tt), lambda b, i: (b, 0, i))],
        out_shape=[jax.ShapeDtypeStruct((bsz, L, S5_WIDTH), F32),
                   jax.ShapeDtypeStruct((bsz, L, hp), BF16),
                   jax.ShapeDtypeStruct((bsz, L, hp), BF16),
                   jax.ShapeDtypeStruct((bsz, MLA_WIDTH, L), BF16)],
        compiler_params=_params(("parallel", "parallel")),
        name="proj",
    )(x, posf, sc1, sh1, win, wqa, wqb, wk, wv, invf, sgn)


def _s5_kernel(u_ref, wb_ref, wc_ref, are_ref, aim_ref, d_ref, wglu_ref, bglu_ref, gn_ref,
               y_ref, utm, st, yt, carry, *, tc, bsz):
    @pl.when(pl.program_id(0) == 0)
    def _():
        carry[...] = jnp.zeros_like(carry)

    nslab = S5_WIDTH // LANES
    for b in range(bsz):
        for j in range(nslab):
            utm[j, pl.ds(b, tc, stride=bsz), :] = u_ref[b, :, j * LANES:(j + 1) * LANES]
    u = jnp.concatenate([utm[j] for j in range(nslab)], axis=-1)
    st[...] = jnp.dot(u.astype(BF16), wb_ref[...], preferred_element_type=F32)

    cb = 512
    for c0 in range(0, S5_COLS, cb):
        cre = slice(c0, c0 + cb)
        cim = slice(S5_COLS + c0, S5_COLS + c0 + cb)
        are = are_ref[:, cre]
        aim = aim_ref[:, cre]

        def step(t, sc, cre=cre, cim=cim, are=are, aim=aim):
            sre, sim = sc
            r = pl.multiple_of(t * bsz, bsz)
            nre = are * sre - aim * sim + st[pl.ds(r, bsz), cre]
            nim = are * sim + aim * sre + st[pl.ds(r, bsz), cim]
            st[pl.ds(r, bsz), cre] = nre
            st[pl.ds(r, bsz), cim] = nim
            return nre, nim

        sre, sim = lax.fori_loop(0, tc, step, (carry[:, cre], carry[:, cim]), unroll=4)
        carry[:, cre] = sre
        carry[:, cim] = sim

    y = jnp.dot(st[...].astype(BF16), wc_ref[...], preferred_element_type=F32) + d_ref[...] * u
    y = _gelu(y)
    z = jnp.dot(y.astype(BF16), wglu_ref[...], preferred_element_type=F32) + bglu_ref[...]
    y = y * _sigmoid(z)
    y = y * lax.rsqrt(jnp.mean(y * y, axis=-1, keepdims=True) + NORM_EPS) * gn_ref[...]
    for j in range(nslab):
        yt[j] = y[:, j * LANES:(j + 1) * LANES]
    for b in range(bsz):
        for j in range(nslab):
            y_ref[b, :, j * LANES:(j + 1) * LANES] = yt[j, pl.ds(b, tc, stride=bsz), :].astype(BF16)


def _s5(u, wb, wc, are, aim, d, wglu, bglu, gn, tc):
    bsz, L, w = u.shape
    full = lambda a: pl.BlockSpec(a.shape, lambda i: (0,) * a.ndim)
    return pl.pallas_call(
        functools.partial(_s5_kernel, tc=tc, bsz=bsz),
        grid=(L // tc,),
        in_specs=[pl.BlockSpec((bsz, tc, w), lambda i: (0, i, 0)),
                  full(wb), full(wc), full(are), full(aim), full(d), full(wglu), full(bglu), full(gn)],
        out_specs=pl.BlockSpec((bsz, tc, w), lambda i: (0, i, 0)),
        out_shape=jax.ShapeDtypeStruct((bsz, L, w), BF16),
        scratch_shapes=[pltpu.VMEM((w // LANES, tc * bsz, LANES), F32),
                        pltpu.VMEM((tc * bsz, 2 * S5_COLS), F32),
                        pltpu.VMEM((w // LANES, tc * bsz, LANES), F32),
                        pltpu.VMEM((bsz, 2 * S5_COLS), F32)],
        compiler_params=_params(("arbitrary",)),
        name="s5",
    )(u, wb, wc, are, aim, d, wglu, bglu, gn)


def _attn_kernel(q_ref, k_ref, vt_ref, o_ref, *, tq):
    qi = pl.program_id(2)
    nt = (((1,), (1,)), ((), ()))
    qs = [q_ref[0, :, hh * HEAD_PAD:(hh + 1) * HEAD_PAD] for hh in range(2)]

    def scores(kb):
        ks = pl.multiple_of(kb * tq, tq)
        return tuple(lax.dot_general(k_ref[0, pl.ds(ks, tq), hh * HEAD_PAD:(hh + 1) * HEAD_PAD], qs[hh], nt,
                                     preferred_element_type=F32) for hh in range(2))

    def update(kb, sts, state, masked):
        ks = pl.multiple_of(kb * tq, tq)
        new = []
        for hh in range(2):
            m, l, acc = state[hh]
            st = sts[hh]
            vt = vt_ref[0, hh * MLA_V:(hh + 1) * MLA_V, pl.ds(ks, tq)]
            if masked:
                kc = lax.broadcasted_iota(jnp.int32, (tq, tq), 0) // CHUNK
                qc = lax.broadcasted_iota(jnp.int32, (tq, tq), 1) // CHUNK
                st = jnp.where(kc <= qc, st, MASK_VALUE)
            m_new = jnp.maximum(m, jnp.max(st, axis=0, keepdims=True))
            alpha = jnp.exp2(m - m_new)
            pt = jnp.exp2(st - m_new)
            l = alpha * l + jnp.sum(pt, axis=0, keepdims=True)
            acc = alpha * acc + jnp.dot(vt, pt.astype(BF16), preferred_element_type=F32)
            new.append((m_new, l, acc))
        return tuple(new)

    def body(kb, carry):
        sts, state = carry
        nxt = scores(kb + 1)
        return nxt, update(kb, sts, state, False)

    init = tuple((jnp.full((1, tq), MASK_VALUE, F32), jnp.zeros((1, tq), F32),
                  jnp.zeros((MLA_V, tq), F32)) for _ in range(2))
    sts, state = lax.fori_loop(0, qi, body, (scores(0), init))
    (_, l0, a0), (_, l1, a1) = update(qi, sts, state, True)
    o_ref[0] = jnp.concatenate([a0 / l0, a1 / l1], axis=0).T.astype(BF16)


def _attn(q, k, vt, tq):
    bsz, L, hp = q.shape
    return pl.pallas_call(
        functools.partial(_attn_kernel, tq=tq),
        grid=(bsz, MLA_HEADS // 2, L // tq),
        in_specs=[pl.BlockSpec((1, tq, 2 * HEAD_PAD), lambda b, h, i: (b, i, h)),
                  pl.BlockSpec((1, L, 2 * HEAD_PAD), lambda b, h, i: (b, 0, h)),
                  pl.BlockSpec((1, 2 * MLA_V, L), lambda b, h, i: (b, h, 0))],
        out_specs=pl.BlockSpec((1, tq, 2 * MLA_V), lambda b, h, i: (b, i, h)),
        out_shape=jax.ShapeDtypeStruct((bsz, L, MLA_WIDTH), BF16),
        compiler_params=_params(("parallel", "parallel", "arbitrary")),
        name="attn",
    )(q, k, vt)


def _top16_rows(s, want_rank):
    work = s
    rank = jnp.full(s.shape, 127.0, F32) if want_rank else None
    ridx = lax.broadcasted_iota(jnp.int32, (PEER_TOPK, s.shape[1]), 0)
    vals = jnp.zeros((PEER_TOPK, s.shape[1]), F32)
    for r in range(PEER_TOPK):
        m = jnp.max(work, axis=0, keepdims=True)
        hit = work == m
        if want_rank:
            rank = jnp.where(hit, float(r), rank)
        work = jnp.where(hit, NEG_BIG, work)
        vals = jnp.where(ridx == r, m, vals)
    return vals, rank


def _mix_kernel(x_ref, ys_ref, ym_ref, g1_ref, sc2_ref, sh2_ref, gnm_ref, wo1_ref, wo2_ref,
                l1g_ref, l1b_ref, wq_ref, k1_ref, k2_ref,
                x1_ref, h2t_ref, cnt_ref, e1_ref, rk2_ref, e2_ref):
    x = x_ref[0]
    ym = ym_ref[0].astype(F32)
    ym = (ym * lax.rsqrt(jnp.mean(ym * ym, axis=-1, keepdims=True) + NORM_EPS) * gnm_ref[...]).astype(BF16)
    ymix = (jnp.dot(ys_ref[0], wo1_ref[...], preferred_element_type=F32)
            + jnp.dot(ym, wo2_ref[...], preferred_element_type=F32))
    r = DEEPNORM_ALPHA * x + (1.0 + g1_ref[0]) * ymix
    mu = jnp.mean(r, axis=-1, keepdims=True)
    rc = r - mu
    var = jnp.mean(rc * rc, axis=-1, keepdims=True)
    x1 = rc * lax.rsqrt(var + NORM_EPS) * l1g_ref[...] + l1b_ref[...]
    x1_ref[0] = x1
    h2 = x1 * (1.0 + sc2_ref[0]) + sh2_ref[0]
    h2t = h2.T.astype(BF16)
    h2t_ref[...] = h2t
    qt = jnp.dot(wq_ref[...], h2t, preferred_element_type=F32)
    for hh in range(PEER_HEADS):
        q1 = qt[hh * PEER_QDIM:hh * PEER_QDIM + PEER_HALF]
        q2 = qt[hh * PEER_QDIM + PEER_HALF:(hh + 1) * PEER_QDIM]
        s1 = jnp.dot(k1_ref[hh], q1.astype(BF16), preferred_element_type=F32)
        s2 = jnp.dot(k2_ref[hh], q2.astype(BF16), preferred_element_type=F32)
        v1, _ = _top16_rows(s1, False)
        v2, rk2 = _top16_rows(s2, True)
        cands = [v1[0:1] + v2]
        for a in range(1, 8):
            cands.append(v1[a:a + 1] + v2[0:8])
        cands.append(v1[8:16] + v2[0:1])
        top, _ = _top16_rows(jnp.concatenate(cands, axis=0), False)
        theta = top[PEER_TOPK - 1:PEER_TOPK]
        z = jnp.sum(jnp.exp(top - top[0:1]), axis=0, keepdims=True)
        dense_b = 4
        cnt = jnp.zeros_like(s1)
        for b in range(dense_b):
            cnt = cnt + jnp.where(s1 + v2[b:b + 1] >= theta, 1.0, 0.0)
        for a in range(3):
            extra = jnp.sum(jnp.where(v1[a:a + 1] + v2[dense_b:] >= theta, 1.0, 0.0), axis=0, keepdims=True)
            cnt = cnt + jnp.where(s1 == v1[a:a + 1], extra, 0.0)
        cnt_ref[hh] = jnp.where(s1 >= v1[PEER_TOPK - 1:PEER_TOPK], cnt, 0.0)
        e1_ref[hh] = jnp.exp(s1 - v1[0:1])
        rk2_ref[hh] = rk2.astype(BF16)
        e2_ref[hh] = (jnp.exp(s2 - v2[0:1]) / z).astype(BF16)


def _mix(x, ys, ym, g1, sc2, sh2, gnm, wo1, wo2, l1g, l1b, wqt, k1, k2, tt):
    bsz, L, dm = x.shape
    n = bsz * L
    nl = L // tt
    full = lambda a: pl.BlockSpec(a.shape, lambda b, i: (0,) * a.ndim)
    tok = lambda w: pl.BlockSpec((1, tt, w), lambda b, i: (b, i, 0))
    vec = pl.BlockSpec((1, 1, dm), lambda b, i: (b, 0, 0))
    tab = pl.BlockSpec((PEER_HEADS, PEER_NKEYS, tt), lambda b, i: (0, 0, b * nl + i))
    tab_shape = jax.ShapeDtypeStruct((PEER_HEADS, PEER_NKEYS, n), F32)
    tab_bf16 = jax.ShapeDtypeStruct((PEER_HEADS, PEER_NKEYS, n), BF16)
    return pl.pallas_call(
        _mix_kernel,
        grid=(bsz, nl),
        in_specs=[tok(dm), tok(S5_WIDTH), tok(MLA_WIDTH), vec, vec, vec,
                  full(gnm), full(wo1), full(wo2), full(l1g), full(l1b), full(wqt), full(k1), full(k2)],
        out_specs=[tok(dm), pl.BlockSpec((dm, tt), lambda b, i: (0, b * nl + i)), tab, tab, tab, tab],
        out_shape=[jax.ShapeDtypeStruct((bsz, L, dm), F32), jax.ShapeDtypeStruct((dm, n), BF16),
                   tab_shape, tab_shape, tab_bf16, tab_bf16],
        compiler_params=_params(("parallel", "parallel")),
        name="mix",
    )(x, ys, ym, g1, sc2, sh2, gnm, wo1, wo2, l1g, l1b, wqt, k1, k2)


def _peer_kernel(h2t_ref, cnt_ref, e1_ref, rk2_ref, e2_ref, u_ref, vt_ref, x1_ref, g2_ref, l2g_ref, l2b_ref,
                 o_ref, acc, wa, *, ni):
    eb = pl.program_id(1)

    @pl.when(eb == 0)
    def _():
        acc[...] = jnp.zeros_like(acc)

    sc = jnp.dot(u_ref[...], h2t_ref[...], preferred_element_type=F32)
    for ii in range(ni):
        i = eb * ni + ii
        w = None
        for hh in range(PEER_HEADS):
            c = cnt_ref[hh, pl.ds(i, 1), :].astype(BF16)
            e = e1_ref[hh, pl.ds(i, 1), :].astype(BF16)
            term = jnp.where(rk2_ref[hh] < c, e2_ref[hh], jnp.zeros((), BF16)) * e
            w = term if w is None else w + term
        rows = slice(ii * PEER_NKEYS, (ii + 1) * PEER_NKEYS)
        wa[rows, :] = w * _gelu(sc[rows].astype(BF16))
    acc[...] += jnp.dot(vt_ref[...], wa[...], preferred_element_type=F32)

    @pl.when(eb == pl.num_programs(1) - 1)
    def _():
        yff = acc[...].T
        r = DEEPNORM_ALPHA * x1_ref[0] + (1.0 + g2_ref[0]) * yff
        mu = jnp.mean(r, axis=-1, keepdims=True)
        rc = r - mu
        var = jnp.mean(rc * rc, axis=-1, keepdims=True)
        o_ref[0] = rc * lax.rsqrt(var + NORM_EPS) * l2g_ref[...] + l2b_ref[...]


def _peer(h2t, cnt, e1, rk2, e2, ub, vt, x1, g2, l2g, l2b, tt, ni):
    bsz, L, dm = x1.shape
    nl = L // tt
    eb = ni * PEER_NKEYS
    tab = pl.BlockSpec((PEER_HEADS, PEER_NKEYS, tt), lambda t, e: (0, 0, t))
    full = lambda a: pl.BlockSpec(a.shape, lambda t, e: (0,) * a.ndim)
    return pl.pallas_call(
        functools.partial(_peer_kernel, ni=ni),
        grid=(bsz * nl, PEER_NKEYS // ni),
        in_specs=[pl.BlockSpec((dm, tt), lambda t, e: (0, t)), tab, tab, tab, tab,
                  pl.BlockSpec((eb, dm), lambda t, e: (e, 0)),
                  pl.BlockSpec((dm, eb), lambda t, e: (0, e)),
                  pl.BlockSpec((1, tt, dm), lambda t, e: (t // nl, t % nl, 0)),
                  pl.BlockSpec((1, 1, dm), lambda t, e: (t // nl, 0, 0)),
                  full(l2g), full(l2b)],
        out_specs=pl.BlockSpec((1, tt, dm), lambda t, e: (t // nl, t % nl, 0)),
        out_shape=jax.ShapeDtypeStruct((bsz, L, dm), F32),
        scratch_shapes=[pltpu.VMEM((dm, tt), F32), pltpu.VMEM((eb, tt), BF16)],
        compiler_params=_params(("parallel", "arbitrary")),
        name="peer",
    )(h2t, cnt, e1, rk2, e2, ub, vt, x1, g2, l2g, l2b)


def _s5_discretize(lam_re, lam_im, log_dt, b_re, b_im, c_re, c_im):
    dt = jnp.exp(log_dt)[:, None]
    mag = jnp.exp(lam_re * dt)
    ab_re = mag * jnp.cos(lam_im * dt)
    ab_im = mag * jnp.sin(lam_im * dt)
    den = lam_re * lam_re + lam_im * lam_im
    nr = ab_re - 1.0
    coef_re = (nr * lam_re + ab_im * lam_im) / den
    coef_im = (ab_im * lam_re - nr * lam_im) / den
    bb_re = coef_re[..., None] * b_re - coef_im[..., None] * b_im
    bb_im = coef_re[..., None] * b_im + coef_im[..., None] * b_re
    eye = jnp.eye(S5_GROUPS, dtype=F32)
    wb_re = jnp.einsum("gph,gk->ghkp", bb_re, eye).reshape(S5_WIDTH, S5_COLS)
    wb_im = jnp.einsum("gph,gk->ghkp", bb_im, eye).reshape(S5_WIDTH, S5_COLS)
    wb = jnp.concatenate([wb_re, wb_im], axis=1).astype(BF16)
    wc_re = jnp.einsum("ghp,gk->gpkh", c_re, eye).reshape(S5_COLS, S5_WIDTH)
    wc_im = jnp.einsum("ghp,gk->gpkh", c_im, eye).reshape(S5_COLS, S5_WIDTH)
    wc = jnp.concatenate([wc_re, -wc_im], axis=0).astype(BF16)
    return wb, wc, ab_re.reshape(1, S5_COLS), ab_im.reshape(1, S5_COLS)


def _pad_heads(w, width, offset):
    kdim = w.shape[0]
    w = w.reshape(kdim, MLA_HEADS, width)
    out = jnp.zeros((kdim, MLA_HEADS, HEAD_PAD), w.dtype)
    out = out.at[:, :, offset:offset + width].set(w)
    return out.reshape(kdim, MLA_HEADS * HEAD_PAD)


def _swap_halves(w):
    half = w.shape[-1] // 2
    return jnp.concatenate([w[..., half:], w[..., :half]], axis=-1)


def kernel(x, c, positions, w_ada, b_ada, w_in, s5_lambda_re, s5_lambda_im, s5_log_dt, s5_b_re, s5_b_im, s5_c_re, s5_c_im, s5_d, s5_w_glu, s5_b_glu, mla_q_norm, mla_w_uq, mla_kv_norm, mla_w_ukv, gn_s5, gn_mla, w_out, ln1_g, ln1_b, peer_w_query, peer_keys1, peer_keys2, peer_u, peer_v, ln2_g, ln2_b):
    bsz, L, dm = x.shape
    depth = w_ada.shape[0]
    t_proj = min(512, L)
    t_s5 = min(128, L)
    t_attn = min(512, L)
    t_mix = min(256, L)
    t_peer = min(512, L)
    peer_ni = 4

    inv_freq = ROPE_THETA ** (-jnp.arange(0, MLA_ROPE, 2, dtype=F32) / MLA_ROPE)
    half = MLA_ROPE // 2
    invf = jnp.zeros((1, HEAD_PAD), F32).at[0, MLA_NOPE:MLA_NOPE + half].set(inv_freq)
    invf = invf.at[0, MLA_NOPE + half:MLA_NOPE + MLA_ROPE].set(inv_freq)
    sgn = jnp.zeros((1, HEAD_PAD), F32).at[0, MLA_NOPE:MLA_NOPE + half].set(-1.0)
    sgn = sgn.at[0, MLA_NOPE + half:MLA_NOPE + MLA_ROPE].set(1.0)
    posf = positions.astype(F32)[..., None]

    for l in range(depth):
        mod = _ada(c, w_ada[l], b_ada[l])
        sh1, sc1, g1, sh2, sc2, g2 = [m[:, None, :] for m in jnp.split(mod, 6, axis=-1)]

        wi = w_in[l]
        o1, o2, o3 = S5_WIDTH, S5_WIDTH + Q_LORA, S5_WIDTH + Q_LORA + KV_LORA
        w_kr = wi[:, o3:]
        zpad = lambda w: jnp.zeros((dm, HEAD_PAD), F32).at[:, MLA_NOPE:MLA_NOPE + MLA_ROPE].set(w)
        win = jnp.concatenate([wi[:, :o3], zpad(w_kr), zpad(_swap_halves(w_kr))], axis=1).astype(BF16)
        scale = (MLA_NOPE + MLA_ROPE) ** -0.5 * math.log2(math.e)
        wuq = (mla_w_uq[l] * mla_q_norm[l][:, None] * scale).reshape(Q_LORA, MLA_HEADS, MLA_NOPE + MLA_ROPE)
        wuq_main = wuq.reshape(Q_LORA, MLA_HEADS * (MLA_NOPE + MLA_ROPE))
        wqa = _pad_heads(wuq_main, MLA_NOPE + MLA_ROPE, 0).astype(BF16)
        wuq_sw = _swap_halves(wuq[:, :, MLA_NOPE:]).reshape(Q_LORA, MLA_HEADS * MLA_ROPE)
        wqb = _pad_heads(wuq_sw, MLA_ROPE, MLA_NOPE).astype(BF16)
        wukv = (mla_w_ukv[l] * mla_kv_norm[l][:, None]).reshape(KV_LORA, MLA_HEADS, MLA_NOPE + MLA_V)
        wk = _pad_heads(wukv[:, :, :MLA_NOPE].reshape(KV_LORA, MLA_HEADS * MLA_NOPE), MLA_NOPE, 0).astype(BF16)
        wv = wukv[:, :, MLA_NOPE:].reshape(KV_LORA, MLA_HEADS * MLA_V).T.astype(BF16)

        u, q, k, v = _proj(x, posf, sc1, sh1, win, wqa, wqb, wk, wv, invf, sgn, t_proj)

        wb, wc, are, aim = _s5_discretize(s5_lambda_re[l], s5_lambda_im[l], s5_log_dt[l], s5_b_re[l],
                                          s5_b_im[l], s5_c_re[l], s5_c_im[l])
        are = jnp.broadcast_to(are, (bsz, S5_COLS))
        aim = jnp.broadcast_to(aim, (bsz, S5_COLS))
        y_s5 = _s5(u, wb, wc, are, aim, s5_d[l][None], s5_w_glu[l].astype(BF16), s5_b_glu[l][None],
                   gn_s5[l][None], t_s5)

        y_mla = _attn(q, k, v, t_attn)

        wo = w_out[l].astype(BF16)
        wqt = peer_w_query[l].T.astype(BF16)
        x1, h2t, cnt, e1, rk2, e2 = _mix(x, y_s5, y_mla, g1, sc2, sh2, gn_mla[l][None], wo[:S5_WIDTH],
                                         wo[S5_WIDTH:], ln1_g[l][None], ln1_b[l][None], wqt,
                                         peer_keys1[l].astype(BF16), peer_keys2[l].astype(BF16), t_mix)

        x = _peer(h2t, cnt, e1, rk2, e2, peer_u[l].astype(BF16), peer_v[l].T.astype(BF16), x1, g2,
                  ln2_g[l][None], ln2_b[l][None], t_peer, peer_ni)
    return x
```

```python
import functools
import math

import jax
import jax.numpy as jnp
from jax import lax
from jax.experimental import pallas as pl
from jax.experimental.pallas import tpu as pltpu

F32 = jnp.float32
BF16 = jnp.bfloat16

D_MODEL = 1024
CHUNK = 64
S5_WIDTH = 512
S5_GROUP = 16
S5_GROUPS = 32
S5_STATE = 64
S5_COLS = S5_GROUPS * S5_STATE
MLA_WIDTH = 512
MLA_HEADS = 8
MLA_NOPE = 64
MLA_ROPE = 32
MLA_V = 64
Q_LORA = 384
KV_LORA = 256
ROPE_THETA = 10000.0
HEAD_PAD = 128
PEER_HEADS = 8
PEER_NKEYS = 128
PEER_QDIM = 256
PEER_HALF = 128
PEER_TOPK = 16
PEER_EXPERTS = PEER_NKEYS * PEER_NKEYS
DEEPNORM_ALPHA = 2.0 ** 0.25
NORM_EPS = 1e-6
MASK_VALUE = -1e30
NEG_BIG = -3.0e38

LANES = 128
SUBLANES = 8
VMEM_LIMIT = 56 * 1024 * 1024


def _gelu(x):
    return 0.5 * x * (1.0 + jnp.tanh(0.7978845608028654 * (x + 0.044715 * (x * x * x))))


def _sigmoid(x):
    return 1.0 / (1.0 + jnp.exp(-x))


def _params(sem):
    return pltpu.CompilerParams(dimension_semantics=sem, vmem_limit_bytes=VMEM_LIMIT)


def _ada_kernel(c_ref, w_ref, b_ref, o_ref):
    c = c_ref[...]
    ca = c * _sigmoid(c)
    o_ref[...] = jnp.dot(ca, w_ref[...], preferred_element_type=F32,
                         precision=lax.Precision.HIGHEST) + b_ref[...]


def _ada(c, w_ada, b_ada):
    bsz, dm = c.shape
    n = w_ada.shape[1]
    tn = 1024
    return pl.pallas_call(
        _ada_kernel,
        grid=(n // tn,),
        in_specs=[pl.BlockSpec((bsz, dm), lambda j: (0, 0)),
                  pl.BlockSpec((dm, tn), lambda j: (0, j)),
                  pl.BlockSpec((1, tn), lambda j: (0, j))],
        out_specs=pl.BlockSpec((bsz, tn), lambda j: (0, j)),
        out_shape=jax.ShapeDtypeStruct((bsz, n), F32),
        compiler_params=_params(("arbitrary",)),
        name="ada",
    )(c, w_ada, b_ada.reshape(1, n))


def _proj_kernel(x_ref, pos_ref, sc_ref, sh_ref, win_ref, wqa_ref, wqb_ref, wk_ref, wv_ref,
                 invf_ref, sgn_ref, u_ref, q_ref, k_ref, v_ref):
    x = x_ref[0]
    h = (x * (1.0 + sc_ref[0]) + sh_ref[0]).astype(BF16)
    proj = jnp.dot(h, win_ref[...], preferred_element_type=F32)
    o = S5_WIDTH
    u_ref[0] = proj[:, :o]
    cq = proj[:, o:o + Q_LORA]
    o += Q_LORA
    ckv = proj[:, o:o + KV_LORA]
    o += KV_LORA
    kra = proj[:, o:o + HEAD_PAD]
    krb = proj[:, o + HEAD_PAD:o + 2 * HEAD_PAD]
    ang = pos_ref[0] * invf_ref[...]
    cs = jnp.cos(ang)
    sn = jnp.sin(ang) * sgn_ref[...]
    cqn = (cq * lax.rsqrt(jnp.mean(cq * cq, axis=-1, keepdims=True) + NORM_EPS)).astype(BF16)
    qa = jnp.dot(cqn, wqa_ref[...], preferred_element_type=F32)
    qb = jnp.dot(cqn, wqb_ref[...], preferred_element_type=F32)
    ckvn = (ckv * lax.rsqrt(jnp.mean(ckv * ckv, axis=-1, keepdims=True) + NORM_EPS)).astype(BF16)
    kn = jnp.dot(ckvn, wk_ref[...], preferred_element_type=F32)
    kp = kra * cs + krb * sn
    for hh in range(MLA_HEADS):
        sl = slice(hh * HEAD_PAD, (hh + 1) * HEAD_PAD)
        q_ref[0, :, sl] = (qa[:, sl] * cs + qb[:, sl] * sn).astype(BF16)
        k_ref[0, :, sl] = (kn[:, sl] + kp).astype(BF16)
    v_ref[0] = lax.dot_general(wv_ref[...], ckvn, (((1,), (1,)), ((), ())),
                               preferred_element_type=F32).astype(BF16)


def _proj(x, posf, sc1, sh1, win, wqa, wqb, wk, wv, invf, sgn, tt):
    bsz, L, dm = x.shape
    hp = MLA_HEADS * HEAD_PAD
    full = lambda a: pl.BlockSpec(a.shape, lambda b, i: (0,) * a.ndim)
    return pl.pallas_call(
        _proj_kernel,
        grid=(bsz, L // tt),
        in_specs=[pl.BlockSpec((1, tt, dm), lambda b, i: (b, i, 0)),
                  pl.BlockSpec((1, tt, 1), lambda b, i: (b, i, 0)),
                  pl.BlockSpec((1, 1, dm), lambda b, i: (b, 0, 0)),
                  pl.BlockSpec((1, 1, dm), lambda b, i: (b, 0, 0)),
                  full(win), full(wqa), full(wqb), full(wk), full(wv), full(invf), full(sgn)],
        out_specs=[pl.BlockSpec((1, tt, S5_WIDTH), lambda b, i: (b, i, 0)),
                   pl.BlockSpec((1, tt, hp), lambda b, i: (b, i, 0)),
                   pl.BlockSpec((1, tt, hp), lambda b, i: (b, i, 0)),
                   pl.BlockSpec((1, MLA_WIDTH, tt), lambda b, i: (b, 0, i))],
        out_shape=[jax.ShapeDtypeStruct((bsz, L, S5_WIDTH), F32),
                   jax.ShapeDtypeStruct((bsz, L, hp), BF16),
                   jax.ShapeDtypeStruct((bsz, L, hp), BF16),
                   jax.ShapeDtypeStruct((bsz, MLA_WIDTH, L), BF16)],
        compiler_params=_params(("parallel", "parallel")),
        name="proj",
    )(x, posf, sc1, sh1, win, wqa, wqb, wk, wv, invf, sgn)


def _s5_kernel(u_ref, wb_ref, wc_ref, are_ref, aim_ref, d_ref, wglu_ref, bglu_ref, gn_ref,
               y_ref, utm, st, yt, carry, *, tc, bsz):
    @pl.when(pl.program_id(0) == 0)
    def _():
        carry[...] = jnp.zeros_like(carry)

    nslab = S5_WIDTH // LANES
    for b in range(bsz):
        for j in range(nslab):
            utm[j, pl.ds(b, tc, stride=bsz), :] = u_ref[b, :, j * LANES:(j + 1) * LANES]
    u = jnp.concatenate([utm[j] for j in range(nslab)], axis=-1)
    st[...] = jnp.dot(u.astype(BF16), wb_ref[...], preferred_element_type=F32)

    cb = 512
    for c0 in range(0, S5_COLS, cb):
        cre = slice(c0, c0 + cb)
        cim = slice(S5_COLS + c0, S5_COLS + c0 + cb)
        are = are_ref[:, cre]
        aim = aim_ref[:, cre]

        def step(t, sc, cre=cre, cim=cim, are=are, aim=aim):
            sre, sim = sc
            r = pl.multiple_of(t * bsz, bsz)
            nre = are * sre - aim * sim + st[pl.ds(r, bsz), cre]
            nim = are * sim + aim * sre + st[pl.ds(r, bsz), cim]
            st[pl.ds(r, bsz), cre] = nre
            st[pl.ds(r, bsz), cim] = nim
            return nre, nim

        sre, sim = lax.fori_loop(0, tc, step, (carry[:, cre], carry[:, cim]), unroll=4)
        carry[:, cre] = sre
        carry[:, cim] = sim

    y = jnp.dot(st[...].astype(BF16), wc_ref[...], preferred_element_type=F32) + d_ref[...] * u
    y = _gelu(y)
    z = jnp.dot(y.astype(BF16), wglu_ref[...], preferred_element_type=F32) + bglu_ref[...]
    y = y * _sigmoid(z)
    y = y * lax.rsqrt(jnp.mean(y * y, axis=-1, keepdims=True) + NORM_EPS) * gn_ref[...]
    for j in range(nslab):
        yt[j] = y[:, j * LANES:(j + 1) * LANES]
    for b in range(bsz):
        for j in range(nslab):
            y_ref[b, :, j * LANES:(j + 1) * LANES] = yt[j, pl.ds(b, tc, stride=bsz), :].astype(BF16)


def _s5(u, wb, wc, are, aim, d, wglu, bglu, gn, tc):
    bsz, L, w = u.shape
    full = lambda a: pl.BlockSpec(a.shape, lambda i: (0,) * a.ndim)
    return pl.pallas_call(
        functools.partial(_s5_kernel, tc=tc, bsz=bsz),
        grid=(L // tc,),
        in_specs=[pl.BlockSpec((bsz, tc, w), lambda i: (0, i, 0)),
                  full(wb), full(wc), full(are), full(aim), full(d), full(wglu), full(bglu), full(gn)],
        out_specs=pl.BlockSpec((bsz, tc, w), lambda i: (0, i, 0)),
        out_shape=jax.ShapeDtypeStruct((bsz, L, w), BF16),
        scratch_shapes=[pltpu.VMEM((w // LANES, tc * bsz, LANES), F32),
                        pltpu.VMEM((tc * bsz, 2 * S5_COLS), F32),
                        pltpu.VMEM((w // LANES, tc * bsz, LANES), F32),
                        pltpu.VMEM((bsz, 2 * S5_COLS), F32)],
        compiler_params=_params(("arbitrary",)),
        name="s5",
    )(u, wb, wc, are, aim, d, wglu, bglu, gn)


def _attn_kernel(q_ref, k_ref, vt_ref, o_ref, sa, sb, mxa, mxb, m_s, l_s, acc_s, *, tq):
    qi = pl.program_id(2)
    nt = (((1,), (1,)), ((), ()))
    qs = [q_ref[0, :, hh * HEAD_PAD:(hh + 1) * HEAD_PAD] for hh in range(2)]

    m_s[...] = jnp.full(m_s.shape, MASK_VALUE, F32)
    l_s[...] = jnp.zeros_like(l_s)
    acc_s[...] = jnp.zeros_like(acc_s)

    def scores_into(kb, sbuf, mxbuf):
        ks = pl.multiple_of(kb * tq, tq)
        for hh in range(2):
            st = lax.dot_general(k_ref[0, pl.ds(ks, tq), hh * HEAD_PAD:(hh + 1) * HEAD_PAD], qs[hh], nt,
                                 preferred_element_type=F32)
            sbuf[hh] = st
            mxbuf[hh] = jnp.max(st, axis=0, keepdims=True)

    def softmax_from(kb, sbuf, mxbuf, masked):
        ks = pl.multiple_of(kb * tq, tq)
        for hh in range(2):
            st = sbuf[hh]
            if masked:
                kc = lax.broadcasted_iota(jnp.int32, (tq, tq), 0) // CHUNK
                qc = lax.broadcasted_iota(jnp.int32, (tq, tq), 1) // CHUNK
                st = jnp.where(kc <= qc, st, MASK_VALUE)
                mx = jnp.max(st, axis=0, keepdims=True)
            else:
                mx = mxbuf[hh]
            vt = vt_ref[0, hh * MLA_V:(hh + 1) * MLA_V, pl.ds(ks, tq)]
            m = m_s[hh]
            m_new = jnp.maximum(m, mx)
            alpha = jnp.exp2(m - m_new)
            pt = jnp.exp2(st - m_new)
            l_s[hh] = alpha * l_s[hh] + jnp.sum(pt, axis=0, keepdims=True)
            acc_s[hh] = alpha * acc_s[hh] + jnp.dot(vt, pt.astype(BF16), preferred_element_type=F32)
            m_s[hh] = m_new

    def finish():
        o_ref[0] = jnp.concatenate([acc_s[0] / l_s[0], acc_s[1] / l_s[1]], axis=0).T.astype(BF16)

    scores_into(0, sa, mxa)

    def pair(j, carry):
        scores_into(2 * j + 1, sb, mxb)
        softmax_from(2 * j, sa, mxa, False)
        scores_into(2 * j + 2, sa, mxa)
        softmax_from(2 * j + 1, sb, mxb, False)
        return carry

    lax.fori_loop(0, qi // 2, pair, 0)

    @pl.when(qi % 2 == 0)
    def _():
        softmax_from(qi, sa, mxa, True)
        finish()

    @pl.when(qi % 2 == 1)
    def _():
        scores_into(qi, sb, mxb)
        softmax_from(qi - 1, sa, mxa, False)
        softmax_from(qi, sb, mxb, True)
        finish()


def _attn(q, k, vt, tq):
    bsz, L, hp = q.shape
    return pl.pallas_call(
        functools.partial(_attn_kernel, tq=tq),
        grid=(bsz, MLA_HEADS // 2, L // tq),
        in_specs=[pl.BlockSpec((1, tq, 2 * HEAD_PAD), lambda b, h, i: (b, i, h)),
                  pl.BlockSpec((1, L, 2 * HEAD_PAD), lambda b, h, i: (b, 0, h)),
                  pl.BlockSpec((1, 2 * MLA_V, L), lambda b, h, i: (b, h, 0))],
        out_specs=pl.BlockSpec((1, tq, 2 * MLA_V), lambda b, h, i: (b, i, h)),
        out_shape=jax.ShapeDtypeStruct((bsz, L, MLA_WIDTH), BF16),
        scratch_shapes=[pltpu.VMEM((2, tq, tq), F32), pltpu.VMEM((2, tq, tq), F32),
                        pltpu.VMEM((2, 1, tq), F32), pltpu.VMEM((2, 1, tq), F32),
                        pltpu.VMEM((2, 1, tq), F32), pltpu.VMEM((2, 1, tq), F32),
                        pltpu.VMEM((2, MLA_V, tq), F32)],
        compiler_params=_params(("parallel", "parallel", "arbitrary")),
        name="attn",
    )(q, k, vt)


def _top16_rows(s, want_rank):
    work = s
    rank = jnp.full(s.shape, 127.0, F32) if want_rank else None
    ridx = lax.broadcasted_iota(jnp.int32, (PEER_TOPK, s.shape[1]), 0)
    vals = jnp.zeros((PEER_TOPK, s.shape[1]), F32)
    for r in range(PEER_TOPK):
        m = jnp.max(work, axis=0, keepdims=True)
        hit = work == m
        if want_rank:
            rank = jnp.where(hit, float(r), rank)
        work = jnp.where(hit, NEG_BIG, work)
        vals = jnp.where(ridx == r, m, vals)
    return vals, rank


def _mix_kernel(x_ref, ys_ref, ym_ref, g1_ref, sc2_ref, sh2_ref, gnm_ref, wo1_ref, wo2_ref,
                l1g_ref, l1b_ref, wq_ref, k1_ref, k2_ref,
                x1_ref, h2t_ref, cnt_ref, e1_ref, rk2_ref, e2_ref):
    x = x_ref[0]
    ym = ym_ref[0].astype(F32)
    ym = (ym * lax.rsqrt(jnp.mean(ym * ym, axis=-1, keepdims=True) + NORM_EPS) * gnm_ref[...]).astype(BF16)
    ymix = (jnp.dot(ys_ref[0], wo1_ref[...], preferred_element_type=F32)
            + jnp.dot(ym, wo2_ref[...], preferred_element_type=F32))
    r = DEEPNORM_ALPHA * x + (1.0 + g1_ref[0]) * ymix
    mu = jnp.mean(r, axis=-1, keepdims=True)
    rc = r - mu
    var = jnp.mean(rc * rc, axis=-1, keepdims=True)
    x1 = rc * lax.rsqrt(var + NORM_EPS) * l1g_ref[...] + l1b_ref[...]
    x1_ref[0] = x1
    h2 = x1 * (1.0 + sc2_ref[0]) + sh2_ref[0]
    h2t = h2.T.astype(BF16)
    h2t_ref[...] = h2t
    qt = jnp.dot(wq_ref[...], h2t, preferred_element_type=F32)
    for hh in range(PEER_HEADS):
        q1 = qt[hh * PEER_QDIM:hh * PEER_QDIM + PEER_HALF]
        q2 = qt[hh * PEER_QDIM + PEER_HALF:(hh + 1) * PEER_QDIM]
        s1 = jnp.dot(k1_ref[hh], q1.astype(BF16), preferred_element_type=F32)
        s2 = jnp.dot(k2_ref[hh], q2.astype(BF16), preferred_element_type=F32)
        v1, _ = _top16_rows(s1, False)
        v2, rk2 = _top16_rows(s2, True)
        cands = [v1[0:1] + v2]
        for a in range(1, 8):
            cands.append(v1[a:a + 1] + v2[0:8])
        cands.append(v1[8:16] + v2[0:1])
        top, _ = _top16_rows(jnp.concatenate(cands, axis=0), False)
        theta = top[PEER_TOPK - 1:PEER_TOPK]
        z = jnp.sum(jnp.exp(top - top[0:1]), axis=0, keepdims=True)
        dense_b = 4
        cnt = jnp.zeros_like(s1)
        for b in range(dense_b):
            cnt = cnt + jnp.where(s1 + v2[b:b + 1] >= theta, 1.0, 0.0)
        for a in range(3):
            extra = jnp.sum(jnp.where(v1[a:a + 1] + v2[dense_b:] >= theta, 1.0, 0.0), axis=0, keepdims=True)
            cnt = cnt + jnp.where(s1 == v1[a:a + 1], extra, 0.0)
        cnt_ref[hh] = jnp.where(s1 >= v1[PEER_TOPK - 1:PEER_TOPK], cnt, 0.0)
        e1_ref[hh] = jnp.exp(s1 - v1[0:1])
        rk2_ref[hh] = rk2.astype(BF16)
        e2_ref[hh] = (jnp.exp(s2 - v2[0:1]) / z).astype(BF16)


def _mix(x, ys, ym, g1, sc2, sh2, gnm, wo1, wo2, l1g, l1b, wqt, k1, k2, tt):
    bsz, L, dm = x.shape
    n = bsz * L
    nl = L // tt
    full = lambda a: pl.BlockSpec(a.shape, lambda b, i: (0,) * a.ndim)
    tok = lambda w: pl.BlockSpec((1, tt, w), lambda b, i: (b, i, 0))
    vec = pl.BlockSpec((1, 1, dm), lambda b, i: (b, 0, 0))
    tab = pl.BlockSpec((PEER_HEADS, PEER_NKEYS, tt), lambda b, i: (0, 0, b * nl + i))
    tab_shape = jax.ShapeDtypeStruct((PEER_HEADS, PEER_NKEYS, n), F32)
    tab_bf16 = jax.ShapeDtypeStruct((PEER_HEADS, PEER_NKEYS, n), BF16)
    return pl.pallas_call(
        _mix_kernel,
        grid=(bsz, nl),
        in_specs=[tok(dm), tok(S5_WIDTH), tok(MLA_WIDTH), vec, vec, vec,
                  full(gnm), full(wo1), full(wo2), full(l1g), full(l1b), full(wqt), full(k1), full(k2)],
        out_specs=[tok(dm), pl.BlockSpec((dm, tt), lambda b, i: (0, b * nl + i)), tab, tab, tab, tab],
        out_shape=[jax.ShapeDtypeStruct((bsz, L, dm), F32), jax.ShapeDtypeStruct((dm, n), BF16),
                   tab_shape, tab_shape, tab_bf16, tab_bf16],
        compiler_params=_params(("parallel", "parallel")),
        name="mix",
    )(x, ys, ym, g1, sc2, sh2, gnm, wo1, wo2, l1g, l1b, wqt, k1, k2)


def _peer_kernel(h2t_ref, cnt_ref, e1_ref, rk2_ref, e2_ref, u_ref, vt_ref, x1_ref, g2_ref, l2g_ref, l2b_ref,
                 o_ref, acc, wa0, wa1, *, ni, sub):
    eb = pl.program_id(1)
    ne = pl.num_programs(1) - 1

    @pl.when(eb == 0)
    def _():
        acc[...] = jnp.zeros_like(acc)
        wa1[...] = jnp.zeros_like(wa1)

    dm = acc.shape[0]

    def consume(wa_r, q, nq):
        rows = slice(q * (dm // nq), (q + 1) * (dm // nq))
        acc[rows, :] += jnp.dot(vt_ref[rows, :], wa_r[...], preferred_element_type=F32)

    def step(wa_w, wa_r):
        sc = None
        for ii in range(ni):
            i = eb * ni + ii
            w = None
            for hh in range(PEER_HEADS):
                c = cnt_ref[hh, pl.ds(i, 1), :].astype(BF16)
                e = e1_ref[hh, pl.ds(i, 1), :].astype(BF16)
                term = jnp.where(rk2_ref[hh] < c, e2_ref[hh], jnp.zeros((), BF16)) * e
                w = term if w is None else w + term
            if ii % sub == 0:
                sc = jnp.dot(u_ref[ii * PEER_NKEYS:(ii + sub) * PEER_NKEYS, :], h2t_ref[...],
                             preferred_element_type=F32)
            consume(wa_r, ii, ni)
            rows = slice((ii % sub) * PEER_NKEYS, (ii % sub + 1) * PEER_NKEYS)
            wa_w[ii * PEER_NKEYS:(ii + 1) * PEER_NKEYS, :] = w * _gelu(sc[rows].astype(BF16))

    @pl.when((eb % 2 == 0) & (eb < ne))
    def _():
        step(wa0, wa1)

    @pl.when(eb % 2 == 1)
    def _():
        step(wa1, wa0)

    @pl.when(eb == ne)
    def _():
        consume(wa1, 0, 1)
        yff = acc[...].T
        r = DEEPNORM_ALPHA * x1_ref[0] + (1.0 + g2_ref[0]) * yff
        mu = jnp.mean(r, axis=-1, keepdims=True)
        rc = r - mu
        var = jnp.mean(rc * rc, axis=-1, keepdims=True)
        o_ref[0] = rc * lax.rsqrt(var + NORM_EPS) * l2g_ref[...] + l2b_ref[...]


def _peer(h2t, cnt, e1, rk2, e2, ub, vt, x1, g2, l2g, l2b, tt, ni):
    bsz, L, dm = x1.shape
    nl = L // tt
    eb = ni * PEER_NKEYS
    tab = pl.BlockSpec((PEER_HEADS, PEER_NKEYS, tt), lambda t, e: (0, 0, t))
    full = lambda a: pl.BlockSpec(a.shape, lambda t, e: (0,) * a.ndim)
    ne = PEER_NKEYS // ni
    assert ne % 2 == 0 and ni % 2 == 0
    return pl.pallas_call(
        functools.partial(_peer_kernel, ni=ni, sub=ni // 2),
        grid=(bsz * nl, ne + 1),
        in_specs=[pl.BlockSpec((dm, tt), lambda t, e: (0, t)), tab, tab, tab, tab,
                  pl.BlockSpec((eb, dm), lambda t, e: (jnp.minimum(e, ne - 1), 0)),
                  pl.BlockSpec((dm, eb), lambda t, e: (0, jnp.maximum(e - 1, 0))),
                  pl.BlockSpec((1, tt, dm), lambda t, e: (t // nl, t % nl, 0)),
                  pl.BlockSpec((1, 1, dm), lambda t, e: (t // nl, 0, 0)),
                  full(l2g), full(l2b)],
        out_specs=pl.BlockSpec((1, tt, dm), lambda t, e: (t // nl, t % nl, 0)),
        out_shape=jax.ShapeDtypeStruct((bsz, L, dm), F32),
        scratch_shapes=[pltpu.VMEM((dm, tt), F32), pltpu.VMEM((eb, tt), BF16), pltpu.VMEM((eb, tt), BF16)],
        compiler_params=_params(("parallel", "arbitrary")),
        name="peer",
    )(h2t, cnt, e1, rk2, e2, ub, vt, x1, g2, l2g, l2b)


def _s5_discretize(lam_re, lam_im, log_dt, b_re, b_im, c_re, c_im):
    dt = jnp.exp(log_dt)[:, None]
    mag = jnp.exp(lam_re * dt)
    ab_re = mag * jnp.cos(lam_im * dt)
    ab_im = mag * jnp.sin(lam_im * dt)
    den = lam_re * lam_re + lam_im * lam_im
    nr = ab_re - 1.0
    coef_re = (nr * lam_re + ab_im * lam_im) / den
    coef_im = (ab_im * lam_re - nr * lam_im) / den
    bb_re = coef_re[..., None] * b_re - coef_im[..., None] * b_im
    bb_im = coef_re[..., None] * b_im + coef_im[..., None] * b_re
    eye = jnp.eye(S5_GROUPS, dtype=F32)
    wb_re = jnp.einsum("gph,gk->ghkp", bb_re, eye).reshape(S5_WIDTH, S5_COLS)
    wb_im = jnp.einsum("gph,gk->ghkp", bb_im, eye).reshape(S5_WIDTH, S5_COLS)
    wb = jnp.concatenate([wb_re, wb_im], axis=1).astype(BF16)
    wc_re = jnp.einsum("ghp,gk->gpkh", c_re, eye).reshape(S5_COLS, S5_WIDTH)
    wc_im = jnp.einsum("ghp,gk->gpkh", c_im, eye).reshape(S5_COLS, S5_WIDTH)
    wc = jnp.concatenate([wc_re, -wc_im], axis=0).astype(BF16)
    return wb, wc, ab_re.reshape(1, S5_COLS), ab_im.reshape(1, S5_COLS)


def _pad_heads(w, width, offset):
    kdim = w.shape[0]
    w = w.reshape(kdim, MLA_HEADS, width)
    out = jnp.zeros((kdim, MLA_HEADS, HEAD_PAD), w.dtype)
    out = out.at[:, :, offset:offset + width].set(w)
    return out.reshape(kdim, MLA_HEADS * HEAD_PAD)


def _swap_halves(w):
    half = w.shape[-1] // 2
    return jnp.concatenate([w[..., half:], w[..., :half]], axis=-1)


def kernel(x, c, positions, w_ada, b_ada, w_in, s5_lambda_re, s5_lambda_im, s5_log_dt, s5_b_re, s5_b_im, s5_c_re, s5_c_im, s5_d, s5_w_glu, s5_b_glu, mla_q_norm, mla_w_uq, mla_kv_norm, mla_w_ukv, gn_s5, gn_mla, w_out, ln1_g, ln1_b, peer_w_query, peer_keys1, peer_keys2, peer_u, peer_v, ln2_g, ln2_b):
    bsz, L, dm = x.shape
    depth = w_ada.shape[0]
    t_proj = min(512, L)
    t_s5 = min(128, L)
    t_attn = min(512, L)
    t_mix = min(256, L)
    t_peer = min(512, L)
    peer_ni = 4

    inv_freq = ROPE_THETA ** (-jnp.arange(0, MLA_ROPE, 2, dtype=F32) / MLA_ROPE)
    half = MLA_ROPE // 2
    invf = jnp.zeros((1, HEAD_PAD), F32).at[0, MLA_NOPE:MLA_NOPE + half].set(inv_freq)
    invf = invf.at[0, MLA_NOPE + half:MLA_NOPE + MLA_ROPE].set(inv_freq)
    sgn = jnp.zeros((1, HEAD_PAD), F32).at[0, MLA_NOPE:MLA_NOPE + half].set(-1.0)
    sgn = sgn.at[0, MLA_NOPE + half:MLA_NOPE + MLA_ROPE].set(1.0)
    posf = positions.astype(F32)[..., None]

    for l in range(depth):
        mod = _ada(c, w_ada[l], b_ada[l])
        sh1, sc1, g1, sh2, sc2, g2 = [m[:, None, :] for m in jnp.split(mod, 6, axis=-1)]

        wi = w_in[l]
        o1, o2, o3 = S5_WIDTH, S5_WIDTH + Q_LORA, S5_WIDTH + Q_LORA + KV_LORA
        w_kr = wi[:, o3:]
        zpad = lambda w: jnp.zeros((dm, HEAD_PAD), F32).at[:, MLA_NOPE:MLA_NOPE + MLA_ROPE].set(w)
        win = jnp.concatenate([wi[:, :o3], zpad(w_kr), zpad(_swap_halves(w_kr))], axis=1).astype(BF16)
        scale = (MLA_NOPE + MLA_ROPE) ** -0.5 * math.log2(math.e)
        wuq = (mla_w_uq[l] * mla_q_norm[l][:, None] * scale).reshape(Q_LORA, MLA_HEADS, MLA_NOPE + MLA_ROPE)
        wuq_main = wuq.reshape(Q_LORA, MLA_HEADS * (MLA_NOPE + MLA_ROPE))
        wqa = _pad_heads(wuq_main, MLA_NOPE + MLA_ROPE, 0).astype(BF16)
        wuq_sw = _swap_halves(wuq[:, :, MLA_NOPE:]).reshape(Q_LORA, MLA_HEADS * MLA_ROPE)
        wqb = _pad_heads(wuq_sw, MLA_ROPE, MLA_NOPE).astype(BF16)
        wukv = (mla_w_ukv[l] * mla_kv_norm[l][:, None]).reshape(KV_LORA, MLA_HEADS, MLA_NOPE + MLA_V)
        wk = _pad_heads(wukv[:, :, :MLA_NOPE].reshape(KV_LORA, MLA_HEADS * MLA_NOPE), MLA_NOPE, 0).astype(BF16)
        wv = wukv[:, :, MLA_NOPE:].reshape(KV_LORA, MLA_HEADS * MLA_V).T.astype(BF16)

        u, q, k, v = _proj(x, posf, sc1, sh1, win, wqa, wqb, wk, wv, invf, sgn, t_proj)

        wb, wc, are, aim = _s5_discretize(s5_lambda_re[l], s5_lambda_im[l], s5_log_dt[l], s5_b_re[l],
                                          s5_b_im[l], s5_c_re[l], s5_c_im[l])
        are = jnp.broadcast_to(are, (bsz, S5_COLS))
        aim = jnp.broadcast_to(aim, (bsz, S5_COLS))
        y_s5 = _s5(u, wb, wc, are, aim, s5_d[l][None], s5_w_glu[l].astype(BF16), s5_b_glu[l][None],
                   gn_s5[l][None], t_s5)

        y_mla = _attn(q, k, v, t_attn)

        wo = w_out[l].astype(BF16)
        wqt = peer_w_query[l].T.astype(BF16)
        x1, h2t, cnt, e1, rk2, e2 = _mix(x, y_s5, y_mla, g1, sc2, sh2, gn_mla[l][None], wo[:S5_WIDTH],
                                         wo[S5_WIDTH:], ln1_g[l][None], ln1_b[l][None], wqt,
                                         peer_keys1[l].astype(BF16), peer_keys2[l].astype(BF16), t_mix)

        x = _peer(h2t, cnt, e1, rk2, e2, peer_u[l].astype(BF16), peer_v[l].T.astype(BF16), x1, g2,
                  ln2_g[l][None], ln2_b[l][None], t_peer, peer_ni)
    return x
```

```python
import functools
import math

import jax
import jax.numpy as jnp
from jax import lax
from jax.experimental import pallas as pl
from jax.experimental.pallas import tpu as pltpu

F32 = jnp.float32
BF16 = jnp.bfloat16

D_MODEL = 1024
CHUNK = 64
S5_WIDTH = 512
S5_GROUP = 16
S5_GROUPS = 32
S5_STATE = 64
S5_COLS = S5_GROUPS * S5_STATE
MLA_WIDTH = 512
MLA_HEADS = 8
MLA_NOPE = 64
MLA_ROPE = 32
MLA_V = 64
Q_LORA = 384
KV_LORA = 256
ROPE_THETA = 10000.0
HEAD_PAD = 128
PEER_HEADS = 8
PEER_NKEYS = 128
PEER_QDIM = 256
PEER_HALF = 128
PEER_TOPK = 16
PEER_EXPERTS = PEER_NKEYS * PEER_NKEYS
DEEPNORM_ALPHA = 2.0 ** 0.25
NORM_EPS = 1e-6
MASK_VALUE = -1e30
NEG_BIG = -3.0e38

LANES = 128
SUBLANES = 8
VMEM_LIMIT = 56 * 1024 * 1024


def _gelu(x):
    return 0.5 * x * (1.0 + jnp.tanh(0.7978845608028654 * (x + 0.044715 * (x * x * x))))


def _sigmoid(x):
    return 1.0 / (1.0 + jnp.exp(-x))


def _params(sem):
    return pltpu.CompilerParams(dimension_semantics=sem, vmem_limit_bytes=VMEM_LIMIT)


def _ada_kernel(c_ref, w_ref, b_ref, o_ref):
    c = c_ref[...]
    ca = c * _sigmoid(c)
    o_ref[...] = jnp.dot(ca, w_ref[...], preferred_element_type=F32,
                         precision=lax.Precision.HIGHEST) + b_ref[...]


def _ada(c, w_ada, b_ada):
    bsz, dm = c.shape
    n = w_ada.shape[1]
    tn = 1024
    return pl.pallas_call(
        _ada_kernel,
        grid=(n // tn,),
        in_specs=[pl.BlockSpec((bsz, dm), lambda j: (0, 0)),
                  pl.BlockSpec((dm, tn), lambda j: (0, j)),
                  pl.BlockSpec((1, tn), lambda j: (0, j))],
        out_specs=pl.BlockSpec((bsz, tn), lambda j: (0, j)),
        out_shape=jax.ShapeDtypeStruct((bsz, n), F32),
        compiler_params=_params(("arbitrary",)),
        name="ada",
    )(c, w_ada, b_ada.reshape(1, n))


def _proj_kernel(x_ref, pos_ref, sc_ref, sh_ref, win_ref, wqa_ref, wqb_ref, wk_ref, wv_ref,
                 invf_ref, sgn_ref, u_ref, q_ref, k_ref, v_ref):
    x = x_ref[0]
    h = (x * (1.0 + sc_ref[0]) + sh_ref[0]).astype(BF16)
    proj = jnp.dot(h, win_ref[...], preferred_element_type=F32)
    o = S5_WIDTH
    u_ref[0] = proj[:, :o]
    cq = proj[:, o:o + Q_LORA]
    o += Q_LORA
    ckv = proj[:, o:o + KV_LORA]
    o += KV_LORA
    kra = proj[:, o:o + HEAD_PAD]
    krb = proj[:, o + HEAD_PAD:o + 2 * HEAD_PAD]
    ang = pos_ref[0] * invf_ref[...]
    cs = jnp.cos(ang)
    sn = jnp.sin(ang) * sgn_ref[...]
    cqn = (cq * lax.rsqrt(jnp.mean(cq * cq, axis=-1, keepdims=True) + NORM_EPS)).astype(BF16)
    qa = jnp.dot(cqn, wqa_ref[...], preferred_element_type=F32)
    qb = jnp.dot(cqn, wqb_ref[...], preferred_element_type=F32)
    ckvn = (ckv * lax.rsqrt(jnp.mean(ckv * ckv, axis=-1, keepdims=True) + NORM_EPS)).astype(BF16)
    kn = jnp.dot(ckvn, wk_ref[...], preferred_element_type=F32)
    kp = kra * cs + krb * sn
    for hh in range(MLA_HEADS):
        sl = slice(hh * HEAD_PAD, (hh + 1) * HEAD_PAD)
        q_ref[0, :, sl] = (qa[:, sl] * cs + qb[:, sl] * sn).astype(BF16)
        k_ref[0, :, sl] = (kn[:, sl] + kp).astype(BF16)
    v_ref[0] = lax.dot_general(wv_ref[...], ckvn, (((1,), (1,)), ((), ())),
                               preferred_element_type=F32).astype(BF16)


def _proj(x, posf, sc1, sh1, win, wqa, wqb, wk, wv, invf, sgn, tt):
    bsz, L, dm = x.shape
    hp = MLA_HEADS * HEAD_PAD
    full = lambda a: pl.BlockSpec(a.shape, lambda b, i: (0,) * a.ndim)
    return pl.pallas_call(
        _proj_kernel,
        grid=(bsz, L // tt),
        in_specs=[pl.BlockSpec((1, tt, dm), lambda b, i: (b, i, 0)),
                  pl.BlockSpec((1, tt, 1), lambda b, i: (b, i, 0)),
                  pl.BlockSpec((1, 1, dm), lambda b, i: (b, 0, 0)),
                  pl.BlockSpec((1, 1, dm), lambda b, i: (b, 0, 0)),
                  full(win), full(wqa), full(wqb), full(wk), full(wv), full(invf), full(sgn)],
        out_specs=[pl.BlockSpec((1, tt, S5_WIDTH), lambda b, i: (b, i, 0)),
                   pl.BlockSpec((1, tt, hp), lambda b, i: (b, i, 0)),
                   pl.BlockSpec((1, tt, hp), lambda b, i: (b, i, 0)),
                   pl.BlockSpec((1, MLA_WIDTH, tt), lambda b, i: (b, 0, i))],
        out_shape=[jax.ShapeDtypeStruct((bsz, L, S5_WIDTH), F32),
                   jax.ShapeDtypeStruct((bsz, L, hp), BF16),
                   jax.ShapeDtypeStruct((bsz, L, hp), BF16),
                   jax.ShapeDtypeStruct((bsz, MLA_WIDTH, L), BF16)],
        compiler_params=_params(("parallel", "parallel")),
        name="proj",
    )(x, posf, sc1, sh1, win, wqa, wqb, wk, wv, invf, sgn)


def _s5_kernel(u_ref, wb_ref, wc_ref, are_ref, aim_ref, d_ref, wglu_ref, bglu_ref, gn_ref,
               y_ref, utm, st, yt, carry, *, tc, bsz):
    @pl.when(pl.program_id(0) == 0)
    def _():
        carry[...] = jnp.zeros_like(carry)

    nslab = S5_WIDTH // LANES
    for b in range(bsz):
        for j in range(nslab):
            utm[j, pl.ds(b, tc, stride=bsz), :] = u_ref[b, :, j * LANES:(j + 1) * LANES]
    u = jnp.concatenate([utm[j] for j in range(nslab)], axis=-1)
    st[...] = jnp.dot(u.astype(BF16), wb_ref[...], preferred_element_type=F32)

    cb = 512
    for c0 in range(0, S5_COLS, cb):
        cre = slice(c0, c0 + cb)
        cim = slice(S5_COLS + c0, S5_COLS + c0 + cb)
        are = are_ref[:, cre]
        aim = aim_ref[:, cre]

        def step(t, sc, cre=cre, cim=cim, are=are, aim=aim):
            sre, sim = sc
            r = pl.multiple_of(t * bsz, bsz)
            nre = are * sre - aim * sim + st[pl.ds(r, bsz), cre]
            nim = are * sim + aim * sre + st[pl.ds(r, bsz), cim]
            st[pl.ds(r, bsz), cre] = nre
            st[pl.ds(r, bsz), cim] = nim
            return nre, nim

        sre, sim = lax.fori_loop(0, tc, step, (carry[:, cre], carry[:, cim]), unroll=4)
        carry[:, cre] = sre
        carry[:, cim] = sim

    y = jnp.dot(st[...].astype(BF16), wc_ref[...], preferred_element_type=F32) + d_ref[...] * u
    y = _gelu(y)
    z = jnp.dot(y.astype(BF16), wglu_ref[...], preferred_element_type=F32) + bglu_ref[...]
    y = y * _sigmoid(z)
    y = y * lax.rsqrt(jnp.mean(y * y, axis=-1, keepdims=True) + NORM_EPS) * gn_ref[...]
    for j in range(nslab):
        yt[j] = y[:, j * LANES:(j + 1) * LANES]
    for b in range(bsz):
        for j in range(nslab):
            y_ref[b, :, j * LANES:(j + 1) * LANES] = yt[j, pl.ds(b, tc, stride=bsz), :].astype(BF16)


def _s5(u, wb, wc, are, aim, d, wglu, bglu, gn, tc):
    bsz, L, w = u.shape
    full = lambda a: pl.BlockSpec(a.shape, lambda i: (0,) * a.ndim)
    return pl.pallas_call(
        functools.partial(_s5_kernel, tc=tc, bsz=bsz),
        grid=(L // tc,),
        in_specs=[pl.BlockSpec((bsz, tc, w), lambda i: (0, i, 0)),
                  full(wb), full(wc), full(are), full(aim), full(d), full(wglu), full(bglu), full(gn)],
        out_specs=pl.BlockSpec((bsz, tc, w), lambda i: (0, i, 0)),
        out_shape=jax.ShapeDtypeStruct((bsz, L, w), BF16),
        scratch_shapes=[pltpu.VMEM((w // LANES, tc * bsz, LANES), F32),
                        pltpu.VMEM((tc * bsz, 2 * S5_COLS), F32),
                        pltpu.VMEM((w // LANES, tc * bsz, LANES), F32),
                        pltpu.VMEM((bsz, 2 * S5_COLS), F32)],
        compiler_params=_params(("arbitrary",)),
        name="s5",
    )(u, wb, wc, are, aim, d, wglu, bglu, gn)


def _attn_kernel(q_ref, k_ref, vt_ref, o_ref, sa, sb, mxa, mxb, m_s, l_s, acc_s, *, tq):
    qi = pl.program_id(2)
    nt = (((1,), (1,)), ((), ()))
    qs = [q_ref[0, :, hh * HEAD_PAD:(hh + 1) * HEAD_PAD] for hh in range(2)]

    m_s[...] = jnp.full(m_s.shape, MASK_VALUE, F32)
    l_s[...] = jnp.zeros_like(l_s)
    acc_s[...] = jnp.zeros_like(acc_s)

    def scores_into(kb, sbuf, mxbuf):
        ks = pl.multiple_of(kb * tq, tq)
        for hh in range(2):
            st = lax.dot_general(k_ref[0, pl.ds(ks, tq), hh * HEAD_PAD:(hh + 1) * HEAD_PAD], qs[hh], nt,
                                 preferred_element_type=F32)
            sbuf[hh] = st
            mxbuf[hh] = jnp.max(st, axis=0, keepdims=True)

    def softmax_from(kb, sbuf, mxbuf, masked):
        ks = pl.multiple_of(kb * tq, tq)
        for hh in range(2):
            st = sbuf[hh]
            if masked:
                kc = lax.broadcasted_iota(jnp.int32, (tq, tq), 0) // CHUNK
                qc = lax.broadcasted_iota(jnp.int32, (tq, tq), 1) // CHUNK
                st = jnp.where(kc <= qc, st, MASK_VALUE)
                mx = jnp.max(st, axis=0, keepdims=True)
            else:
                mx = mxbuf[hh]
            vt = vt_ref[0, hh * MLA_V:(hh + 1) * MLA_V, pl.ds(ks, tq)]
            m = m_s[hh]
            m_new = jnp.maximum(m, mx)
            alpha = jnp.exp2(m - m_new)
            pt = jnp.exp2(st - m_new)
            l_s[hh] = alpha * l_s[hh] + jnp.sum(pt, axis=0, keepdims=True)
            acc_s[hh] = alpha * acc_s[hh] + jnp.dot(vt, pt.astype(BF16), preferred_element_type=F32)
            m_s[hh] = m_new

    def finish():
        o_ref[0] = jnp.concatenate([acc_s[0] / l_s[0], acc_s[1] / l_s[1]], axis=0).T.astype(BF16)

    scores_into(0, sa, mxa)

    def pair(j, carry):
        scores_into(2 * j + 1, sb, mxb)
        softmax_from(2 * j, sa, mxa, False)
        scores_into(2 * j + 2, sa, mxa)
        softmax_from(2 * j + 1, sb, mxb, False)
        return carry

    lax.fori_loop(0, qi // 2, pair, 0)

    @pl.when(qi % 2 == 0)
    def _():
        softmax_from(qi, sa, mxa, True)
        finish()

    @pl.when(qi % 2 == 1)
    def _():
        scores_into(qi, sb, mxb)
        softmax_from(qi - 1, sa, mxa, False)
        softmax_from(qi, sb, mxb, True)
        finish()


def _attn(q, k, vt, tq):
    bsz, L, hp = q.shape
    return pl.pallas_call(
        functools.partial(_attn_kernel, tq=tq),
        grid=(bsz, MLA_HEADS // 2, L // tq),
        in_specs=[pl.BlockSpec((1, tq, 2 * HEAD_PAD), lambda b, h, i: (b, i, h)),
                  pl.BlockSpec((1, L, 2 * HEAD_PAD), lambda b, h, i: (b, 0, h)),
                  pl.BlockSpec((1, 2 * MLA_V, L), lambda b, h, i: (b, h, 0))],
        out_specs=pl.BlockSpec((1, tq, 2 * MLA_V), lambda b, h, i: (b, i, h)),
        out_shape=jax.ShapeDtypeStruct((bsz, L, MLA_WIDTH), BF16),
        scratch_shapes=[pltpu.VMEM((2, tq, tq), F32), pltpu.VMEM((2, tq, tq), F32),
                        pltpu.VMEM((2, 1, tq), F32), pltpu.VMEM((2, 1, tq), F32),
                        pltpu.VMEM((2, 1, tq), F32), pltpu.VMEM((2, 1, tq), F32),
                        pltpu.VMEM((2, MLA_V, tq), F32)],
        compiler_params=_params(("parallel", "parallel", "arbitrary")),
        name="attn",
    )(q, k, vt)


def _top16_rows(s, want_rank):
    work = s
    rank = jnp.full(s.shape, 127.0, F32) if want_rank else None
    ridx = lax.broadcasted_iota(jnp.int32, (PEER_TOPK, s.shape[1]), 0)
    vals = jnp.zeros((PEER_TOPK, s.shape[1]), F32)
    for r in range(PEER_TOPK):
        m = jnp.max(work, axis=0, keepdims=True)
        hit = work == m
        if want_rank:
            rank = jnp.where(hit, float(r), rank)
        work = jnp.where(hit, NEG_BIG, work)
        vals = jnp.where(ridx == r, m, vals)
    return vals, rank


def _mix_kernel(x_ref, ys_ref, ym_ref, g1_ref, sc2_ref, sh2_ref, gnm_ref, wo1_ref, wo2_ref,
                l1g_ref, l1b_ref, wq_ref, k1_ref, k2_ref,
                x1_ref, h2t_ref, cnt_ref, e1_ref, rk2_ref, e2_ref):
    x = x_ref[0]
    ym = ym_ref[0].astype(F32)
    ym = (ym * lax.rsqrt(jnp.mean(ym * ym, axis=-1, keepdims=True) + NORM_EPS) * gnm_ref[...]).astype(BF16)
    ymix = (jnp.dot(ys_ref[0], wo1_ref[...], preferred_element_type=F32)
            + jnp.dot(ym, wo2_ref[...], preferred_element_type=F32))
    r = DEEPNORM_ALPHA * x + (1.0 + g1_ref[0]) * ymix
    mu = jnp.mean(r, axis=-1, keepdims=True)
    rc = r - mu
    var = jnp.mean(rc * rc, axis=-1, keepdims=True)
    x1 = rc * lax.rsqrt(var + NORM_EPS) * l1g_ref[...] + l1b_ref[...]
    x1_ref[0] = x1
    h2 = x1 * (1.0 + sc2_ref[0]) + sh2_ref[0]
    h2t = h2.T.astype(BF16)
    h2t_ref[...] = h2t
    qt = jnp.dot(wq_ref[...], h2t, preferred_element_type=F32)
    for hh in range(PEER_HEADS):
        q1 = qt[hh * PEER_QDIM:hh * PEER_QDIM + PEER_HALF]
        q2 = qt[hh * PEER_QDIM + PEER_HALF:(hh + 1) * PEER_QDIM]
        s1 = jnp.dot(k1_ref[hh], q1.astype(BF16), preferred_element_type=F32)
        s2 = jnp.dot(k2_ref[hh], q2.astype(BF16), preferred_element_type=F32)
        v1, _ = _top16_rows(s1, False)
        v2, rk2 = _top16_rows(s2, True)
        cands = [v1[0:1] + v2]
        for a in range(1, 8):
            cands.append(v1[a:a + 1] + v2[0:8])
        cands.append(v1[8:16] + v2[0:1])
        top, _ = _top16_rows(jnp.concatenate(cands, axis=0), False)
        theta = top[PEER_TOPK - 1:PEER_TOPK]
        z = jnp.sum(jnp.exp(top - top[0:1]), axis=0, keepdims=True)
        dense_b = 4
        cnt = jnp.zeros_like(s1)
        for b in range(dense_b):
            cnt = cnt + jnp.where(s1 + v2[b:b + 1] >= theta, 1.0, 0.0)
        for a in range(3):
            extra = jnp.sum(jnp.where(v1[a:a + 1] + v2[dense_b:] >= theta, 1.0, 0.0), axis=0, keepdims=True)
            cnt = cnt + jnp.where(s1 == v1[a:a + 1], extra, 0.0)
        cnt_ref[hh] = jnp.where(s1 >= v1[PEER_TOPK - 1:PEER_TOPK], cnt, 0.0)
        e1_ref[hh] = jnp.exp(s1 - v1[0:1])
        rk2_ref[hh] = rk2.astype(BF16)
        e2_ref[hh] = (jnp.exp(s2 - v2[0:1]) / z).astype(BF16)


def _mix(x, ys, ym, g1, sc2, sh2, gnm, wo1, wo2, l1g, l1b, wqt, k1, k2, tt):
    bsz, L, dm = x.shape
    n = bsz * L
    nl = L // tt
    full = lambda a: pl.BlockSpec(a.shape, lambda b, i: (0,) * a.ndim)
    tok = lambda w: pl.BlockSpec((1, tt, w), lambda b, i: (b, i, 0))
    vec = pl.BlockSpec((1, 1, dm), lambda b, i: (b, 0, 0))
    tab = pl.BlockSpec((PEER_HEADS, PEER_NKEYS, tt), lambda b, i: (0, 0, b * nl + i))
    tab_shape = jax.ShapeDtypeStruct((PEER_HEADS, PEER_NKEYS, n), F32)
    tab_bf16 = jax.ShapeDtypeStruct((PEER_HEADS, PEER_NKEYS, n), BF16)
    return pl.pallas_call(
        _mix_kernel,
        grid=(bsz, nl),
        in_specs=[tok(dm), tok(S5_WIDTH), tok(MLA_WIDTH), vec, vec, vec,
                  full(gnm), full(wo1), full(wo2), full(l1g), full(l1b), full(wqt), full(k1), full(k2)],
        out_specs=[tok(dm), pl.BlockSpec((dm, tt), lambda b, i: (0, b * nl + i)), tab, tab, tab, tab],
        out_shape=[jax.ShapeDtypeStruct((bsz, L, dm), F32), jax.ShapeDtypeStruct((dm, n), BF16),
                   tab_shape, tab_shape, tab_bf16, tab_bf16],
        compiler_params=_params(("parallel", "parallel")),
        name="mix",
    )(x, ys, ym, g1, sc2, sh2, gnm, wo1, wo2, l1g, l1b, wqt, k1, k2)


def _peer_kernel(h2t_ref, cnt_ref, e1_ref, rk2_ref, e2_ref, u_ref, vt_ref, x1_ref, g2_ref, l2g_ref, l2b_ref,
                 o_ref, acc, wa0, wa1, *, ni, sub, nq):
    eb = pl.program_id(1)
    ne = pl.num_programs(1) - 1

    @pl.when(eb == 0)
    def _():
        acc[...] = jnp.zeros_like(acc)
        wa1[...] = jnp.zeros_like(wa1)

    dm = acc.shape[0]

    def consume(wa_r, q, nq):
        rows = slice(q * (dm // nq), (q + 1) * (dm // nq))
        acc[rows, :] += jnp.dot(vt_ref[0, rows, :], wa_r[...], preferred_element_type=F32)

    def step(wa_w, wa_r):
        sc = None
        for ii in range(ni):
            i = eb * ni + ii
            w = None
            for hh in range(PEER_HEADS):
                c = cnt_ref[hh, pl.ds(i, 1), :].astype(BF16)
                e = e1_ref[hh, pl.ds(i, 1), :].astype(BF16)
                term = jnp.where(rk2_ref[hh] < c, e2_ref[hh], jnp.zeros((), BF16)) * e
                w = term if w is None else w + term
            if ii % sub == 0:
                sc = jnp.dot(u_ref[ii * PEER_NKEYS:(ii + sub) * PEER_NKEYS, :], h2t_ref[...],
                             preferred_element_type=F32)
            if ii % (ni // nq) == 0:
                consume(wa_r, ii // (ni // nq), nq)
            rows = slice((ii % sub) * PEER_NKEYS, (ii % sub + 1) * PEER_NKEYS)
            wa_w[ii * PEER_NKEYS:(ii + 1) * PEER_NKEYS, :] = w * _gelu(sc[rows].astype(BF16))

    @pl.when((eb % 2 == 0) & (eb < ne))
    def _():
        step(wa0, wa1)

    @pl.when(eb % 2 == 1)
    def _():
        step(wa1, wa0)

    @pl.when(eb == ne)
    def _():
        consume(wa1, 0, 1)
        yff = acc[...].T
        r = DEEPNORM_ALPHA * x1_ref[0] + (1.0 + g2_ref[0]) * yff
        mu = jnp.mean(r, axis=-1, keepdims=True)
        rc = r - mu
        var = jnp.mean(rc * rc, axis=-1, keepdims=True)
        o_ref[0] = rc * lax.rsqrt(var + NORM_EPS) * l2g_ref[...] + l2b_ref[...]


def _peer(h2t, cnt, e1, rk2, e2, ub, vt, x1, g2, l2g, l2b, tt, ni):
    bsz, L, dm = x1.shape
    nl = L // tt
    eb = ni * PEER_NKEYS
    tab = pl.BlockSpec((PEER_HEADS, PEER_NKEYS, tt), lambda t, e: (0, 0, t))
    full = lambda a: pl.BlockSpec(a.shape, lambda t, e: (0,) * a.ndim)
    ne = PEER_NKEYS // ni
    assert ne % 2 == 0 and ni % 2 == 0
    return pl.pallas_call(
        functools.partial(_peer_kernel, ni=ni, sub=2, nq=4),
        grid=(bsz * nl, ne + 1),
        in_specs=[pl.BlockSpec((dm, tt), lambda t, e: (0, t)), tab, tab, tab, tab,
                  pl.BlockSpec((eb, dm), lambda t, e: (jnp.minimum(e, ne - 1), 0)),
                  pl.BlockSpec((1, dm, eb), lambda t, e: (jnp.maximum(e - 1, 0), 0, 0)),
                  pl.BlockSpec((1, tt, dm), lambda t, e: (t // nl, t % nl, 0)),
                  pl.BlockSpec((1, 1, dm), lambda t, e: (t // nl, 0, 0)),
                  full(l2g), full(l2b)],
        out_specs=pl.BlockSpec((1, tt, dm), lambda t, e: (t // nl, t % nl, 0)),
        out_shape=jax.ShapeDtypeStruct((bsz, L, dm), F32),
        scratch_shapes=[pltpu.VMEM((dm, tt), F32), pltpu.VMEM((eb, tt), BF16), pltpu.VMEM((eb, tt), BF16)],
        compiler_params=_params(("parallel", "arbitrary")),
        name="peer",
    )(h2t, cnt, e1, rk2, e2, ub, vt, x1, g2, l2g, l2b)


def _s5_discretize(lam_re, lam_im, log_dt, b_re, b_im, c_re, c_im):
    dt = jnp.exp(log_dt)[:, None]
    mag = jnp.exp(lam_re * dt)
    ab_re = mag * jnp.cos(lam_im * dt)
    ab_im = mag * jnp.sin(lam_im * dt)
    den = lam_re * lam_re + lam_im * lam_im
    nr = ab_re - 1.0
    coef_re = (nr * lam_re + ab_im * lam_im) / den
    coef_im = (ab_im * lam_re - nr * lam_im) / den
    bb_re = coef_re[..., None] * b_re - coef_im[..., None] * b_im
    bb_im = coef_re[..., None] * b_im + coef_im[..., None] * b_re
    eye = jnp.eye(S5_GROUPS, dtype=F32)
    wb_re = jnp.einsum("gph,gk->ghkp", bb_re, eye).reshape(S5_WIDTH, S5_COLS)
    wb_im = jnp.einsum("gph,gk->ghkp", bb_im, eye).reshape(S5_WIDTH, S5_COLS)
    wb = jnp.concatenate([wb_re, wb_im], axis=1).astype(BF16)
    wc_re = jnp.einsum("ghp,gk->gpkh", c_re, eye).reshape(S5_COLS, S5_WIDTH)
    wc_im = jnp.einsum("ghp,gk->gpkh", c_im, eye).reshape(S5_COLS, S5_WIDTH)
    wc = jnp.concatenate([wc_re, -wc_im], axis=0).astype(BF16)
    return wb, wc, ab_re.reshape(1, S5_COLS), ab_im.reshape(1, S5_COLS)


def _pad_heads(w, width, offset):
    kdim = w.shape[0]
    w = w.reshape(kdim, MLA_HEADS, width)
    out = jnp.zeros((kdim, MLA_HEADS, HEAD_PAD), w.dtype)
    out = out.at[:, :, offset:offset + width].set(w)
    return out.reshape(kdim, MLA_HEADS * HEAD_PAD)


def _swap_halves(w):
    half = w.shape[-1] // 2
    return jnp.concatenate([w[..., half:], w[..., :half]], axis=-1)


def kernel(x, c, positions, w_ada, b_ada, w_in, s5_lambda_re, s5_lambda_im, s5_log_dt, s5_b_re, s5_b_im, s5_c_re, s5_c_im, s5_d, s5_w_glu, s5_b_glu, mla_q_norm, mla_w_uq, mla_kv_norm, mla_w_ukv, gn_s5, gn_mla, w_out, ln1_g, ln1_b, peer_w_query, peer_keys1, peer_keys2, peer_u, peer_v, ln2_g, ln2_b):
    bsz, L, dm = x.shape
    depth = w_ada.shape[0]
    t_proj = min(512, L)
    t_s5 = min(128, L)
    t_attn = min(512, L)
    t_mix = min(256, L)
    t_peer = min(512, L)
    peer_ni = 8

    inv_freq = ROPE_THETA ** (-jnp.arange(0, MLA_ROPE, 2, dtype=F32) / MLA_ROPE)
    half = MLA_ROPE // 2
    invf = jnp.zeros((1, HEAD_PAD), F32).at[0, MLA_NOPE:MLA_NOPE + half].set(inv_freq)
    invf = invf.at[0, MLA_NOPE + half:MLA_NOPE + MLA_ROPE].set(inv_freq)
    sgn = jnp.zeros((1, HEAD_PAD), F32).at[0, MLA_NOPE:MLA_NOPE + half].set(-1.0)
    sgn = sgn.at[0, MLA_NOPE + half:MLA_NOPE + MLA_ROPE].set(1.0)
    posf = positions.astype(F32)[..., None]

    for l in range(depth):
        mod = _ada(c, w_ada[l], b_ada[l])
        sh1, sc1, g1, sh2, sc2, g2 = [m[:, None, :] for m in jnp.split(mod, 6, axis=-1)]

        wi = w_in[l]
        o1, o2, o3 = S5_WIDTH, S5_WIDTH + Q_LORA, S5_WIDTH + Q_LORA + KV_LORA
        w_kr = wi[:, o3:]
        zpad = lambda w: jnp.zeros((dm, HEAD_PAD), F32).at[:, MLA_NOPE:MLA_NOPE + MLA_ROPE].set(w)
        win = jnp.concatenate([wi[:, :o3], zpad(w_kr), zpad(_swap_halves(w_kr))], axis=1).astype(BF16)
        scale = (MLA_NOPE + MLA_ROPE) ** -0.5 * math.log2(math.e)
        wuq = (mla_w_uq[l] * mla_q_norm[l][:, None] * scale).reshape(Q_LORA, MLA_HEADS, MLA_NOPE + MLA_ROPE)
        wuq_main = wuq.reshape(Q_LORA, MLA_HEADS * (MLA_NOPE + MLA_ROPE))
        wqa = _pad_heads(wuq_main, MLA_NOPE + MLA_ROPE, 0).astype(BF16)
        wuq_sw = _swap_halves(wuq[:, :, MLA_NOPE:]).reshape(Q_LORA, MLA_HEADS * MLA_ROPE)
        wqb = _pad_heads(wuq_sw, MLA_ROPE, MLA_NOPE).astype(BF16)
        wukv = (mla_w_ukv[l] * mla_kv_norm[l][:, None]).reshape(KV_LORA, MLA_HEADS, MLA_NOPE + MLA_V)
        wk = _pad_heads(wukv[:, :, :MLA_NOPE].reshape(KV_LORA, MLA_HEADS * MLA_NOPE), MLA_NOPE, 0).astype(BF16)
        wv = wukv[:, :, MLA_NOPE:].reshape(KV_LORA, MLA_HEADS * MLA_V).T.astype(BF16)

        u, q, k, v = _proj(x, posf, sc1, sh1, win, wqa, wqb, wk, wv, invf, sgn, t_proj)

        wb, wc, are, aim = _s5_discretize(s5_lambda_re[l], s5_lambda_im[l], s5_log_dt[l], s5_b_re[l],
                                          s5_b_im[l], s5_c_re[l], s5_c_im[l])
        are = jnp.broadcast_to(are, (bsz, S5_COLS))
        aim = jnp.broadcast_to(aim, (bsz, S5_COLS))
        y_s5 = _s5(u, wb, wc, are, aim, s5_d[l][None], s5_w_glu[l].astype(BF16), s5_b_glu[l][None],
                   gn_s5[l][None], t_s5)

        y_mla = _attn(q, k, v, t_attn)

        wo = w_out[l].astype(BF16)
        wqt = peer_w_query[l].T.astype(BF16)
        x1, h2t, cnt, e1, rk2, e2 = _mix(x, y_s5, y_mla, g1, sc2, sh2, gn_mla[l][None], wo[:S5_WIDTH],
                                         wo[S5_WIDTH:], ln1_g[l][None], ln1_b[l][None], wqt,
                                         peer_keys1[l].astype(BF16), peer_keys2[l].astype(BF16), t_mix)

        vt = peer_v[l].astype(BF16).reshape(PEER_NKEYS // peer_ni, peer_ni * PEER_NKEYS, dm).transpose(0, 2, 1)
        x = _peer(h2t, cnt, e1, rk2, e2, peer_u[l].astype(BF16), vt, x1, g2,
                  ln2_g[l][None], ln2_b[l][None], t_peer, peer_ni)
    return x
```

```python
import functools
import math

import jax
import jax.numpy as jnp
from jax import lax
from jax.experimental import pallas as pl
from jax.experimental.pallas import tpu as pltpu

F32 = jnp.float32
BF16 = jnp.bfloat16
FP8 = jnp.float8_e4m3fn
FP8_TOP = 256.0

D_MODEL = 1024
CHUNK = 64
S5_WIDTH = 512
S5_GROUP = 16
S5_GROUPS = 32
S5_STATE = 64
S5_COLS = S5_GROUPS * S5_STATE
MLA_WIDTH = 512
MLA_HEADS = 8
MLA_NOPE = 64
MLA_ROPE = 32
MLA_V = 64
Q_LORA = 384
KV_LORA = 256
ROPE_THETA = 10000.0
HEAD_PAD = 128
PEER_HEADS = 8
PEER_NKEYS = 128
PEER_QDIM = 256
PEER_HALF = 128
PEER_TOPK = 16
PEER_EXPERTS = PEER_NKEYS * PEER_NKEYS
DEEPNORM_ALPHA = 2.0 ** 0.25
NORM_EPS = 1e-6
MASK_VALUE = -1e30
NEG_BIG = -3.0e38

LANES = 128
SUBLANES = 8
VMEM_LIMIT = 56 * 1024 * 1024


def _gelu(x):
    return 0.5 * x * (1.0 + jnp.tanh(0.7978845608028654 * (x + 0.044715 * (x * x * x))))


def _sigmoid(x):
    return 1.0 / (1.0 + jnp.exp(-x))


def _params(sem, flags=None):
    return pltpu.CompilerParams(dimension_semantics=sem, vmem_limit_bytes=VMEM_LIMIT, flags=flags)


def _ada_kernel(c_ref, w_ref, b_ref, o_ref):
    c = c_ref[...]
    ca = c * _sigmoid(c)
    o_ref[...] = jnp.dot(ca, w_ref[...], preferred_element_type=F32,
                         precision=lax.Precision.HIGHEST) + b_ref[...]


def _ada(c, w_ada, b_ada):
    bsz, dm = c.shape
    n = w_ada.shape[1]
    tn = 1024
    return pl.pallas_call(
        _ada_kernel,
        grid=(n // tn,),
        in_specs=[pl.BlockSpec((bsz, dm), lambda j: (0, 0)),
                  pl.BlockSpec((dm, tn), lambda j: (0, j)),
                  pl.BlockSpec((1, tn), lambda j: (0, j))],
        out_specs=pl.BlockSpec((bsz, tn), lambda j: (0, j)),
        out_shape=jax.ShapeDtypeStruct((bsz, n), F32),
        compiler_params=_params(("arbitrary",)),
        name="ada",
    )(c, w_ada, b_ada.reshape(1, n))


def _proj_kernel(x_ref, pos_ref, sc_ref, sh_ref, win_ref, wqa_ref, wqb_ref, wk_ref, wv_ref,
                 invf_ref, sgn_ref, u_ref, q_ref, k_ref, v_ref):
    x = x_ref[0]
    h = (x * (1.0 + sc_ref[0]) + sh_ref[0]).astype(BF16)
    proj = jnp.dot(h, win_ref[...], preferred_element_type=F32)
    o = S5_WIDTH
    u_ref[0] = proj[:, :o]
    cq = proj[:, o:o + Q_LORA]
    o += Q_LORA
    ckv = proj[:, o:o + KV_LORA]
    o += KV_LORA
    kra = proj[:, o:o + HEAD_PAD]
    krb = proj[:, o + HEAD_PAD:o + 2 * HEAD_PAD]
    ang = pos_ref[0] * invf_ref[...]
    cs = jnp.cos(ang)
    sn = jnp.sin(ang) * sgn_ref[...]
    cqn = (cq * lax.rsqrt(jnp.mean(cq * cq, axis=-1, keepdims=True) + NORM_EPS)).astype(BF16)
    qa = jnp.dot(cqn, wqa_ref[...], preferred_element_type=F32)
    qb = jnp.dot(cqn, wqb_ref[...], preferred_element_type=F32)
    ckvn = (ckv * lax.rsqrt(jnp.mean(ckv * ckv, axis=-1, keepdims=True) + NORM_EPS)).astype(BF16)
    kn = jnp.dot(ckvn, wk_ref[...], preferred_element_type=F32)
    kp = kra * cs + krb * sn
    for hh in range(MLA_HEADS):
        sl = slice(hh * HEAD_PAD, (hh + 1) * HEAD_PAD)
        q_ref[0, :, sl] = (qa[:, sl] * cs + qb[:, sl] * sn).astype(BF16)
        k_ref[0, :, sl] = (kn[:, sl] + kp).astype(BF16)
    v_ref[0] = lax.dot_general(wv_ref[...], ckvn, (((1,), (1,)), ((), ())),
                               preferred_element_type=F32).astype(BF16)


def _proj(x, posf, sc1, sh1, win, wqa, wqb, wk, wv, invf, sgn, tt):
    bsz, L, dm = x.shape
    hp = MLA_HEADS * HEAD_PAD
    full = lambda a: pl.BlockSpec(a.shape, lambda b, i: (0,) * a.ndim)
    return pl.pallas_call(
        _proj_kernel,
        grid=(bsz, L // tt),
        in_specs=[pl.BlockSpec((1, tt, dm), lambda b, i: (b, i, 0)),
                  pl.BlockSpec((1, tt, 1), lambda b, i: (b, i, 0)),
                  pl.BlockSpec((1, 1, dm), lambda b, i: (b, 0, 0)),
                  pl.BlockSpec((1, 1, dm), lambda b, i: (b, 0, 0)),
                  full(win), full(wqa), full(wqb), full(wk), full(wv), full(invf), full(sgn)],
        out_specs=[pl.BlockSpec((1, tt, S5_WIDTH), lambda b, i: (b, i, 0)),
                   pl.BlockSpec((1, tt, hp), lambda b, i: (b, i, 0)),
                   pl.BlockSpec((1, tt, hp), lambda b, i: (b, i, 0)),
                   pl.BlockSpec((1, MLA_WIDTH, tt), lambda b, i: (b, 0, i))],
        out_shape=[jax.ShapeDtypeStruct((bsz, L, S5_WIDTH), F32),
                   jax.ShapeDtypeStruct((bsz, L, hp), BF16),
                   jax.ShapeDtypeStruct((bsz, L, hp), BF16),
                   jax.ShapeDtypeStruct((bsz, MLA_WIDTH, L), BF16)],
        compiler_params=_params(("parallel", "parallel")),
        name="proj",
    )(x, posf, sc1, sh1, win, wqa, wqb, wk, wv, invf, sgn)


def _s5_kernel(u_ref, wb_ref, wc_ref, are_ref, aim_ref, d_ref, wglu_ref, bglu_ref, gn_ref,
               y_ref, utm, st, yt, carry, *, tc, bsz):
    @pl.when(pl.program_id(0) == 0)
    def _():
        carry[...] = jnp.zeros_like(carry)

    nslab = S5_WIDTH // LANES
    for b in range(bsz):
        for j in range(nslab):
            utm[j, pl.ds(b, tc, stride=bsz), :] = u_ref[b, :, j * LANES:(j + 1) * LANES]
    u = jnp.concatenate([utm[j] for j in range(nslab)], axis=-1)
    st[...] = jnp.dot(u.astype(BF16), wb_ref[...], preferred_element_type=F32)

    cb = 512
    for c0 in range(0, S5_COLS, cb):
        cre = slice(c0, c0 + cb)
        cim = slice(S5_COLS + c0, S5_COLS + c0 + cb)
        are = are_ref[:, cre]
        aim = aim_ref[:, cre]

        def step(t, sc, cre=cre, cim=cim, are=are, aim=aim):
            sre, sim = sc
            r = pl.multiple_of(t * bsz, bsz)
            nre = are * sre - aim * sim + st[pl.ds(r, bsz), cre]
            nim = are * sim + aim * sre + st[pl.ds(r, bsz), cim]
            st[pl.ds(r, bsz), cre] = nre
            st[pl.ds(r, bsz), cim] = nim
            return nre, nim

        sre, sim = lax.fori_loop(0, tc, step, (carry[:, cre], carry[:, cim]), unroll=4)
        carry[:, cre] = sre
        carry[:, cim] = sim

    y = jnp.dot(st[...].astype(BF16), wc_ref[...], preferred_element_type=F32) + d_ref[...] * u
    y = _gelu(y)
    z = jnp.dot(y.astype(BF16), wglu_ref[...], preferred_element_type=F32) + bglu_ref[...]
    y = y * _sigmoid(z)
    y = y * lax.rsqrt(jnp.mean(y * y, axis=-1, keepdims=True) + NORM_EPS) * gn_ref[...]
    for j in range(nslab):
        yt[j] = y[:, j * LANES:(j + 1) * LANES]
    for b in range(bsz):
        for j in range(nslab):
            y_ref[b, :, j * LANES:(j + 1) * LANES] = yt[j, pl.ds(b, tc, stride=bsz), :].astype(BF16)


def _s5(u, wb, wc, are, aim, d, wglu, bglu, gn, tc):
    bsz, L, w = u.shape
    full = lambda a: pl.BlockSpec(a.shape, lambda i: (0,) * a.ndim)
    return pl.pallas_call(
        functools.partial(_s5_kernel, tc=tc, bsz=bsz),
        grid=(L // tc,),
        in_specs=[pl.BlockSpec((bsz, tc, w), lambda i: (0, i, 0)),
                  full(wb), full(wc), full(are), full(aim), full(d), full(wglu), full(bglu), full(gn)],
        out_specs=pl.BlockSpec((bsz, tc, w), lambda i: (0, i, 0)),
        out_shape=jax.ShapeDtypeStruct((bsz, L, w), BF16),
        scratch_shapes=[pltpu.VMEM((w // LANES, tc * bsz, LANES), F32),
                        pltpu.VMEM((tc * bsz, 2 * S5_COLS), F32),
                        pltpu.VMEM((w // LANES, tc * bsz, LANES), F32),
                        pltpu.VMEM((bsz, 2 * S5_COLS), F32)],
        compiler_params=_params(("arbitrary",)),
        name="s5",
    )(u, wb, wc, are, aim, d, wglu, bglu, gn)


def _attn_kernel(q_ref, k_ref, vt_ref, o_ref, sa, sb, mxa, mxb, m_s, l_s, acc_s, *, tq):
    qi = pl.program_id(2)
    nt = (((1,), (1,)), ((), ()))
    qs = [q_ref[0, :, hh * HEAD_PAD:(hh + 1) * HEAD_PAD] for hh in range(2)]

    m_s[...] = jnp.full(m_s.shape, MASK_VALUE, F32)
    l_s[...] = jnp.zeros_like(l_s)
    acc_s[...] = jnp.zeros_like(acc_s)

    def scores_into(kb, sbuf, mxbuf):
        ks = pl.multiple_of(kb * tq, tq)
        for hh in range(2):
            st = lax.dot_general(k_ref[0, pl.ds(ks, tq), hh * HEAD_PAD:(hh + 1) * HEAD_PAD], qs[hh], nt,
                                 preferred_element_type=F32)
            sbuf[hh] = st
            mxbuf[hh] = jnp.max(st, axis=0, keepdims=True)

    def softmax_from(kb, sbuf, mxbuf, masked):
        ks = pl.multiple_of(kb * tq, tq)
        for hh in range(2):
            st = sbuf[hh]
            if masked:
                kc = lax.broadcasted_iota(jnp.int32, (tq, tq), 0) // CHUNK
                qc = lax.broadcasted_iota(jnp.int32, (tq, tq), 1) // CHUNK
                st = jnp.where(kc <= qc, st, MASK_VALUE)
                mx = jnp.max(st, axis=0, keepdims=True)
            else:
                mx = mxbuf[hh]
            vt = vt_ref[0, hh * MLA_V:(hh + 1) * MLA_V, pl.ds(ks, tq)]
            m = m_s[hh]
            m_new = jnp.maximum(m, mx)
            alpha = jnp.exp2(m - m_new)
            pt = jnp.exp2(st - m_new)
            l_s[hh] = alpha * l_s[hh] + jnp.sum(pt, axis=0, keepdims=True)
            acc_s[hh] = alpha * acc_s[hh] + jnp.dot(vt, pt.astype(BF16), preferred_element_type=F32)
            m_s[hh] = m_new

    def finish():
        o_ref[0] = jnp.concatenate([acc_s[0] / l_s[0], acc_s[1] / l_s[1]], axis=0).T.astype(BF16)

    scores_into(0, sa, mxa)

    def pair(j, carry):
        scores_into(2 * j + 1, sb, mxb)
        softmax_from(2 * j, sa, mxa, False)
        scores_into(2 * j + 2, sa, mxa)
        softmax_from(2 * j + 1, sb, mxb, False)
        return carry

    lax.fori_loop(0, qi // 2, pair, 0)

    @pl.when(qi % 2 == 0)
    def _():
        softmax_from(qi, sa, mxa, True)
        finish()

    @pl.when(qi % 2 == 1)
    def _():
        scores_into(qi, sb, mxb)
        softmax_from(qi - 1, sa, mxa, False)
        softmax_from(qi, sb, mxb, True)
        finish()


def _attn(q, k, vt, tq):
    bsz, L, hp = q.shape
    return pl.pallas_call(
        functools.partial(_attn_kernel, tq=tq),
        grid=(bsz, MLA_HEADS // 2, L // tq),
        in_specs=[pl.BlockSpec((1, tq, 2 * HEAD_PAD), lambda b, h, i: (b, i, h)),
                  pl.BlockSpec((1, L, 2 * HEAD_PAD), lambda b, h, i: (b, 0, h)),
                  pl.BlockSpec((1, 2 * MLA_V, L), lambda b, h, i: (b, h, 0))],
        out_specs=pl.BlockSpec((1, tq, 2 * MLA_V), lambda b, h, i: (b, i, h)),
        out_shape=jax.ShapeDtypeStruct((bsz, L, MLA_WIDTH), BF16),
        scratch_shapes=[pltpu.VMEM((2, tq, tq), F32), pltpu.VMEM((2, tq, tq), F32),
                        pltpu.VMEM((2, 1, tq), F32), pltpu.VMEM((2, 1, tq), F32),
                        pltpu.VMEM((2, 1, tq), F32), pltpu.VMEM((2, 1, tq), F32),
                        pltpu.VMEM((2, MLA_V, tq), F32)],
        compiler_params=_params(("parallel", "parallel", "arbitrary")),
        name="attn",
    )(q, k, vt)


def _top16_rows(s, want_rank):
    work = s
    rank = jnp.full(s.shape, 127.0, F32) if want_rank else None
    ridx = lax.broadcasted_iota(jnp.int32, (PEER_TOPK, s.shape[1]), 0)
    vals = jnp.zeros((PEER_TOPK, s.shape[1]), F32)
    for r in range(PEER_TOPK):
        m = jnp.max(work, axis=0, keepdims=True)
        hit = work == m
        if want_rank:
            rank = jnp.where(hit, float(r), rank)
        work = jnp.where(hit, NEG_BIG, work)
        vals = jnp.where(ridx == r, m, vals)
    return vals, rank


def _mix_kernel(x_ref, ys_ref, ym_ref, g1_ref, sc2_ref, sh2_ref, gnm_ref, wo1_ref, wo2_ref,
                l1g_ref, l1b_ref, wq_ref, k1_ref, k2_ref,
                x1_ref, h8_ref, hinv_ref, cnt_ref, e1_ref, rk2_ref, e2_ref):
    x = x_ref[0]
    ym = ym_ref[0].astype(F32)
    ym = (ym * lax.rsqrt(jnp.mean(ym * ym, axis=-1, keepdims=True) + NORM_EPS) * gnm_ref[...]).astype(BF16)
    ymix = (jnp.dot(ys_ref[0], wo1_ref[...], preferred_element_type=F32)
            + jnp.dot(ym, wo2_ref[...], preferred_element_type=F32))
    r = DEEPNORM_ALPHA * x + (1.0 + g1_ref[0]) * ymix
    mu = jnp.mean(r, axis=-1, keepdims=True)
    rc = r - mu
    var = jnp.mean(rc * rc, axis=-1, keepdims=True)
    x1 = rc * lax.rsqrt(var + NORM_EPS) * l1g_ref[...] + l1b_ref[...]
    x1_ref[0] = x1
    h2 = x1 * (1.0 + sc2_ref[0]) + sh2_ref[0]
    h2tf = h2.T
    amax = jnp.max(jnp.abs(h2tf), axis=0, keepdims=True)
    hinv = jnp.where(amax > 0.0, amax * (1.0 / FP8_TOP), 1.0)
    h8_ref[...] = (h2tf * (1.0 / hinv)).astype(FP8)
    hinv_ref[...] = hinv
    h2t = h2tf.astype(BF16)
    qt = jnp.dot(wq_ref[...], h2t, preferred_element_type=F32)
    for hh in range(PEER_HEADS):
        q1 = qt[hh * PEER_QDIM:hh * PEER_QDIM + PEER_HALF]
        q2 = qt[hh * PEER_QDIM + PEER_HALF:(hh + 1) * PEER_QDIM]
        s1 = jnp.dot(k1_ref[hh], q1.astype(BF16), preferred_element_type=F32)
        s2 = jnp.dot(k2_ref[hh], q2.astype(BF16), preferred_element_type=F32)
        v1, _ = _top16_rows(s1, False)
        v2, rk2 = _top16_rows(s2, True)
        cands = [v1[0:1] + v2]
        for a in range(1, 8):
            cands.append(v1[a:a + 1] + v2[0:8])
        cands.append(v1[8:16] + v2[0:1])
        top, _ = _top16_rows(jnp.concatenate(cands, axis=0), False)
        theta = top[PEER_TOPK - 1:PEER_TOPK]
        z = jnp.sum(jnp.exp(top - top[0:1]), axis=0, keepdims=True)
        dense_b = 4
        cnt = jnp.zeros_like(s1)
        for b in range(dense_b):
            cnt = cnt + jnp.where(s1 + v2[b:b + 1] >= theta, 1.0, 0.0)
        for a in range(3):
            extra = jnp.sum(jnp.where(v1[a:a + 1] + v2[dense_b:] >= theta, 1.0, 0.0), axis=0, keepdims=True)
            cnt = cnt + jnp.where(s1 == v1[a:a + 1], extra, 0.0)
        cnt_ref[hh] = jnp.where(s1 >= v1[PEER_TOPK - 1:PEER_TOPK], cnt, 0.0)
        e1_ref[hh] = jnp.exp(s1 - v1[0:1])
        rk2_ref[hh] = rk2.astype(BF16)
        e2_ref[hh] = (jnp.exp(s2 - v2[0:1]) / z).astype(BF16)


def _mix(x, ys, ym, g1, sc2, sh2, gnm, wo1, wo2, l1g, l1b, wqt, k1, k2, tt):
    bsz, L, dm = x.shape
    n = bsz * L
    nl = L // tt
    full = lambda a: pl.BlockSpec(a.shape, lambda b, i: (0,) * a.ndim)
    tok = lambda w: pl.BlockSpec((1, tt, w), lambda b, i: (b, i, 0))
    vec = pl.BlockSpec((1, 1, dm), lambda b, i: (b, 0, 0))
    tab = pl.BlockSpec((PEER_HEADS, PEER_NKEYS, tt), lambda b, i: (0, 0, b * nl + i))
    tab_shape = jax.ShapeDtypeStruct((PEER_HEADS, PEER_NKEYS, n), F32)
    tab_bf16 = jax.ShapeDtypeStruct((PEER_HEADS, PEER_NKEYS, n), BF16)
    return pl.pallas_call(
        _mix_kernel,
        grid=(bsz, nl),
        in_specs=[tok(dm), tok(S5_WIDTH), tok(MLA_WIDTH), vec, vec, vec,
                  full(gnm), full(wo1), full(wo2), full(l1g), full(l1b), full(wqt), full(k1), full(k2)],
        out_specs=[tok(dm), pl.BlockSpec((dm, tt), lambda b, i: (0, b * nl + i)),
                   pl.BlockSpec((1, tt), lambda b, i: (0, b * nl + i)), tab, tab, tab, tab],
        out_shape=[jax.ShapeDtypeStruct((bsz, L, dm), F32), jax.ShapeDtypeStruct((dm, n), FP8),
                   jax.ShapeDtypeStruct((1, n), F32), tab_shape, tab_shape, tab_bf16, tab_bf16],
        compiler_params=_params(("parallel", "parallel")),
        name="mix",
    )(x, ys, ym, g1, sc2, sh2, gnm, wo1, wo2, l1g, l1b, wqt, k1, k2)


def _peer_kernel(h8_ref, hs_ref, cnt_ref, e1_ref, rk2_ref, e2_ref, u_ref, vt_ref, x1_ref, g2_ref, l2g_ref, l2b_ref,
                 o_ref, acc, wa0, wa1, *, ni, sub, nq):
    eb = pl.program_id(1)
    ne = pl.num_programs(1) - 1
    dm = acc.shape[0]

    @pl.when(eb == 0)
    def _():
        acc[...] = jnp.zeros_like(acc)
        wa1[...] = jnp.zeros_like(wa1)

    def consume(wa_r, q, nq):
        rows = slice(q * (dm // nq), (q + 1) * (dm // nq))
        acc[rows, :] += jnp.dot(vt_ref[0, rows, :], wa_r[...], preferred_element_type=F32)

    def step(wa_w, wa_r):
        sc = None
        for ii in range(ni):
            i = eb * ni + ii
            if ii % sub == 0:
                sc = jnp.dot(u_ref[ii * PEER_NKEYS:(ii + sub) * PEER_NKEYS, :], h8_ref[...],
                             preferred_element_type=F32) * hs_ref[...]
            if ii % (ni // nq) == 0:
                consume(wa_r, ii // (ni // nq), nq)
            rows = slice((ii % sub) * PEER_NKEYS, (ii % sub + 1) * PEER_NKEYS)
            w = None
            for hh in range(PEER_HEADS):
                c = cnt_ref[hh, pl.ds(i, 1), :].astype(BF16)
                e = e1_ref[hh, pl.ds(i, 1), :].astype(BF16)
                term = jnp.where(rk2_ref[hh] < c, e2_ref[hh], jnp.zeros((), BF16)) * e
                w = term if w is None else w + term
            wa_w[ii * PEER_NKEYS:(ii + 1) * PEER_NKEYS, :] = w * _gelu(sc[rows].astype(BF16))

    @pl.when((eb % 2 == 0) & (eb < ne))
    def _():
        step(wa0, wa1)

    @pl.when(eb % 2 == 1)
    def _():
        step(wa1, wa0)

    @pl.when(eb == ne)
    def _():
        consume(wa1, 0, 1)
        yff = acc[...].T
        r = DEEPNORM_ALPHA * x1_ref[0] + (1.0 + g2_ref[0]) * yff
        mu = jnp.mean(r, axis=-1, keepdims=True)
        rc = r - mu
        var = jnp.mean(rc * rc, axis=-1, keepdims=True)
        o_ref[0] = rc * lax.rsqrt(var + NORM_EPS) * l2g_ref[...] + l2b_ref[...]


def _peer(h8, hs, cnt, e1, rk2, e2, ub, vt, x1, g2, l2g, l2b, tt, ni):
    bsz, L, dm = x1.shape
    nl = L // tt
    eb = ni * PEER_NKEYS
    tab = pl.BlockSpec((PEER_HEADS, PEER_NKEYS, tt), lambda t, e: (0, 0, t))
    full = lambda a: pl.BlockSpec(a.shape, lambda t, e: (0,) * a.ndim)
    ne = PEER_NKEYS // ni
    assert ne % 2 == 0 and ni % 4 == 0
    return pl.pallas_call(
        functools.partial(_peer_kernel, ni=ni, sub=2, nq=4),
        grid=(bsz * nl, ne + 1),
        in_specs=[pl.BlockSpec((dm, tt), lambda t, e: (0, t)), pl.BlockSpec((1, tt), lambda t, e: (0, t)),
                  tab, tab, tab, tab,
                  pl.BlockSpec((eb, dm), lambda t, e: (jnp.minimum(e, ne - 1), 0)),
                  pl.BlockSpec((1, dm, eb), lambda t, e: (jnp.maximum(e - 1, 0), 0, 0)),
                  pl.BlockSpec((1, tt, dm), lambda t, e: (t // nl, t % nl, 0)),
                  pl.BlockSpec((1, 1, dm), lambda t, e: (t // nl, 0, 0)),
                  full(l2g), full(l2b)],
        out_specs=pl.BlockSpec((1, tt, dm), lambda t, e: (t // nl, t % nl, 0)),
        out_shape=jax.ShapeDtypeStruct((bsz, L, dm), F32),
        scratch_shapes=[pltpu.VMEM((dm, tt), F32), pltpu.VMEM((eb, tt), BF16), pltpu.VMEM((eb, tt), BF16)],
        compiler_params=_params(("parallel", "arbitrary")),
        name="peer",
    )(h8, hs, cnt, e1, rk2, e2, ub, vt, x1, g2, l2g, l2b)


def _s5_discretize(lam_re, lam_im, log_dt, b_re, b_im, c_re, c_im):
    dt = jnp.exp(log_dt)[:, None]
    mag = jnp.exp(lam_re * dt)
    ab_re = mag * jnp.cos(lam_im * dt)
    ab_im = mag * jnp.sin(lam_im * dt)
    den = lam_re * lam_re + lam_im * lam_im
    nr = ab_re - 1.0
    coef_re = (nr * lam_re + ab_im * lam_im) / den
    coef_im = (ab_im * lam_re - nr * lam_im) / den
    bb_re = coef_re[..., None] * b_re - coef_im[..., None] * b_im
    bb_im = coef_re[..., None] * b_im + coef_im[..., None] * b_re
    eye = jnp.eye(S5_GROUPS, dtype=F32)
    wb_re = jnp.einsum("gph,gk->ghkp", bb_re, eye).reshape(S5_WIDTH, S5_COLS)
    wb_im = jnp.einsum("gph,gk->ghkp", bb_im, eye).reshape(S5_WIDTH, S5_COLS)
    wb = jnp.concatenate([wb_re, wb_im], axis=1).astype(BF16)
    wc_re = jnp.einsum("ghp,gk->gpkh", c_re, eye).reshape(S5_COLS, S5_WIDTH)
    wc_im = jnp.einsum("ghp,gk->gpkh", c_im, eye).reshape(S5_COLS, S5_WIDTH)
    wc = jnp.concatenate([wc_re, -wc_im], axis=0).astype(BF16)
    return wb, wc, ab_re.reshape(1, S5_COLS), ab_im.reshape(1, S5_COLS)


def _pad_heads(w, width, offset):
    kdim = w.shape[0]
    w = w.reshape(kdim, MLA_HEADS, width)
    out = jnp.zeros((kdim, MLA_HEADS, HEAD_PAD), w.dtype)
    out = out.at[:, :, offset:offset + width].set(w)
    return out.reshape(kdim, MLA_HEADS * HEAD_PAD)


def _swap_halves(w):
    half = w.shape[-1] // 2
    return jnp.concatenate([w[..., half:], w[..., :half]], axis=-1)


def kernel(x, c, positions, w_ada, b_ada, w_in, s5_lambda_re, s5_lambda_im, s5_log_dt, s5_b_re, s5_b_im, s5_c_re, s5_c_im, s5_d, s5_w_glu, s5_b_glu, mla_q_norm, mla_w_uq, mla_kv_norm, mla_w_ukv, gn_s5, gn_mla, w_out, ln1_g, ln1_b, peer_w_query, peer_keys1, peer_keys2, peer_u, peer_v, ln2_g, ln2_b):
    bsz, L, dm = x.shape
    depth = w_ada.shape[0]
    t_proj = min(512, L)
    t_s5 = min(128, L)
    t_attn = min(512, L)
    t_mix = min(256, L)
    t_peer = min(512, L)
    peer_ni = 8

    inv_freq = ROPE_THETA ** (-jnp.arange(0, MLA_ROPE, 2, dtype=F32) / MLA_ROPE)
    half = MLA_ROPE // 2
    invf = jnp.zeros((1, HEAD_PAD), F32).at[0, MLA_NOPE:MLA_NOPE + half].set(inv_freq)
    invf = invf.at[0, MLA_NOPE + half:MLA_NOPE + MLA_ROPE].set(inv_freq)
    sgn = jnp.zeros((1, HEAD_PAD), F32).at[0, MLA_NOPE:MLA_NOPE + half].set(-1.0)
    sgn = sgn.at[0, MLA_NOPE + half:MLA_NOPE + MLA_ROPE].set(1.0)
    posf = positions.astype(F32)[..., None]

    for l in range(depth):
        mod = _ada(c, w_ada[l], b_ada[l])
        sh1, sc1, g1, sh2, sc2, g2 = [m[:, None, :] for m in jnp.split(mod, 6, axis=-1)]

        wi = w_in[l]
        o1, o2, o3 = S5_WIDTH, S5_WIDTH + Q_LORA, S5_WIDTH + Q_LORA + KV_LORA
        w_kr = wi[:, o3:]
        zpad = lambda w: jnp.zeros((dm, HEAD_PAD), F32).at[:, MLA_NOPE:MLA_NOPE + MLA_ROPE].set(w)
        win = jnp.concatenate([wi[:, :o3], zpad(w_kr), zpad(_swap_halves(w_kr))], axis=1).astype(BF16)
        scale = (MLA_NOPE + MLA_ROPE) ** -0.5 * math.log2(math.e)
        wuq = (mla_w_uq[l] * mla_q_norm[l][:, None] * scale).reshape(Q_LORA, MLA_HEADS, MLA_NOPE + MLA_ROPE)
        wuq_main = wuq.reshape(Q_LORA, MLA_HEADS * (MLA_NOPE + MLA_ROPE))
        wqa = _pad_heads(wuq_main, MLA_NOPE + MLA_ROPE, 0).astype(BF16)
        wuq_sw = _swap_halves(wuq[:, :, MLA_NOPE:]).reshape(Q_LORA, MLA_HEADS * MLA_ROPE)
        wqb = _pad_heads(wuq_sw, MLA_ROPE, MLA_NOPE).astype(BF16)
        wukv = (mla_w_ukv[l] * mla_kv_norm[l][:, None]).reshape(KV_LORA, MLA_HEADS, MLA_NOPE + MLA_V)
        wk = _pad_heads(wukv[:, :, :MLA_NOPE].reshape(KV_LORA, MLA_HEADS * MLA_NOPE), MLA_NOPE, 0).astype(BF16)
        wv = wukv[:, :, MLA_NOPE:].reshape(KV_LORA, MLA_HEADS * MLA_V).T.astype(BF16)

        u, q, k, v = _proj(x, posf, sc1, sh1, win, wqa, wqb, wk, wv, invf, sgn, t_proj)

        wb, wc, are, aim = _s5_discretize(s5_lambda_re[l], s5_lambda_im[l], s5_log_dt[l], s5_b_re[l],
                                          s5_b_im[l], s5_c_re[l], s5_c_im[l])
        are = jnp.broadcast_to(are, (bsz, S5_COLS))
        aim = jnp.broadcast_to(aim, (bsz, S5_COLS))
        y_s5 = _s5(u, wb, wc, are, aim, s5_d[l][None], s5_w_glu[l].astype(BF16), s5_b_glu[l][None],
                   gn_s5[l][None], t_s5)

        y_mla = _attn(q, k, v, t_attn)

        wo = w_out[l].astype(BF16)
        wqt = peer_w_query[l].T.astype(BF16)
        x1, h8, hinv, cnt, e1, rk2, e2 = _mix(x, y_s5, y_mla, g1, sc2, sh2, gn_mla[l][None], wo[:S5_WIDTH],
                                              wo[S5_WIDTH:], ln1_g[l][None], ln1_b[l][None], wqt,
                                              peer_keys1[l].astype(BF16), peer_keys2[l].astype(BF16), t_mix)

        u_amax = jnp.max(jnp.abs(peer_u[l]))
        u_inv = jnp.where(u_amax > 0.0, u_amax * (1.0 / FP8_TOP), 1.0)
        u8 = (peer_u[l] * (1.0 / u_inv)).astype(FP8)
        vt = peer_v[l].astype(BF16).reshape(PEER_NKEYS // peer_ni, peer_ni * PEER_NKEYS, dm).transpose(0, 2, 1)
        x = _peer(h8, hinv * u_inv, cnt, e1, rk2, e2, u8, vt, x1, g2,
                  ln2_g[l][None], ln2_b[l][None], t_peer, peer_ni)
    return x
```

```python
import functools
import math

import jax
import jax.numpy as jnp
from jax import lax
from jax.experimental import pallas as pl
from jax.experimental.pallas import tpu as pltpu

F32 = jnp.float32
BF16 = jnp.bfloat16

D_MODEL = 1024
CHUNK = 64
S5_WIDTH = 512
S5_GROUP = 16
S5_GROUPS = 32
S5_STATE = 64
S5_COLS = S5_GROUPS * S5_STATE
MLA_WIDTH = 512
MLA_HEADS = 8
MLA_NOPE = 64
MLA_ROPE = 32
MLA_V = 64
Q_LORA = 384
KV_LORA = 256
ROPE_THETA = 10000.0
HEAD_PAD = 128
PEER_HEADS = 8
PEER_NKEYS = 128
PEER_QDIM = 256
PEER_HALF = 128
PEER_TOPK = 16
PEER_EXPERTS = PEER_NKEYS * PEER_NKEYS
DEEPNORM_ALPHA = 2.0 ** 0.25
NORM_EPS = 1e-6
MASK_VALUE = -1e30
NEG_BIG = -3.0e38

LANES = 128
SUBLANES = 8
VMEM_LIMIT = 56 * 1024 * 1024


def _gelu(x):
    return 0.5 * x * (1.0 + jnp.tanh(0.7978845608028654 * (x + 0.044715 * (x * x * x))))


def _sigmoid(x):
    return 1.0 / (1.0 + jnp.exp(-x))


def _params(sem, flags=None):
    return pltpu.CompilerParams(dimension_semantics=sem, vmem_limit_bytes=VMEM_LIMIT, flags=flags)


def _ada_kernel(c_ref, w_ref, b_ref, o_ref):
    c = c_ref[...]
    ca = c * _sigmoid(c)
    o_ref[...] = jnp.dot(ca, w_ref[...], preferred_element_type=F32,
                         precision=lax.Precision.HIGHEST) + b_ref[...]


def _ada(c, w_ada, b_ada):
    bsz, dm = c.shape
    n = w_ada.shape[1]
    tn = 1024
    return pl.pallas_call(
        _ada_kernel,
        grid=(n // tn,),
        in_specs=[pl.BlockSpec((bsz, dm), lambda j: (0, 0)),
                  pl.BlockSpec((dm, tn), lambda j: (0, j)),
                  pl.BlockSpec((1, tn), lambda j: (0, j))],
        out_specs=pl.BlockSpec((bsz, tn), lambda j: (0, j)),
        out_shape=jax.ShapeDtypeStruct((bsz, n), F32),
        compiler_params=_params(("arbitrary",)),
        name="ada",
    )(c, w_ada, b_ada.reshape(1, n))


def _proj_kernel(x_ref, pos_ref, sc_ref, sh_ref, win_ref, wqa_ref, wqb_ref, wk_ref, wv_ref,
                 invf_ref, u_ref, q_ref, k_ref, v_ref):
    x = x_ref[0]
    h = (x * (1.0 + sc_ref[0]) + sh_ref[0]).astype(BF16)
    proj = jnp.dot(h, win_ref[...], preferred_element_type=F32)
    o = S5_WIDTH
    u_ref[0] = proj[:, :o]
    cq = proj[:, o:o + Q_LORA]
    o += Q_LORA
    ckv = proj[:, o:o + KV_LORA]
    o += KV_LORA
    kra = proj[:, o:o + HEAD_PAD]
    krb = proj[:, o + HEAD_PAD:o + 2 * HEAD_PAD]
    ang = invf_ref[...] * pos_ref[0]
    cos_t = jnp.cos(ang)
    sin_t = jnp.sin(ang)
    tt = ang.shape[1]
    cs = jnp.concatenate([jnp.ones((MLA_NOPE, tt), F32), cos_t, cos_t,
                          jnp.zeros((HEAD_PAD - MLA_NOPE - MLA_ROPE, tt), F32)], axis=0).T
    sn = jnp.concatenate([jnp.zeros((MLA_NOPE, tt), F32), -sin_t, sin_t,
                          jnp.zeros((HEAD_PAD - MLA_NOPE - MLA_ROPE, tt), F32)], axis=0).T
    cqn = (cq * lax.rsqrt(jnp.mean(cq * cq, axis=-1, keepdims=True) + NORM_EPS)).astype(BF16)
    qa = jnp.dot(cqn, wqa_ref[...], preferred_element_type=F32)
    qb = jnp.dot(cqn, wqb_ref[...], preferred_element_type=F32)
    ckvn = (ckv * lax.rsqrt(jnp.mean(ckv * ckv, axis=-1, keepdims=True) + NORM_EPS)).astype(BF16)
    kn = jnp.dot(ckvn, wk_ref[...], preferred_element_type=F32)
    kp = kra * cs + krb * sn
    for hh in range(MLA_HEADS):
        sl = slice(hh * HEAD_PAD, (hh + 1) * HEAD_PAD)
        q_ref[0, :, sl] = (qa[:, sl] * cs + qb[:, sl] * sn).astype(BF16)
        k_ref[0, :, sl] = (kn[:, sl] + kp).astype(BF16)
    v_ref[0] = lax.dot_general(wv_ref[...], ckvn, (((1,), (1,)), ((), ())),
                               preferred_element_type=F32).astype(BF16)


def _proj(x, posf, sc1, sh1, win, wqa, wqb, wk, wv, invf, tt):
    bsz, L, dm = x.shape
    hp = MLA_HEADS * HEAD_PAD
    full = lambda a: pl.BlockSpec(a.shape, lambda b, i: (0,) * a.ndim)
    return pl.pallas_call(
        _proj_kernel,
        grid=(bsz, L // tt),
        in_specs=[pl.BlockSpec((1, tt, dm), lambda b, i: (b, i, 0)),
                  pl.BlockSpec((1, 1, tt), lambda b, i: (b, 0, i)),
                  pl.BlockSpec((1, 1, dm), lambda b, i: (b, 0, 0)),
                  pl.BlockSpec((1, 1, dm), lambda b, i: (b, 0, 0)),
                  full(win), full(wqa), full(wqb), full(wk), full(wv), full(invf)],
        out_specs=[pl.BlockSpec((1, tt, S5_WIDTH), lambda b, i: (b, i, 0)),
                   pl.BlockSpec((1, tt, hp), lambda b, i: (b, i, 0)),
                   pl.BlockSpec((1, tt, hp), lambda b, i: (b, i, 0)),
                   pl.BlockSpec((1, MLA_WIDTH, tt), lambda b, i: (b, 0, i))],
        out_shape=[jax.ShapeDtypeStruct((bsz, L, S5_WIDTH), F32),
                   jax.ShapeDtypeStruct((bsz, L, hp), BF16),
                   jax.ShapeDtypeStruct((bsz, L, hp), BF16),
                   jax.ShapeDtypeStruct((bsz, MLA_WIDTH, L), BF16)],
        compiler_params=_params(("parallel", "parallel")),
        name="proj",
    )(x, posf, sc1, sh1, win, wqa, wqb, wk, wv, invf)


def _s5_kernel(u_ref, wb_ref, wc_ref, are_ref, aim_ref, d_ref, wglu_ref, bglu_ref, gn_ref,
               y_ref, utm, st, yt, carry, *, tc, bsz):
    @pl.when(pl.program_id(0) == 0)
    def _():
        carry[...] = jnp.zeros_like(carry)

    nslab = S5_WIDTH // LANES
    for b in range(bsz):
        for j in range(nslab):
            utm[j, pl.ds(b, tc, stride=bsz), :] = u_ref[b, :, j * LANES:(j + 1) * LANES]
    scols = S5_COLS // nslab
    us = [utm[j] for j in range(nslab)]
    for j in range(nslab):
        bu = jnp.dot(us[j].astype(BF16), wb_ref[j], preferred_element_type=F32)
        st[:, j * scols:(j + 1) * scols] = bu[:, :scols]
        st[:, S5_COLS + j * scols:S5_COLS + (j + 1) * scols] = bu[:, scols:]

    cb = 512
    for c0 in range(0, S5_COLS, cb):
        cre = slice(c0, c0 + cb)
        cim = slice(S5_COLS + c0, S5_COLS + c0 + cb)
        are = are_ref[:, cre]
        aim = aim_ref[:, cre]

        def step(t, sc, cre=cre, cim=cim, are=are, aim=aim):
            sre, sim = sc
            r = pl.multiple_of(t * bsz, bsz)
            nre = are * sre - aim * sim + st[pl.ds(r, bsz), cre]
            nim = are * sim + aim * sre + st[pl.ds(r, bsz), cim]
            st[pl.ds(r, bsz), cre] = nre
            st[pl.ds(r, bsz), cim] = nim
            return nre, nim

        sre, sim = lax.fori_loop(0, tc, step, (carry[:, cre], carry[:, cim]), unroll=4)
        carry[:, cre] = sre
        carry[:, cim] = sim

    ys = []
    for j in range(nslab):
        s_re = st[:, j * scols:(j + 1) * scols].astype(BF16)
        s_im = st[:, S5_COLS + j * scols:S5_COLS + (j + 1) * scols].astype(BF16)
        ys.append(jnp.dot(s_re, wc_ref[j, :scols, :], preferred_element_type=F32)
                  + jnp.dot(s_im, wc_ref[j, scols:, :], preferred_element_type=F32)
                  + d_ref[:, j * LANES:(j + 1) * LANES] * us[j])
    y = jnp.concatenate(ys, axis=-1)
    y = _gelu(y)
    z = jnp.dot(y.astype(BF16), wglu_ref[...], preferred_element_type=F32) + bglu_ref[...]
    y = y * _sigmoid(z)
    y = y * lax.rsqrt(jnp.mean(y * y, axis=-1, keepdims=True) + NORM_EPS) * gn_ref[...]
    for j in range(nslab):
        yt[j] = y[:, j * LANES:(j + 1) * LANES]
    for b in range(bsz):
        for j in range(nslab):
            y_ref[b, :, j * LANES:(j + 1) * LANES] = yt[j, pl.ds(b, tc, stride=bsz), :].astype(BF16)


def _s5(u, wb, wc, are, aim, d, wglu, bglu, gn, tc):
    bsz, L, w = u.shape
    full = lambda a: pl.BlockSpec(a.shape, lambda i: (0,) * a.ndim)
    return pl.pallas_call(
        functools.partial(_s5_kernel, tc=tc, bsz=bsz),
        grid=(L // tc,),
        in_specs=[pl.BlockSpec((bsz, tc, w), lambda i: (0, i, 0)),
                  full(wb), full(wc), full(are), full(aim), full(d), full(wglu), full(bglu), full(gn)],
        out_specs=pl.BlockSpec((bsz, tc, w), lambda i: (0, i, 0)),
        out_shape=jax.ShapeDtypeStruct((bsz, L, w), BF16),
        scratch_shapes=[pltpu.VMEM((w // LANES, tc * bsz, LANES), F32),
                        pltpu.VMEM((tc * bsz, 2 * S5_COLS), F32),
                        pltpu.VMEM((w // LANES, tc * bsz, LANES), F32),
                        pltpu.VMEM((bsz, 2 * S5_COLS), F32)],
        compiler_params=_params(("arbitrary",)),
        name="s5",
    )(u, wb, wc, are, aim, d, wglu, bglu, gn)


def _attn_kernel(q_ref, k_ref, vt_ref, o_ref, sa, sb, mxa, mxb, m_s, l_s, acc_s, *, tq):
    qi = pl.program_id(2)
    nt = (((1,), (1,)), ((), ()))
    qs = [q_ref[0, :, hh * HEAD_PAD:(hh + 1) * HEAD_PAD] for hh in range(2)]

    m_s[...] = jnp.full(m_s.shape, MASK_VALUE, F32)
    l_s[...] = jnp.zeros_like(l_s)
    acc_s[...] = jnp.zeros_like(acc_s)

    def scores_into(kb, sbuf, mxbuf):
        ks = pl.multiple_of(kb * tq, tq)
        for hh in range(2):
            st = lax.dot_general(k_ref[0, pl.ds(ks, tq), hh * HEAD_PAD:(hh + 1) * HEAD_PAD], qs[hh], nt,
                                 preferred_element_type=F32)
            sbuf[hh] = st
            mxbuf[hh] = jnp.max(st, axis=0, keepdims=True)

    def softmax_from(kb, sbuf, mxbuf, masked):
        ks = pl.multiple_of(kb * tq, tq)
        for hh in range(2):
            st = sbuf[hh]
            if masked:
                kc = lax.broadcasted_iota(jnp.int32, (tq, tq), 0) // CHUNK
                qc = lax.broadcasted_iota(jnp.int32, (tq, tq), 1) // CHUNK
                st = jnp.where(kc <= qc, st, MASK_VALUE)
                mx = jnp.max(st, axis=0, keepdims=True)
            else:
                mx = mxbuf[hh]
            vt = vt_ref[0, hh * MLA_V:(hh + 1) * MLA_V, pl.ds(ks, tq)]
            m = m_s[hh]
            m_new = jnp.maximum(m, mx)
            alpha = jnp.exp2(m - m_new)
            pt = jnp.exp2(st - m_new)
            l_s[hh] = alpha * l_s[hh] + jnp.sum(pt, axis=0, keepdims=True)
            acc_s[hh] = alpha * acc_s[hh] + jnp.dot(vt, pt.astype(BF16), preferred_element_type=F32)
            m_s[hh] = m_new

    def finish():
        o_ref[0] = jnp.concatenate([acc_s[0] / l_s[0], acc_s[1] / l_s[1]], axis=0).T.astype(BF16)

    scores_into(0, sa, mxa)

    def pair(j, carry):
        scores_into(2 * j + 1, sb, mxb)
        softmax_from(2 * j, sa, mxa, False)
        scores_into(2 * j + 2, sa, mxa)
        softmax_from(2 * j + 1, sb, mxb, False)
        return carry

    lax.fori_loop(0, qi // 2, pair, 0)

    @pl.when(qi % 2 == 0)
    def _():
        softmax_from(qi, sa, mxa, True)
        finish()

    @pl.when(qi % 2 == 1)
    def _():
        scores_into(qi, sb, mxb)
        softmax_from(qi - 1, sa, mxa, False)
        softmax_from(qi, sb, mxb, True)
        finish()


def _attn(q, k, vt, tq):
    bsz, L, hp = q.shape
    return pl.pallas_call(
        functools.partial(_attn_kernel, tq=tq),
        grid=(bsz, MLA_HEADS // 2, L // tq),
        in_specs=[pl.BlockSpec((1, tq, 2 * HEAD_PAD), lambda b, h, i: (b, i, h)),
                  pl.BlockSpec((1, L, 2 * HEAD_PAD), lambda b, h, i: (b, 0, h)),
                  pl.BlockSpec((1, 2 * MLA_V, L), lambda b, h, i: (b, h, 0))],
        out_specs=pl.BlockSpec((1, tq, 2 * MLA_V), lambda b, h, i: (b, i, h)),
        out_shape=jax.ShapeDtypeStruct((bsz, L, MLA_WIDTH), BF16),
        scratch_shapes=[pltpu.VMEM((2, tq, tq), F32), pltpu.VMEM((2, tq, tq), F32),
                        pltpu.VMEM((2, 1, tq), F32), pltpu.VMEM((2, 1, tq), F32),
                        pltpu.VMEM((2, 1, tq), F32), pltpu.VMEM((2, 1, tq), F32),
                        pltpu.VMEM((2, MLA_V, tq), F32)],
        compiler_params=_params(("parallel", "parallel", "arbitrary")),
        name="attn",
    )(q, k, vt)


def _top16_rows(s, want_rank):
    work = s
    rank = jnp.full(s.shape, 127.0, F32) if want_rank else None
    ridx = lax.broadcasted_iota(jnp.int32, (PEER_TOPK, s.shape[1]), 0)
    vals = jnp.zeros((PEER_TOPK, s.shape[1]), F32)
    for r in range(PEER_TOPK):
        m = jnp.max(work, axis=0, keepdims=True)
        hit = work == m
        if want_rank:
            rank = jnp.where(hit, float(r), rank)
        work = jnp.where(hit, NEG_BIG, work)
        vals = jnp.where(ridx == r, m, vals)
    return vals, rank


def _mix_kernel(x_ref, ys_ref, ym_ref, g1_ref, sc2_ref, sh2_ref, gnm_ref, wo1_ref, wo2_ref,
                l1g_ref, l1b_ref, wq_ref, k1_ref, k2_ref,
                x1_ref, h2t_ref, cnt_ref, e1_ref, rk2_ref, e2_ref):
    x = x_ref[0]
    ym = ym_ref[0].astype(F32)
    ym = (ym * lax.rsqrt(jnp.mean(ym * ym, axis=-1, keepdims=True) + NORM_EPS) * gnm_ref[...]).astype(BF16)
    ymix = (jnp.dot(ys_ref[0], wo1_ref[...], preferred_element_type=F32)
            + jnp.dot(ym, wo2_ref[...], preferred_element_type=F32))
    r = DEEPNORM_ALPHA * x + (1.0 + g1_ref[0]) * ymix
    mu = jnp.mean(r, axis=-1, keepdims=True)
    rc = r - mu
    var = jnp.mean(rc * rc, axis=-1, keepdims=True)
    x1 = rc * lax.rsqrt(var + NORM_EPS) * l1g_ref[...] + l1b_ref[...]
    x1_ref[0] = x1
    h2 = x1 * (1.0 + sc2_ref[0]) + sh2_ref[0]
    h2t = h2.T.astype(BF16)
    h2t_ref[...] = h2t
    qt = jnp.dot(wq_ref[...], h2t, preferred_element_type=F32)
    for hh in range(PEER_HEADS):
        q1 = qt[hh * PEER_QDIM:hh * PEER_QDIM + PEER_HALF]
        q2 = qt[hh * PEER_QDIM + PEER_HALF:(hh + 1) * PEER_QDIM]
        s1 = jnp.dot(k1_ref[hh], q1.astype(BF16), preferred_element_type=F32)
        s2 = jnp.dot(k2_ref[hh], q2.astype(BF16), preferred_element_type=F32)
        v1, _ = _top16_rows(s1, False)
        v2, rk2 = _top16_rows(s2, True)
        cands = [v1[0:1] + v2]
        for a in range(1, 8):
            cands.append(v1[a:a + 1] + v2[0:8])
        cands.append(v1[8:16] + v2[0:1])
        top, _ = _top16_rows(jnp.concatenate(cands, axis=0), False)
        theta = top[PEER_TOPK - 1:PEER_TOPK]
        z = jnp.sum(jnp.exp(top - top[0:1]), axis=0, keepdims=True)
        dense_b = 4
        cnt = jnp.zeros_like(s1)
        for b in range(dense_b):
            cnt = cnt + jnp.where(s1 + v2[b:b + 1] >= theta, 1.0, 0.0)
        for a in range(3):
            extra = jnp.sum(jnp.where(v1[a:a + 1] + v2[dense_b:] >= theta, 1.0, 0.0), axis=0, keepdims=True)
            cnt = cnt + jnp.where(s1 == v1[a:a + 1], extra, 0.0)
        cnt_ref[hh] = jnp.where(s1 >= v1[PEER_TOPK - 1:PEER_TOPK], cnt, 0.0)
        e1_ref[hh] = jnp.exp(s1 - v1[0:1])
        rk2_ref[hh] = rk2.astype(BF16)
        e2_ref[hh] = (jnp.exp(s2 - v2[0:1]) / z).astype(BF16)


def _mix(x, ys, ym, g1, sc2, sh2, gnm, wo1, wo2, l1g, l1b, wqt, k1, k2, tt):
    bsz, L, dm = x.shape
    n = bsz * L
    nl = L // tt
    full = lambda a: pl.BlockSpec(a.shape, lambda b, i: (0,) * a.ndim)
    tok = lambda w: pl.BlockSpec((1, tt, w), lambda b, i: (b, i, 0))
    vec = pl.BlockSpec((1, 1, dm), lambda b, i: (b, 0, 0))
    tab = pl.BlockSpec((PEER_HEADS, PEER_NKEYS, tt), lambda b, i: (0, 0, b * nl + i))
    tab_shape = jax.ShapeDtypeStruct((PEER_HEADS, PEER_NKEYS, n), F32)
    tab_bf16 = jax.ShapeDtypeStruct((PEER_HEADS, PEER_NKEYS, n), BF16)
    return pl.pallas_call(
        _mix_kernel,
        grid=(bsz, nl),
        in_specs=[tok(dm), tok(S5_WIDTH), tok(MLA_WIDTH), vec, vec, vec,
                  full(gnm), full(wo1), full(wo2), full(l1g), full(l1b), full(wqt), full(k1), full(k2)],
        out_specs=[tok(dm), pl.BlockSpec((dm, tt), lambda b, i: (0, b * nl + i)), tab, tab, tab, tab],
        out_shape=[jax.ShapeDtypeStruct((bsz, L, dm), F32), jax.ShapeDtypeStruct((dm, n), BF16),
                   tab_shape, tab_shape, tab_bf16, tab_bf16],
        compiler_params=_params(("parallel", "parallel")),
        name="mix",
    )(x, ys, ym, g1, sc2, sh2, gnm, wo1, wo2, l1g, l1b, wqt, k1, k2)


def _peer_kernel(h2t_ref, cnt_ref, e1_ref, rk2_ref, e2_ref, u_ref, vt_ref, x1_ref, g2_ref, l2g_ref, l2b_ref,
                 o_ref, acc, wa0, wa1, *, ni, sub, nq):
    eb = pl.program_id(1)
    ne = pl.num_programs(1) - 1
    dm = acc.shape[0]

    @pl.when(eb == 0)
    def _():
        acc[...] = jnp.zeros_like(acc)
        wa1[...] = jnp.zeros_like(wa1)

    def consume(wa_r, q, nq):
        rows = slice(q * (dm // nq), (q + 1) * (dm // nq))
        acc[rows, :] += jnp.dot(vt_ref[0, rows, :], wa_r[...], preferred_element_type=F32)

    def step(wa_w, wa_r):
        sc = None
        for ii in range(ni):
            i = eb * ni + ii
            if ii % sub == 0:
                sc = jnp.dot(u_ref[ii * PEER_NKEYS:(ii + sub) * PEER_NKEYS, :], h2t_ref[...],
                             preferred_element_type=F32)
            if ii % (ni // nq) == 0:
                consume(wa_r, ii // (ni // nq), nq)
            rows = slice((ii % sub) * PEER_NKEYS, (ii % sub + 1) * PEER_NKEYS)
            w = None
            for hh in range(PEER_HEADS):
                c = cnt_ref[hh, pl.ds(i, 1), :].astype(BF16)
                e = e1_ref[hh, pl.ds(i, 1), :].astype(BF16)
                term = jnp.where(rk2_ref[hh] < c, e2_ref[hh], jnp.zeros((), BF16)) * e
                w = term if w is None else w + term
            wa_w[ii * PEER_NKEYS:(ii + 1) * PEER_NKEYS, :] = w * _gelu(sc[rows].astype(BF16))

    @pl.when((eb % 2 == 0) & (eb < ne))
    def _():
        step(wa0, wa1)

    @pl.when(eb % 2 == 1)
    def _():
        step(wa1, wa0)

    @pl.when(eb == ne)
    def _():
        consume(wa1, 0, 1)
        yff = acc[...].T
        r = DEEPNORM_ALPHA * x1_ref[0] + (1.0 + g2_ref[0]) * yff
        mu = jnp.mean(r, axis=-1, keepdims=True)
        rc = r - mu
        var = jnp.mean(rc * rc, axis=-1, keepdims=True)
        o_ref[0] = rc * lax.rsqrt(var + NORM_EPS) * l2g_ref[...] + l2b_ref[...]


def _peer(h2t, cnt, e1, rk2, e2, ub, vt, x1, g2, l2g, l2b, tt, ni):
    bsz, L, dm = x1.shape
    nl = L // tt
    eb = ni * PEER_NKEYS
    tab = pl.BlockSpec((PEER_HEADS, PEER_NKEYS, tt), lambda t, e: (0, 0, t))
    full = lambda a: pl.BlockSpec(a.shape, lambda t, e: (0,) * a.ndim)
    ne = PEER_NKEYS // ni
    assert ne % 2 == 0 and ni % 4 == 0
    return pl.pallas_call(
        functools.partial(_peer_kernel, ni=ni, sub=2, nq=4),
        grid=(bsz * nl, ne + 1),
        in_specs=[pl.BlockSpec((dm, tt), lambda t, e: (0, t)), tab, tab, tab, tab,
                  pl.BlockSpec((eb, dm), lambda t, e: (jnp.minimum(e, ne - 1), 0)),
                  pl.BlockSpec((1, dm, eb), lambda t, e: (jnp.maximum(e - 1, 0), 0, 0)),
                  pl.BlockSpec((1, tt, dm), lambda t, e: (t // nl, t % nl, 0)),
                  pl.BlockSpec((1, 1, dm), lambda t, e: (t // nl, 0, 0)),
                  full(l2g), full(l2b)],
        out_specs=pl.BlockSpec((1, tt, dm), lambda t, e: (t // nl, t % nl, 0)),
        out_shape=jax.ShapeDtypeStruct((bsz, L, dm), F32),
        scratch_shapes=[pltpu.VMEM((dm, tt), F32), pltpu.VMEM((eb, tt), BF16), pltpu.VMEM((eb, tt), BF16)],
        compiler_params=_params(("parallel", "arbitrary")),
        name="peer",
    )(h2t, cnt, e1, rk2, e2, ub, vt, x1, g2, l2g, l2b)


def _s5_discretize(lam_re, lam_im, log_dt, b_re, b_im, c_re, c_im):
    dt = jnp.exp(log_dt)[:, None]
    mag = jnp.exp(lam_re * dt)
    ab_re = mag * jnp.cos(lam_im * dt)
    ab_im = mag * jnp.sin(lam_im * dt)
    den = lam_re * lam_re + lam_im * lam_im
    nr = ab_re - 1.0
    coef_re = (nr * lam_re + ab_im * lam_im) / den
    coef_im = (ab_im * lam_re - nr * lam_im) / den
    bb_re = coef_re[..., None] * b_re - coef_im[..., None] * b_im
    bb_im = coef_re[..., None] * b_im + coef_im[..., None] * b_re
    nslab = S5_WIDTH // LANES
    gs = S5_GROUPS // nslab
    eye = jnp.eye(gs, dtype=F32)
    slab = lambda a: a.reshape((nslab, gs) + a.shape[1:])
    wb_re = jnp.einsum("sgph,gk->sghkp", slab(bb_re), eye).reshape(nslab, LANES, gs * S5_STATE)
    wb_im = jnp.einsum("sgph,gk->sghkp", slab(bb_im), eye).reshape(nslab, LANES, gs * S5_STATE)
    wb = jnp.concatenate([wb_re, wb_im], axis=2).astype(BF16)
    wc_re = jnp.einsum("sghp,gk->sgpkh", slab(c_re), eye).reshape(nslab, gs * S5_STATE, LANES)
    wc_im = jnp.einsum("sghp,gk->sgpkh", slab(c_im), eye).reshape(nslab, gs * S5_STATE, LANES)
    wc = jnp.concatenate([wc_re, -wc_im], axis=1).astype(BF16)
    return wb, wc, ab_re.reshape(1, S5_COLS), ab_im.reshape(1, S5_COLS)


def _pad_heads(w, width, offset):
    kdim = w.shape[0]
    w = w.reshape(kdim, MLA_HEADS, width)
    out = jnp.zeros((kdim, MLA_HEADS, HEAD_PAD), w.dtype)
    out = out.at[:, :, offset:offset + width].set(w)
    return out.reshape(kdim, MLA_HEADS * HEAD_PAD)


def _swap_halves(w):
    half = w.shape[-1] // 2
    return jnp.concatenate([w[..., half:], w[..., :half]], axis=-1)


def kernel(x, c, positions, w_ada, b_ada, w_in, s5_lambda_re, s5_lambda_im, s5_log_dt, s5_b_re, s5_b_im, s5_c_re, s5_c_im, s5_d, s5_w_glu, s5_b_glu, mla_q_norm, mla_w_uq, mla_kv_norm, mla_w_ukv, gn_s5, gn_mla, w_out, ln1_g, ln1_b, peer_w_query, peer_keys1, peer_keys2, peer_u, peer_v, ln2_g, ln2_b):
    bsz, L, dm = x.shape
    depth = w_ada.shape[0]
    t_proj = min(512, L)
    t_s5 = min(128, L)
    t_attn = min(512, L)
    t_mix = min(256, L)
    t_peer = min(512, L)
    peer_ni = 8

    invf = (ROPE_THETA ** (-jnp.arange(0, MLA_ROPE, 2, dtype=F32) / MLA_ROPE))[:, None]
    posf = positions.astype(F32)[:, None, :]

    for l in range(depth):
        mod = _ada(c, w_ada[l], b_ada[l])
        sh1, sc1, g1, sh2, sc2, g2 = [m[:, None, :] for m in jnp.split(mod, 6, axis=-1)]

        wi = w_in[l]
        o1, o2, o3 = S5_WIDTH, S5_WIDTH + Q_LORA, S5_WIDTH + Q_LORA + KV_LORA
        w_kr = wi[:, o3:]
        zpad = lambda w: jnp.zeros((dm, HEAD_PAD), F32).at[:, MLA_NOPE:MLA_NOPE + MLA_ROPE].set(w)
        win = jnp.concatenate([wi[:, :o3], zpad(w_kr), zpad(_swap_halves(w_kr))], axis=1).astype(BF16)
        scale = (MLA_NOPE + MLA_ROPE) ** -0.5 * math.log2(math.e)
        wuq = (mla_w_uq[l] * mla_q_norm[l][:, None] * scale).reshape(Q_LORA, MLA_HEADS, MLA_NOPE + MLA_ROPE)
        wuq_main = wuq.reshape(Q_LORA, MLA_HEADS * (MLA_NOPE + MLA_ROPE))
        wqa = _pad_heads(wuq_main, MLA_NOPE + MLA_ROPE, 0).astype(BF16)
        wuq_sw = _swap_halves(wuq[:, :, MLA_NOPE:]).reshape(Q_LORA, MLA_HEADS * MLA_ROPE)
        wqb = _pad_heads(wuq_sw, MLA_ROPE, MLA_NOPE).astype(BF16)
        wukv = (mla_w_ukv[l] * mla_kv_norm[l][:, None]).reshape(KV_LORA, MLA_HEADS, MLA_NOPE + MLA_V)
        wk = _pad_heads(wukv[:, :, :MLA_NOPE].reshape(KV_LORA, MLA_HEADS * MLA_NOPE), MLA_NOPE, 0).astype(BF16)
        wv = wukv[:, :, MLA_NOPE:].reshape(KV_LORA, MLA_HEADS * MLA_V).T.astype(BF16)

        u, q, k, v = _proj(x, posf, sc1, sh1, win, wqa, wqb, wk, wv, invf, t_proj)

        wb, wc, are, aim = _s5_discretize(s5_lambda_re[l], s5_lambda_im[l], s5_log_dt[l], s5_b_re[l],
                                          s5_b_im[l], s5_c_re[l], s5_c_im[l])
        are = jnp.broadcast_to(are, (bsz, S5_COLS))
        aim = jnp.broadcast_to(aim, (bsz, S5_COLS))
        y_s5 = _s5(u, wb, wc, are, aim, s5_d[l][None], s5_w_glu[l].astype(BF16), s5_b_glu[l][None],
                   gn_s5[l][None], t_s5)

        y_mla = _attn(q, k, v, t_attn)

        wo = w_out[l].astype(BF16)
        wqt = peer_w_query[l].T.astype(BF16)
        x1, h2t, cnt, e1, rk2, e2 = _mix(x, y_s5, y_mla, g1, sc2, sh2, gn_mla[l][None], wo[:S5_WIDTH],
                                         wo[S5_WIDTH:], ln1_g[l][None], ln1_b[l][None], wqt,
                                         peer_keys1[l].astype(BF16), peer_keys2[l].astype(BF16), t_mix)

        vt = peer_v[l].astype(BF16).reshape(PEER_NKEYS // peer_ni, peer_ni * PEER_NKEYS, dm).transpose(0, 2, 1)
        x = _peer(h2t, cnt, e1, rk2, e2, peer_u[l].astype(BF16), vt, x1, g2,
                  ln2_g[l][None], ln2_b[l][None], t_peer, peer_ni)
    return x
```

```python
import functools
import math

import jax
import jax.numpy as jnp
from jax import lax
from jax.experimental import pallas as pl
from jax.experimental.pallas import tpu as pltpu

F32 = jnp.float32
BF16 = jnp.bfloat16

D_MODEL = 1024
CHUNK = 64
S5_WIDTH = 512
S5_GROUP = 16
S5_GROUPS = 32
S5_STATE = 64
S5_COLS = S5_GROUPS * S5_STATE
MLA_WIDTH = 512
MLA_HEADS = 8
MLA_NOPE = 64
MLA_ROPE = 32
MLA_V = 64
Q_LORA = 384
KV_LORA = 256
ROPE_THETA = 10000.0
HEAD_PAD = 128
PEER_HEADS = 8
PEER_NKEYS = 128
PEER_QDIM = 256
PEER_HALF = 128
PEER_TOPK = 16
PEER_EXPERTS = PEER_NKEYS * PEER_NKEYS
DEEPNORM_ALPHA = 2.0 ** 0.25
NORM_EPS = 1e-6
MASK_VALUE = -1e30
NEG_BIG = -3.0e38

LANES = 128
SUBLANES = 8
VMEM_LIMIT = 56 * 1024 * 1024


def _gelu(x):
    return 0.5 * x * (1.0 + jnp.tanh(0.7978845608028654 * (x + 0.044715 * (x * x * x))))


def _sigmoid(x):
    return 1.0 / (1.0 + jnp.exp(-x))


def _params(sem, flags=None):
    return pltpu.CompilerParams(dimension_semantics=sem, vmem_limit_bytes=VMEM_LIMIT, flags=flags)


def _ada_kernel(c_ref, w_ref, b_ref, o_ref):
    c = c_ref[...]
    ca = c * _sigmoid(c)
    o_ref[...] = jnp.dot(ca, w_ref[...], preferred_element_type=F32,
                         precision=lax.Precision.HIGHEST) + b_ref[...]


def _ada(c, w_ada, b_ada):
    bsz, dm = c.shape
    n = w_ada.shape[1]
    tn = 1024
    return pl.pallas_call(
        _ada_kernel,
        grid=(n // tn,),
        in_specs=[pl.BlockSpec((bsz, dm), lambda j: (0, 0)),
                  pl.BlockSpec((dm, tn), lambda j: (0, j)),
                  pl.BlockSpec((1, tn), lambda j: (0, j))],
        out_specs=pl.BlockSpec((bsz, tn), lambda j: (0, j)),
        out_shape=jax.ShapeDtypeStruct((bsz, n), F32),
        compiler_params=_params(("arbitrary",)),
        name="ada",
    )(c, w_ada, b_ada.reshape(1, n))


def _proj_kernel(x_ref, pos_ref, sc_ref, sh_ref, win_ref, wqa_ref, wqb_ref, wk_ref, wv_ref,
                 invf_ref, u_ref, q_ref, k_ref, v_ref):
    x = x_ref[0]
    h = (x * (1.0 + sc_ref[0]) + sh_ref[0]).astype(BF16)
    proj = jnp.dot(h, win_ref[...], preferred_element_type=F32)
    o = S5_WIDTH
    u_ref[0] = proj[:, :o]
    cq = proj[:, o:o + Q_LORA]
    o += Q_LORA
    ckv = proj[:, o:o + KV_LORA]
    o += KV_LORA
    kra = proj[:, o:o + HEAD_PAD]
    krb = proj[:, o + HEAD_PAD:o + 2 * HEAD_PAD]
    ang = invf_ref[...] * pos_ref[0]
    cos_t = jnp.cos(ang)
    sin_t = jnp.sin(ang)
    tt = ang.shape[1]
    cs = jnp.concatenate([jnp.ones((MLA_NOPE, tt), F32), cos_t, cos_t,
                          jnp.zeros((HEAD_PAD - MLA_NOPE - MLA_ROPE, tt), F32)], axis=0).T
    sn = jnp.concatenate([jnp.zeros((MLA_NOPE, tt), F32), -sin_t, sin_t,
                          jnp.zeros((HEAD_PAD - MLA_NOPE - MLA_ROPE, tt), F32)], axis=0).T
    cqn = (cq * lax.rsqrt(jnp.mean(cq * cq, axis=-1, keepdims=True) + NORM_EPS)).astype(BF16)
    qa = jnp.dot(cqn, wqa_ref[...], preferred_element_type=F32)
    qb = jnp.dot(cqn, wqb_ref[...], preferred_element_type=F32)
    ckvn = (ckv * lax.rsqrt(jnp.mean(ckv * ckv, axis=-1, keepdims=True) + NORM_EPS)).astype(BF16)
    kn = jnp.dot(ckvn, wk_ref[...], preferred_element_type=F32)
    kp = kra * cs + krb * sn
    for hh in range(MLA_HEADS):
        sl = slice(hh * HEAD_PAD, (hh + 1) * HEAD_PAD)
        q_ref[0, :, sl] = (qa[:, sl] * cs + qb[:, sl] * sn).astype(BF16)
        k_ref[0, :, sl] = (kn[:, sl] + kp).astype(BF16)
    v_ref[0] = lax.dot_general(wv_ref[...], ckvn, (((1,), (1,)), ((), ())),
                               preferred_element_type=F32).astype(BF16)


def _proj(x, posf, sc1, sh1, win, wqa, wqb, wk, wv, invf, tt):
    bsz, L, dm = x.shape
    hp = MLA_HEADS * HEAD_PAD
    full = lambda a: pl.BlockSpec(a.shape, lambda b, i: (0,) * a.ndim)
    return pl.pallas_call(
        _proj_kernel,
        grid=(bsz, L // tt),
        in_specs=[pl.BlockSpec((1, tt, dm), lambda b, i: (b, i, 0)),
                  pl.BlockSpec((1, 1, tt), lambda b, i: (b, 0, i)),
                  pl.BlockSpec((1, 1, dm), lambda b, i: (b, 0, 0)),
                  pl.BlockSpec((1, 1, dm), lambda b, i: (b, 0, 0)),
                  full(win), full(wqa), full(wqb), full(wk), full(wv), full(invf)],
        out_specs=[pl.BlockSpec((1, tt, S5_WIDTH), lambda b, i: (b, i, 0)),
                   pl.BlockSpec((1, tt, hp), lambda b, i: (b, i, 0)),
                   pl.BlockSpec((1, tt, hp), lambda b, i: (b, i, 0)),
                   pl.BlockSpec((1, MLA_WIDTH, tt), lambda b, i: (b, 0, i))],
        out_shape=[jax.ShapeDtypeStruct((bsz, L, S5_WIDTH), F32),
                   jax.ShapeDtypeStruct((bsz, L, hp), BF16),
                   jax.ShapeDtypeStruct((bsz, L, hp), BF16),
                   jax.ShapeDtypeStruct((bsz, MLA_WIDTH, L), BF16)],
        compiler_params=_params(("parallel", "parallel")),
        name="proj",
    )(x, posf, sc1, sh1, win, wqa, wqb, wk, wv, invf)


def _s5_kernel(u_ref, wb_ref, wc_ref, are_ref, aim_ref, d_ref, wglu_ref, bglu_ref, gn_ref,
               y_ref, utm, st, yt, carry, *, tc, bsz):
    @pl.when(pl.program_id(0) == 0)
    def _():
        carry[...] = jnp.zeros_like(carry)

    nslab = S5_WIDTH // LANES
    for b in range(bsz):
        for j in range(nslab):
            utm[j, pl.ds(b, tc, stride=bsz), :] = u_ref[b, :, j * LANES:(j + 1) * LANES]
    scols = S5_COLS // nslab
    us = [utm[j] for j in range(nslab)]
    for j in range(nslab):
        bu = jnp.dot(us[j].astype(BF16), wb_ref[j], preferred_element_type=F32)
        st[:, j * scols:(j + 1) * scols] = bu[:, :scols]
        st[:, S5_COLS + j * scols:S5_COLS + (j + 1) * scols] = bu[:, scols:]

    cb = 512
    for c0 in range(0, S5_COLS, cb):
        cre = slice(c0, c0 + cb)
        cim = slice(S5_COLS + c0, S5_COLS + c0 + cb)
        are = are_ref[:, cre]
        aim = aim_ref[:, cre]

        def step(t, sc, cre=cre, cim=cim, are=are, aim=aim):
            sre, sim = sc
            r = pl.multiple_of(t * bsz, bsz)
            nre = are * sre - aim * sim + st[pl.ds(r, bsz), cre]
            nim = are * sim + aim * sre + st[pl.ds(r, bsz), cim]
            st[pl.ds(r, bsz), cre] = nre
            st[pl.ds(r, bsz), cim] = nim
            return nre, nim

        sre, sim = lax.fori_loop(0, tc, step, (carry[:, cre], carry[:, cim]), unroll=4)
        carry[:, cre] = sre
        carry[:, cim] = sim

    ys = []
    for j in range(nslab):
        s_re = st[:, j * scols:(j + 1) * scols].astype(BF16)
        s_im = st[:, S5_COLS + j * scols:S5_COLS + (j + 1) * scols].astype(BF16)
        ys.append(jnp.dot(s_re, wc_ref[j, :scols, :], preferred_element_type=F32)
                  + jnp.dot(s_im, wc_ref[j, scols:, :], preferred_element_type=F32)
                  + d_ref[:, j * LANES:(j + 1) * LANES] * us[j])
    y = jnp.concatenate(ys, axis=-1)
    y = _gelu(y)
    z = jnp.dot(y.astype(BF16), wglu_ref[...], preferred_element_type=F32) + bglu_ref[...]
    y = y * _sigmoid(z)
    y = y * lax.rsqrt(jnp.mean(y * y, axis=-1, keepdims=True) + NORM_EPS) * gn_ref[...]
    for j in range(nslab):
        yt[j] = y[:, j * LANES:(j + 1) * LANES]
    for b in range(bsz):
        for j in range(nslab):
            y_ref[b, :, j * LANES:(j + 1) * LANES] = yt[j, pl.ds(b, tc, stride=bsz), :].astype(BF16)


def _s5(u, wb, wc, are, aim, d, wglu, bglu, gn, tc):
    bsz, L, w = u.shape
    full = lambda a: pl.BlockSpec(a.shape, lambda i: (0,) * a.ndim)
    return pl.pallas_call(
        functools.partial(_s5_kernel, tc=tc, bsz=bsz),
        grid=(L // tc,),
        in_specs=[pl.BlockSpec((bsz, tc, w), lambda i: (0, i, 0)),
                  full(wb), full(wc), full(are), full(aim), full(d), full(wglu), full(bglu), full(gn)],
        out_specs=pl.BlockSpec((bsz, tc, w), lambda i: (0, i, 0)),
        out_shape=jax.ShapeDtypeStruct((bsz, L, w), BF16),
        scratch_shapes=[pltpu.VMEM((w // LANES, tc * bsz, LANES), F32),
                        pltpu.VMEM((tc * bsz, 2 * S5_COLS), F32),
                        pltpu.VMEM((w // LANES, tc * bsz, LANES), F32),
                        pltpu.VMEM((bsz, 2 * S5_COLS), F32)],
        compiler_params=_params(("arbitrary",)),
        name="s5",
    )(u, wb, wc, are, aim, d, wglu, bglu, gn)


def _attn_kernel(q_ref, k_ref, vt_ref, o_ref, sa, sb, mxa, mxb, m_s, l_s, acc_s, *, tq):
    qi = pl.program_id(2)
    nt = (((1,), (1,)), ((), ()))
    qs = [q_ref[0, :, hh * HEAD_PAD:(hh + 1) * HEAD_PAD] for hh in range(2)]

    m_s[...] = jnp.full(m_s.shape, MASK_VALUE, F32)
    l_s[...] = jnp.zeros_like(l_s)
    acc_s[...] = jnp.zeros_like(acc_s)

    def scores_into(kb, sbuf, mxbuf):
        ks = pl.multiple_of(kb * tq, tq)
        for hh in range(2):
            st = lax.dot_general(k_ref[0, pl.ds(ks, tq), hh * HEAD_PAD:(hh + 1) * HEAD_PAD], qs[hh], nt,
                                 preferred_element_type=F32)
            sbuf[hh] = st
            mxbuf[hh] = jnp.max(st, axis=0, keepdims=True)

    def softmax_from(kb, sbuf, mxbuf, masked):
        ks = pl.multiple_of(kb * tq, tq)
        for hh in range(2):
            st = sbuf[hh]
            if masked:
                kc = lax.broadcasted_iota(jnp.int32, (tq, tq), 0) // CHUNK
                qc = lax.broadcasted_iota(jnp.int32, (tq, tq), 1) // CHUNK
                st = jnp.where(kc <= qc, st, MASK_VALUE)
                mx = jnp.max(st, axis=0, keepdims=True)
            else:
                mx = mxbuf[hh]
            vt = vt_ref[0, hh * MLA_V:(hh + 1) * MLA_V, pl.ds(ks, tq)]
            m = m_s[hh]
            m_new = jnp.maximum(m, mx)
            alpha = jnp.exp2(m - m_new)
            pt = jnp.exp2(st - m_new)
            l_s[hh] = alpha * l_s[hh] + jnp.sum(pt, axis=0, keepdims=True)
            acc_s[hh] = alpha * acc_s[hh] + jnp.dot(vt, pt.astype(BF16), preferred_element_type=F32)
            m_s[hh] = m_new

    def finish():
        o_ref[0] = jnp.concatenate([acc_s[0] / l_s[0], acc_s[1] / l_s[1]], axis=0).T.astype(BF16)

    scores_into(0, sa, mxa)

    def pair(j, carry):
        scores_into(2 * j + 1, sb, mxb)
        softmax_from(2 * j, sa, mxa, False)
        scores_into(2 * j + 2, sa, mxa)
        softmax_from(2 * j + 1, sb, mxb, False)
        return carry

    lax.fori_loop(0, qi // 2, pair, 0)

    @pl.when(qi % 2 == 0)
    def _():
        softmax_from(qi, sa, mxa, True)
        finish()

    @pl.when(qi % 2 == 1)
    def _():
        scores_into(qi, sb, mxb)
        softmax_from(qi - 1, sa, mxa, False)
        softmax_from(qi, sb, mxb, True)
        finish()


def _attn(q, k, vt, tq):
    bsz, L, hp = q.shape
    return pl.pallas_call(
        functools.partial(_attn_kernel, tq=tq),
        grid=(bsz, MLA_HEADS // 2, L // tq),
        in_specs=[pl.BlockSpec((1, tq, 2 * HEAD_PAD), lambda b, h, i: (b, i, h)),
                  pl.BlockSpec((1, L, 2 * HEAD_PAD), lambda b, h, i: (b, 0, h)),
                  pl.BlockSpec((1, 2 * MLA_V, L), lambda b, h, i: (b, h, 0))],
        out_specs=pl.BlockSpec((1, tq, 2 * MLA_V), lambda b, h, i: (b, i, h)),
        out_shape=jax.ShapeDtypeStruct((bsz, L, MLA_WIDTH), BF16),
        scratch_shapes=[pltpu.VMEM((2, tq, tq), F32), pltpu.VMEM((2, tq, tq), F32),
                        pltpu.VMEM((2, 1, tq), F32), pltpu.VMEM((2, 1, tq), F32),
                        pltpu.VMEM((2, 1, tq), F32), pltpu.VMEM((2, 1, tq), F32),
                        pltpu.VMEM((2, MLA_V, tq), F32)],
        compiler_params=_params(("parallel", "parallel", "arbitrary")),
        name="attn",
    )(q, k, vt)


def _top16_rows(s, want_rank):
    work = s
    rank = jnp.full(s.shape, 127.0, F32) if want_rank else None
    ridx = lax.broadcasted_iota(jnp.int32, (PEER_TOPK, s.shape[1]), 0)
    vals = jnp.zeros((PEER_TOPK, s.shape[1]), F32)
    for r in range(PEER_TOPK):
        m = jnp.max(work, axis=0, keepdims=True)
        hit = work == m
        if want_rank:
            rank = jnp.where(hit, float(r), rank)
        work = jnp.where(hit, NEG_BIG, work)
        vals = jnp.where(ridx == r, m, vals)
    return vals, rank


def _mix_kernel(x_ref, ys_ref, ym_ref, g1_ref, sc2_ref, sh2_ref, gnm_ref, wo1_ref, wo2_ref,
                l1g_ref, l1b_ref, wq_ref, k1_ref, k2_ref,
                x1_ref, h2t_ref, cnt_ref, e1_ref, rk2_ref, e2_ref):
    x = x_ref[0]
    ym = ym_ref[0].astype(F32)
    ym = (ym * lax.rsqrt(jnp.mean(ym * ym, axis=-1, keepdims=True) + NORM_EPS) * gnm_ref[...]).astype(BF16)
    ymix = (jnp.dot(ys_ref[0], wo1_ref[...], preferred_element_type=F32)
            + jnp.dot(ym, wo2_ref[...], preferred_element_type=F32))
    r = DEEPNORM_ALPHA * x + (1.0 + g1_ref[0]) * ymix
    mu = jnp.mean(r, axis=-1, keepdims=True)
    rc = r - mu
    var = jnp.mean(rc * rc, axis=-1, keepdims=True)
    x1 = rc * lax.rsqrt(var + NORM_EPS) * l1g_ref[...] + l1b_ref[...]
    x1_ref[0] = x1
    h2 = x1 * (1.0 + sc2_ref[0]) + sh2_ref[0]
    h2t = h2.T.astype(BF16)
    h2t_ref[...] = h2t
    qt = jnp.dot(wq_ref[...], h2t, preferred_element_type=F32)
    for hh in range(PEER_HEADS):
        q1 = qt[hh * PEER_QDIM:hh * PEER_QDIM + PEER_HALF]
        q2 = qt[hh * PEER_QDIM + PEER_HALF:(hh + 1) * PEER_QDIM]
        s1 = jnp.dot(k1_ref[hh], q1.astype(BF16), preferred_element_type=F32)
        s2 = jnp.dot(k2_ref[hh], q2.astype(BF16), preferred_element_type=F32)
        v1, _ = _top16_rows(s1, False)
        v2, rk2 = _top16_rows(s2, True)
        cands = [v1[0:1] + v2]
        for a in range(1, 8):
            cands.append(v1[a:a + 1] + v2[0:8])
        cands.append(v1[8:16] + v2[0:1])
        top, _ = _top16_rows(jnp.concatenate(cands, axis=0), False)
        theta = top[PEER_TOPK - 1:PEER_TOPK]
        z = jnp.sum(jnp.exp(top - top[0:1]), axis=0, keepdims=True)
        dense_b = 4
        cnt = jnp.zeros_like(s1)
        for b in range(dense_b):
            cnt = cnt + jnp.where(s1 + v2[b:b + 1] >= theta, 1.0, 0.0)
        for a in range(3):
            extra = jnp.sum(jnp.where(v1[a:a + 1] + v2[dense_b:] >= theta, 1.0, 0.0), axis=0, keepdims=True)
            cnt = cnt + jnp.where(s1 == v1[a:a + 1], extra, 0.0)
        cnt_ref[hh] = jnp.where(s1 >= v1[PEER_TOPK - 1:PEER_TOPK], cnt, 0.0)
        e1_ref[hh] = jnp.exp(s1 - v1[0:1])
        rk2_ref[hh] = rk2.astype(BF16)
        e2_ref[hh] = (jnp.exp(s2 - v2[0:1]) / z).astype(BF16)


def _mix(x, ys, ym, g1, sc2, sh2, gnm, wo1, wo2, l1g, l1b, wqt, k1, k2, tt):
    bsz, L, dm = x.shape
    n = bsz * L
    nl = L // tt
    full = lambda a: pl.BlockSpec(a.shape, lambda b, i: (0,) * a.ndim)
    tok = lambda w: pl.BlockSpec((1, tt, w), lambda b, i: (b, i, 0))
    vec = pl.BlockSpec((1, 1, dm), lambda b, i: (b, 0, 0))
    tab = pl.BlockSpec((PEER_HEADS, PEER_NKEYS, tt), lambda b, i: (0, 0, b * nl + i))
    tab_shape = jax.ShapeDtypeStruct((PEER_HEADS, PEER_NKEYS, n), F32)
    tab_bf16 = jax.ShapeDtypeStruct((PEER_HEADS, PEER_NKEYS, n), BF16)
    return pl.pallas_call(
        _mix_kernel,
        grid=(bsz, nl),
        in_specs=[tok(dm), tok(S5_WIDTH), tok(MLA_WIDTH), vec, vec, vec,
                  full(gnm), full(wo1), full(wo2), full(l1g), full(l1b), full(wqt), full(k1), full(k2)],
        out_specs=[tok(dm), pl.BlockSpec((dm, tt), lambda b, i: (0, b * nl + i)), tab, tab, tab, tab],
        out_shape=[jax.ShapeDtypeStruct((bsz, L, dm), F32), jax.ShapeDtypeStruct((dm, n), BF16),
                   tab_shape, tab_shape, tab_bf16, tab_bf16],
        compiler_params=_params(("parallel", "parallel")),
        name="mix",
    )(x, ys, ym, g1, sc2, sh2, gnm, wo1, wo2, l1g, l1b, wqt, k1, k2)


def _peer_kernel(h2t_ref, cnt_ref, e1_ref, rk2_ref, e2_ref, u_ref, vt_ref, x1_ref, g2_ref, l2g_ref, l2b_ref,
                 o_ref, acc, wa0, wa1, *, ni, sub, nq):
    eb = pl.program_id(1)
    ne = pl.num_programs(1) - 1
    dm = acc.shape[0]

    @pl.when(eb == 0)
    def _():
        acc[...] = jnp.zeros_like(acc)
        wa1[...] = jnp.zeros_like(wa1)

    def consume(wa_r, q, nq):
        rows = slice(q * (dm // nq), (q + 1) * (dm // nq))
        acc[rows, :] += jnp.dot(vt_ref[0, rows, :], wa_r[...], preferred_element_type=F32)

    def step(wa_w, wa_r):
        sc = None
        for ii in range(ni):
            i = eb * ni + ii
            if ii % sub == 0:
                sc = jnp.dot(u_ref[ii * PEER_NKEYS:(ii + sub) * PEER_NKEYS, :], h2t_ref[...],
                             preferred_element_type=F32)
            if ii % (ni // nq) == 0:
                consume(wa_r, ii // (ni // nq), nq)
            rows = slice((ii % sub) * PEER_NKEYS, (ii % sub + 1) * PEER_NKEYS)
            w = None
            for hh in range(PEER_HEADS):
                c = cnt_ref[hh, pl.ds(i, 1), :].astype(BF16)
                e = e1_ref[hh, pl.ds(i, 1), :].astype(BF16)
                term = jnp.where(rk2_ref[hh] < c, e2_ref[hh], jnp.zeros((), BF16)) * e
                w = term if w is None else w + term
            wa_w[ii * PEER_NKEYS:(ii + 1) * PEER_NKEYS, :] = w * _gelu(sc[rows].astype(BF16))

    @pl.when((eb % 2 == 0) & (eb < ne))
    def _():
        step(wa0, wa1)

    @pl.when(eb % 2 == 1)
    def _():
        step(wa1, wa0)

    @pl.when(eb == ne)
    def _():
        consume(wa1, 0, 1)
        yff = acc[...].T
        r = DEEPNORM_ALPHA * x1_ref[0] + (1.0 + g2_ref[0]) * yff
        mu = jnp.mean(r, axis=-1, keepdims=True)
        rc = r - mu
        var = jnp.mean(rc * rc, axis=-1, keepdims=True)
        o_ref[0] = rc * lax.rsqrt(var + NORM_EPS) * l2g_ref[...] + l2b_ref[...]


def _peer(h2t, cnt, e1, rk2, e2, ub, vt, x1, g2, l2g, l2b, tt, ni):
    bsz, L, dm = x1.shape
    nl = L // tt
    eb = ni * PEER_NKEYS
    tab = pl.BlockSpec((PEER_HEADS, PEER_NKEYS, tt), lambda t, e: (0, 0, t))
    full = lambda a: pl.BlockSpec(a.shape, lambda t, e: (0,) * a.ndim)
    ne = PEER_NKEYS // ni
    assert ne % 2 == 0 and ni % 4 == 0
    return pl.pallas_call(
        functools.partial(_peer_kernel, ni=ni, sub=2, nq=4),
        grid=(bsz * nl, ne + 1),
        in_specs=[pl.BlockSpec((dm, tt), lambda t, e: (0, t)), tab, tab, tab, tab,
                  pl.BlockSpec((eb, dm), lambda t, e: (jnp.minimum(e, ne - 1), 0)),
                  pl.BlockSpec((1, dm, eb), lambda t, e: (jnp.maximum(e - 1, 0), 0, 0)),
                  pl.BlockSpec((1, tt, dm), lambda t, e: (t // nl, t % nl, 0)),
                  pl.BlockSpec((1, 1, dm), lambda t, e: (t // nl, 0, 0)),
                  full(l2g), full(l2b)],
        out_specs=pl.BlockSpec((1, tt, dm), lambda t, e: (t // nl, t % nl, 0)),
        out_shape=jax.ShapeDtypeStruct((bsz, L, dm), F32),
        scratch_shapes=[pltpu.VMEM((dm, tt), F32), pltpu.VMEM((eb, tt), BF16), pltpu.VMEM((eb, tt), BF16)],
        compiler_params=_params(("parallel", "arbitrary")),
        name="peer",
    )(h2t, cnt, e1, rk2, e2, ub, vt, x1, g2, l2g, l2b)


def _s5_discretize(lam_re, lam_im, log_dt, b_re, b_im, c_re, c_im):
    dt = jnp.exp(log_dt)[:, None]
    mag = jnp.exp(lam_re * dt)
    ab_re = mag * jnp.cos(lam_im * dt)
    ab_im = mag * jnp.sin(lam_im * dt)
    den = lam_re * lam_re + lam_im * lam_im
    nr = ab_re - 1.0
    coef_re = (nr * lam_re + ab_im * lam_im) / den
    coef_im = (ab_im * lam_re - nr * lam_im) / den
    bb_re = coef_re[..., None] * b_re - coef_im[..., None] * b_im
    bb_im = coef_re[..., None] * b_im + coef_im[..., None] * b_re
    nslab = S5_WIDTH // LANES
    gs = S5_GROUPS // nslab
    eye = jnp.eye(gs, dtype=F32)
    slab = lambda a: a.reshape((nslab, gs) + a.shape[1:])
    wb_re = jnp.einsum("sgph,gk->sghkp", slab(bb_re), eye).reshape(nslab, LANES, gs * S5_STATE)
    wb_im = jnp.einsum("sgph,gk->sghkp", slab(bb_im), eye).reshape(nslab, LANES, gs * S5_STATE)
    wb = jnp.concatenate([wb_re, wb_im], axis=2).astype(BF16)
    wc_re = jnp.einsum("sghp,gk->sgpkh", slab(c_re), eye).reshape(nslab, gs * S5_STATE, LANES)
    wc_im = jnp.einsum("sghp,gk->sgpkh", slab(c_im), eye).reshape(nslab, gs * S5_STATE, LANES)
    wc = jnp.concatenate([wc_re, -wc_im], axis=1).astype(BF16)
    return wb, wc, ab_re.reshape(1, S5_COLS), ab_im.reshape(1, S5_COLS)


def _pad_heads(w, width, offset):
    kdim = w.shape[0]
    w = w.reshape(kdim, MLA_HEADS, width)
    out = jnp.zeros((kdim, MLA_HEADS, HEAD_PAD), w.dtype)
    out = out.at[:, :, offset:offset + width].set(w)
    return out.reshape(kdim, MLA_HEADS * HEAD_PAD)


def _swap_halves(w):
    half = w.shape[-1] // 2
    return jnp.concatenate([w[..., half:], w[..., :half]], axis=-1)


def kernel(x, c, positions, w_ada, b_ada, w_in, s5_lambda_re, s5_lambda_im, s5_log_dt, s5_b_re, s5_b_im, s5_c_re, s5_c_im, s5_d, s5_w_glu, s5_b_glu, mla_q_norm, mla_w_uq, mla_kv_norm, mla_w_ukv, gn_s5, gn_mla, w_out, ln1_g, ln1_b, peer_w_query, peer_keys1, peer_keys2, peer_u, peer_v, ln2_g, ln2_b):
    bsz, L, dm = x.shape
    depth = w_ada.shape[0]
    t_proj = min(512, L)
    t_s5 = min(128, L)
    t_attn = min(512, L)
    t_mix = min(256, L)
    t_peer = min(512, L)
    peer_ni = 16

    invf = (ROPE_THETA ** (-jnp.arange(0, MLA_ROPE, 2, dtype=F32) / MLA_ROPE))[:, None]
    posf = positions.astype(F32)[:, None, :]

    for l in range(depth):
        mod = _ada(c, w_ada[l], b_ada[l])
        sh1, sc1, g1, sh2, sc2, g2 = [m[:, None, :] for m in jnp.split(mod, 6, axis=-1)]

        wi = w_in[l]
        o1, o2, o3 = S5_WIDTH, S5_WIDTH + Q_LORA, S5_WIDTH + Q_LORA + KV_LORA
        w_kr = wi[:, o3:]
        zpad = lambda w: jnp.zeros((dm, HEAD_PAD), F32).at[:, MLA_NOPE:MLA_NOPE + MLA_ROPE].set(w)
        win = jnp.concatenate([wi[:, :o3], zpad(w_kr), zpad(_swap_halves(w_kr))], axis=1).astype(BF16)
        scale = (MLA_NOPE + MLA_ROPE) ** -0.5 * math.log2(math.e)
        wuq = (mla_w_uq[l] * mla_q_norm[l][:, None] * scale).reshape(Q_LORA, MLA_HEADS, MLA_NOPE + MLA_ROPE)
        wuq_main = wuq.reshape(Q_LORA, MLA_HEADS * (MLA_NOPE + MLA_ROPE))
        wqa = _pad_heads(wuq_main, MLA_NOPE + MLA_ROPE, 0).astype(BF16)
        wuq_sw = _swap_halves(wuq[:, :, MLA_NOPE:]).reshape(Q_LORA, MLA_HEADS * MLA_ROPE)
        wqb = _pad_heads(wuq_sw, MLA_ROPE, MLA_NOPE).astype(BF16)
        wukv = (mla_w_ukv[l] * mla_kv_norm[l][:, None]).reshape(KV_LORA, MLA_HEADS, MLA_NOPE + MLA_V)
        wk = _pad_heads(wukv[:, :, :MLA_NOPE].reshape(KV_LORA, MLA_HEADS * MLA_NOPE), MLA_NOPE, 0).astype(BF16)
        wv = wukv[:, :, MLA_NOPE:].reshape(KV_LORA, MLA_HEADS * MLA_V).T.astype(BF16)

        u, q, k, v = _proj(x, posf, sc1, sh1, win, wqa, wqb, wk, wv, invf, t_proj)

        wb, wc, are, aim = _s5_discretize(s5_lambda_re[l], s5_lambda_im[l], s5_log_dt[l], s5_b_re[l],
                                          s5_b_im[l], s5_c_re[l], s5_c_im[l])
        are = jnp.broadcast_to(are, (bsz, S5_COLS))
        aim = jnp.broadcast_to(aim, (bsz, S5_COLS))
        y_s5 = _s5(u, wb, wc, are, aim, s5_d[l][None], s5_w_glu[l].astype(BF16), s5_b_glu[l][None],
                   gn_s5[l][None], t_s5)

        y_mla = _attn(q, k, v, t_attn)

        wo = w_out[l].astype(BF16)
        wqt = peer_w_query[l].T.astype(BF16)
        x1, h2t, cnt, e1, rk2, e2 = _mix(x, y_s5, y_mla, g1, sc2, sh2, gn_mla[l][None], wo[:S5_WIDTH],
                                         wo[S5_WIDTH:], ln1_g[l][None], ln1_b[l][None], wqt,
                                         peer_keys1[l].astype(BF16), peer_keys2[l].astype(BF16), t_mix)

        vt = peer_v[l].astype(BF16).reshape(PEER_NKEYS // peer_ni, peer_ni * PEER_NKEYS, dm).transpose(0, 2, 1)
        x = _peer(h2t, cnt, e1, rk2, e2, peer_u[l].astype(BF16), vt, x1, g2,
                  ln2_g[l][None], ln2_b[l][None], t_peer, peer_ni)
    return x
```

```python
import functools
import math
from typing import NamedTuple

import jax
import jax.numpy as jnp
from jax import lax
from jax.experimental import pallas as pl
from jax.experimental.pallas import tpu as pltpu

F32 = jnp.float32
BF16 = jnp.bfloat16

D_MODEL = 1024
CHUNK = 64
S5_WIDTH = 512
S5_GROUP = 16
S5_GROUPS = 32
S5_STATE = 64
S5_COLS = S5_GROUPS * S5_STATE
MLA_WIDTH = 512
MLA_HEADS = 8
MLA_NOPE = 64
MLA_ROPE = 32
MLA_V = 64
Q_LORA = 384
KV_LORA = 256
ROPE_THETA = 10000.0
HEAD_PAD = 128
PEER_HEADS = 8
PEER_NKEYS = 128
PEER_QDIM = 256
PEER_HALF = 128
PEER_TOPK = 16
PEER_EXPERTS = PEER_NKEYS * PEER_NKEYS
DEEPNORM_ALPHA = 2.0 ** 0.25
NORM_EPS = 1e-6
MASK_VALUE = -1e30
NEG_BIG = -3.0e38

LANES = 128
SUBLANES = 8
VMEM_LIMIT = 56 * 1024 * 1024


def _gelu(x):
    return 0.5 * x * (1.0 + jnp.tanh(0.7978845608028654 * (x + 0.044715 * (x * x * x))))


def _sigmoid(x):
    return 1.0 / (1.0 + jnp.exp(-x))


class _Tiles(NamedTuple):
    proj: int
    s5: int
    attn: int
    mix: int
    peer: int
    peer_ni: int
    peer_sub: int
    peer_nq: int
    scan_cols: int


def _tiles(seq_len):
    return _Tiles(proj=min(512, seq_len), s5=min(128, seq_len), attn=min(512, seq_len), mix=min(256, seq_len),
                  peer=min(512, seq_len), peer_ni=16, peer_sub=2, peer_nq=4, scan_cols=512)


def _params(sem):
    return pltpu.CompilerParams(dimension_semantics=sem, vmem_limit_bytes=VMEM_LIMIT)


def _ada_kernel(c_ref, w_ref, b_ref, o_ref):
    c = c_ref[...]
    ca = c * _sigmoid(c)
    o_ref[...] = jnp.dot(ca, w_ref[...], preferred_element_type=F32,
                         precision=lax.Precision.HIGHEST) + b_ref[...]


def _ada(c, w_ada, b_ada):
    bsz, dm = c.shape
    n = w_ada.shape[1]
    tn = dm
    return pl.pallas_call(
        _ada_kernel,
        grid=(n // tn,),
        in_specs=[pl.BlockSpec((bsz, dm), lambda j: (0, 0)),
                  pl.BlockSpec((dm, tn), lambda j: (0, j)),
                  pl.BlockSpec((1, tn), lambda j: (0, j))],
        out_specs=pl.BlockSpec((bsz, tn), lambda j: (0, j)),
        out_shape=jax.ShapeDtypeStruct((bsz, n), F32),
        compiler_params=_params(("arbitrary",)),
        name="ada",
    )(c, w_ada, b_ada.reshape(1, n))


def _proj_kernel(x_ref, pos_ref, sc_ref, sh_ref, win_ref, wqa_ref, wqb_ref, wk_ref, wv_ref,
                 invf_ref, u_ref, q_ref, k_ref, v_ref):
    x = x_ref[0]
    h = (x * (1.0 + sc_ref[0]) + sh_ref[0]).astype(BF16)
    proj = jnp.dot(h, win_ref[...], preferred_element_type=F32)
    o = S5_WIDTH
    u_ref[0] = proj[:, :o]
    cq = proj[:, o:o + Q_LORA]
    o += Q_LORA
    ckv = proj[:, o:o + KV_LORA]
    o += KV_LORA
    kra = proj[:, o:o + HEAD_PAD]
    krb = proj[:, o + HEAD_PAD:o + 2 * HEAD_PAD]
    ang = invf_ref[...] * pos_ref[0]
    cos_t = jnp.cos(ang)
    sin_t = jnp.sin(ang)
    tt = ang.shape[1]
    cs = jnp.concatenate([jnp.ones((MLA_NOPE, tt), F32), cos_t, cos_t,
                          jnp.zeros((HEAD_PAD - MLA_NOPE - MLA_ROPE, tt), F32)], axis=0).T
    sn = jnp.concatenate([jnp.zeros((MLA_NOPE, tt), F32), -sin_t, sin_t,
                          jnp.zeros((HEAD_PAD - MLA_NOPE - MLA_ROPE, tt), F32)], axis=0).T
    cqn = (cq * lax.rsqrt(jnp.mean(cq * cq, axis=-1, keepdims=True) + NORM_EPS)).astype(BF16)
    qa = jnp.dot(cqn, wqa_ref[...], preferred_element_type=F32)
    qb = jnp.dot(cqn, wqb_ref[...], preferred_element_type=F32)
    ckvn = (ckv * lax.rsqrt(jnp.mean(ckv * ckv, axis=-1, keepdims=True) + NORM_EPS)).astype(BF16)
    kn = jnp.dot(ckvn, wk_ref[...], preferred_element_type=F32)
    kp = kra * cs + krb * sn
    for hh in range(MLA_HEADS):
        sl = slice(hh * HEAD_PAD, (hh + 1) * HEAD_PAD)
        q_ref[0, :, sl] = (qa[:, sl] * cs + qb[:, sl] * sn).astype(BF16)
        k_ref[0, :, sl] = (kn[:, sl] + kp).astype(BF16)
    v_ref[0] = lax.dot_general(wv_ref[...], ckvn, (((1,), (1,)), ((), ())),
                               preferred_element_type=F32).astype(BF16)


def _proj(x, posf, sc1, sh1, win, wqa, wqb, wk, wv, invf, tt):
    bsz, L, dm = x.shape
    hp = MLA_HEADS * HEAD_PAD
    full = lambda a: pl.BlockSpec(a.shape, lambda b, i: (0,) * a.ndim)
    return pl.pallas_call(
        _proj_kernel,
        grid=(bsz, L // tt),
        in_specs=[pl.BlockSpec((1, tt, dm), lambda b, i: (b, i, 0)),
                  pl.BlockSpec((1, 1, tt), lambda b, i: (b, 0, i)),
                  pl.BlockSpec((1, 1, dm), lambda b, i: (b, 0, 0)),
                  pl.BlockSpec((1, 1, dm), lambda b, i: (b, 0, 0)),
                  full(win), full(wqa), full(wqb), full(wk), full(wv), full(invf)],
        out_specs=[pl.BlockSpec((1, tt, S5_WIDTH), lambda b, i: (b, i, 0)),
                   pl.BlockSpec((1, tt, hp), lambda b, i: (b, i, 0)),
                   pl.BlockSpec((1, tt, hp), lambda b, i: (b, i, 0)),
                   pl.BlockSpec((1, MLA_WIDTH, tt), lambda b, i: (b, 0, i))],
        out_shape=[jax.ShapeDtypeStruct((bsz, L, S5_WIDTH), F32),
                   jax.ShapeDtypeStruct((bsz, L, hp), BF16),
                   jax.ShapeDtypeStruct((bsz, L, hp), BF16),
                   jax.ShapeDtypeStruct((bsz, MLA_WIDTH, L), BF16)],
        compiler_params=_params(("parallel", "parallel")),
        name="proj",
    )(x, posf, sc1, sh1, win, wqa, wqb, wk, wv, invf)


def _s5_kernel(u_ref, wb_ref, wc_ref, are_ref, aim_ref, d_ref, wglu_ref, bglu_ref, gn_ref,
               y_ref, utm, st, yt, carry, *, tc, bsz, cb):
    @pl.when(pl.program_id(0) == 0)
    def _():
        carry[...] = jnp.zeros_like(carry)

    nslab = S5_WIDTH // LANES
    for b in range(bsz):
        for j in range(nslab):
            utm[j, pl.ds(b, tc, stride=bsz), :] = u_ref[b, :, j * LANES:(j + 1) * LANES]
    scols = S5_COLS // nslab
    us = [utm[j] for j in range(nslab)]
    for j in range(nslab):
        bu = jnp.dot(us[j].astype(BF16), wb_ref[j], preferred_element_type=F32)
        st[:, j * scols:(j + 1) * scols] = bu[:, :scols]
        st[:, S5_COLS + j * scols:S5_COLS + (j + 1) * scols] = bu[:, scols:]

    for c0 in range(0, S5_COLS, cb):
        cre = slice(c0, c0 + cb)
        cim = slice(S5_COLS + c0, S5_COLS + c0 + cb)
        are = are_ref[:, cre]
        aim = aim_ref[:, cre]

        def step(t, sc, cre=cre, cim=cim, are=are, aim=aim):
            sre, sim = sc
            r = pl.multiple_of(t * bsz, bsz)
            nre = are * sre - aim * sim + st[pl.ds(r, bsz), cre]
            nim = are * sim + aim * sre + st[pl.ds(r, bsz), cim]
            st[pl.ds(r, bsz), cre] = nre
            st[pl.ds(r, bsz), cim] = nim
            return nre, nim

        sre, sim = lax.fori_loop(0, tc, step, (carry[:, cre], carry[:, cim]), unroll=4)
        carry[:, cre] = sre
        carry[:, cim] = sim

    ys = []
    for j in range(nslab):
        s_re = st[:, j * scols:(j + 1) * scols].astype(BF16)
        s_im = st[:, S5_COLS + j * scols:S5_COLS + (j + 1) * scols].astype(BF16)
        ys.append(jnp.dot(s_re, wc_ref[j, :scols, :], preferred_element_type=F32)
                  + jnp.dot(s_im, wc_ref[j, scols:, :], preferred_element_type=F32)
                  + d_ref[:, j * LANES:(j + 1) * LANES] * us[j])
    y = jnp.concatenate(ys, axis=-1)
    y = _gelu(y)
    z = jnp.dot(y.astype(BF16), wglu_ref[...], preferred_element_type=F32) + bglu_ref[...]
    y = y * _sigmoid(z)
    y = y * lax.rsqrt(jnp.mean(y * y, axis=-1, keepdims=True) + NORM_EPS) * gn_ref[...]
    for j in range(nslab):
        yt[j] = y[:, j * LANES:(j + 1) * LANES]
    for b in range(bsz):
        for j in range(nslab):
            y_ref[b, :, j * LANES:(j + 1) * LANES] = yt[j, pl.ds(b, tc, stride=bsz), :].astype(BF16)


def _s5(u, wb, wc, are, aim, d, wglu, bglu, gn, tc, scan_cols):
    bsz, L, w = u.shape
    full = lambda a: pl.BlockSpec(a.shape, lambda i: (0,) * a.ndim)
    return pl.pallas_call(
        functools.partial(_s5_kernel, tc=tc, bsz=bsz, cb=scan_cols),
        grid=(L // tc,),
        in_specs=[pl.BlockSpec((bsz, tc, w), lambda i: (0, i, 0)),
                  full(wb), full(wc), full(are), full(aim), full(d), full(wglu), full(bglu), full(gn)],
        out_specs=pl.BlockSpec((bsz, tc, w), lambda i: (0, i, 0)),
        out_shape=jax.ShapeDtypeStruct((bsz, L, w), BF16),
        scratch_shapes=[pltpu.VMEM((w // LANES, tc * bsz, LANES), F32),
                        pltpu.VMEM((tc * bsz, 2 * S5_COLS), F32),
                        pltpu.VMEM((w // LANES, tc * bsz, LANES), F32),
                        pltpu.VMEM((bsz, 2 * S5_COLS), F32)],
        compiler_params=_params(("arbitrary",)),
        name="s5",
    )(u, wb, wc, are, aim, d, wglu, bglu, gn)


def _attn_kernel(q_ref, k_ref, vt_ref, o_ref, sa, sb, mxa, mxb, m_s, l_s, acc_s, *, tq):
    qi = pl.program_id(2)
    nt = (((1,), (1,)), ((), ()))
    qs = [q_ref[0, :, hh * HEAD_PAD:(hh + 1) * HEAD_PAD] for hh in range(2)]

    m_s[...] = jnp.full(m_s.shape, MASK_VALUE, F32)
    l_s[...] = jnp.zeros_like(l_s)
    acc_s[...] = jnp.zeros_like(acc_s)

    def scores_into(kb, sbuf, mxbuf):
        ks = pl.multiple_of(kb * tq, tq)
        for hh in range(2):
            st = lax.dot_general(k_ref[0, pl.ds(ks, tq), hh * HEAD_PAD:(hh + 1) * HEAD_PAD], qs[hh], nt,
                                 preferred_element_type=F32)
            sbuf[hh] = st
            mxbuf[hh] = jnp.max(st, axis=0, keepdims=True)

    def softmax_from(kb, sbuf, mxbuf, masked):
        ks = pl.multiple_of(kb * tq, tq)
        for hh in range(2):
            st = sbuf[hh]
            if masked:
                kc = lax.broadcasted_iota(jnp.int32, (tq, tq), 0) // CHUNK
                qc = lax.broadcasted_iota(jnp.int32, (tq, tq), 1) // CHUNK
                st = jnp.where(kc <= qc, st, MASK_VALUE)
                mx = jnp.max(st, axis=0, keepdims=True)
            else:
                mx = mxbuf[hh]
            vt = vt_ref[0, hh * MLA_V:(hh + 1) * MLA_V, pl.ds(ks, tq)]
            m = m_s[hh]
            m_new = jnp.maximum(m, mx)
            alpha = jnp.exp2(m - m_new)
            pt = jnp.exp2(st - m_new)
            l_s[hh] = alpha * l_s[hh] + jnp.sum(pt, axis=0, keepdims=True)
            acc_s[hh] = alpha * acc_s[hh] + jnp.dot(vt, pt.astype(BF16), preferred_element_type=F32)
            m_s[hh] = m_new

    def finish():
        o_ref[0] = jnp.concatenate([acc_s[0] / l_s[0], acc_s[1] / l_s[1]], axis=0).T.astype(BF16)

    scores_into(0, sa, mxa)

    def pair(j, carry):
        scores_into(2 * j + 1, sb, mxb)
        softmax_from(2 * j, sa, mxa, False)
        scores_into(2 * j + 2, sa, mxa)
        softmax_from(2 * j + 1, sb, mxb, False)
        return carry

    lax.fori_loop(0, qi // 2, pair, 0)

    @pl.when(qi % 2 == 0)
    def _():
        softmax_from(qi, sa, mxa, True)
        finish()

    @pl.when(qi % 2 == 1)
    def _():
        scores_into(qi, sb, mxb)
        softmax_from(qi - 1, sa, mxa, False)
        softmax_from(qi, sb, mxb, True)
        finish()


def _attn(q, k, vt, tq):
    bsz, L, hp = q.shape
    return pl.pallas_call(
        functools.partial(_attn_kernel, tq=tq),
        grid=(bsz, MLA_HEADS // 2, L // tq),
        in_specs=[pl.BlockSpec((1, tq, 2 * HEAD_PAD), lambda b, h, i: (b, i, h)),
                  pl.BlockSpec((1, L, 2 * HEAD_PAD), lambda b, h, i: (b, 0, h)),
                  pl.BlockSpec((1, 2 * MLA_V, L), lambda b, h, i: (b, h, 0))],
        out_specs=pl.BlockSpec((1, tq, 2 * MLA_V), lambda b, h, i: (b, i, h)),
        out_shape=jax.ShapeDtypeStruct((bsz, L, MLA_WIDTH), BF16),
        scratch_shapes=[pltpu.VMEM((2, tq, tq), F32), pltpu.VMEM((2, tq, tq), F32),
                        pltpu.VMEM((2, 1, tq), F32), pltpu.VMEM((2, 1, tq), F32),
                        pltpu.VMEM((2, 1, tq), F32), pltpu.VMEM((2, 1, tq), F32),
                        pltpu.VMEM((2, MLA_V, tq), F32)],
        compiler_params=_params(("parallel", "parallel", "arbitrary")),
        name="attn",
    )(q, k, vt)


def _top16_rows(s, want_rank):
    work = s
    rank = jnp.full(s.shape, 127.0, F32) if want_rank else None
    ridx = lax.broadcasted_iota(jnp.int32, (PEER_TOPK, s.shape[1]), 0)
    vals = jnp.zeros((PEER_TOPK, s.shape[1]), F32)
    for r in range(PEER_TOPK):
        m = jnp.max(work, axis=0, keepdims=True)
        hit = work == m
        if want_rank:
            rank = jnp.where(hit, float(r), rank)
        work = jnp.where(hit, NEG_BIG, work)
        vals = jnp.where(ridx == r, m, vals)
    return vals, rank


def _sort16_pairs():
    n, pairs, p = PEER_TOPK, [], 1
    while p < n:
        k = p
        while k >= 1:
            for j in range(k % p, n - k, 2 * k):
                for i in range(min(k, n - j - k)):
                    if (i + j) // (2 * p) == (i + j + k) // (2 * p):
                        pairs.append((i + j, i + j + k))
            k //= 2
        p *= 2
    return pairs


def _top16_sorted(x):
    n = PEER_TOPK
    a = [x[SUBLANES * k:SUBLANES * (k + 1)] for k in range(n)]

    def exchange(i, j):
        a[i], a[j] = jnp.maximum(a[i], a[j]), jnp.minimum(a[i], a[j])

    for i, j in _sort16_pairs():
        exchange(i, j)
    shift = SUBLANES // 2
    while shift >= 1:
        a = [jnp.maximum(a[k], pltpu.roll(a[n - 1 - k], shift, 0)) for k in range(n)]
        d = n // 2
        while d >= 1:
            for i in range(n):
                if i & d == 0:
                    exchange(i, i + d)
            d //= 2
        shift //= 2
    return a


def _mix_kernel(x_ref, ys_ref, ym_ref, g1_ref, sc2_ref, sh2_ref, gnm_ref, wo1_ref, wo2_ref,
                l1g_ref, l1b_ref, wq_ref, k1_ref, k2_ref,
                x1_ref, h2t_ref, cnt_ref, e1_ref, rk2_ref, e2_ref):
    x = x_ref[0]
    ym = ym_ref[0].astype(F32)
    ym = (ym * lax.rsqrt(jnp.mean(ym * ym, axis=-1, keepdims=True) + NORM_EPS) * gnm_ref[...]).astype(BF16)
    ymix = (jnp.dot(ys_ref[0], wo1_ref[...], preferred_element_type=F32)
            + jnp.dot(ym, wo2_ref[...], preferred_element_type=F32))
    r = DEEPNORM_ALPHA * x + (1.0 + g1_ref[0]) * ymix
    mu = jnp.mean(r, axis=-1, keepdims=True)
    rc = r - mu
    var = jnp.mean(rc * rc, axis=-1, keepdims=True)
    x1 = rc * lax.rsqrt(var + NORM_EPS) * l1g_ref[...] + l1b_ref[...]
    x1_ref[0] = x1
    h2 = x1 * (1.0 + sc2_ref[0]) + sh2_ref[0]
    h2t = h2.T.astype(BF16)
    h2t_ref[...] = h2t
    qt = jnp.dot(wq_ref[...], h2t, preferred_element_type=F32)
    for hh in range(PEER_HEADS):
        q1 = qt[hh * PEER_QDIM:hh * PEER_QDIM + PEER_HALF]
        q2 = qt[hh * PEER_QDIM + PEER_HALF:(hh + 1) * PEER_QDIM]
        s1 = jnp.dot(k1_ref[hh], q1.astype(BF16), preferred_element_type=F32)
        s2 = jnp.dot(k2_ref[hh], q2.astype(BF16), preferred_element_type=F32)
        r1 = _top16_sorted(s1)
        v1 = [r[0:1] for r in r1]
        v2, rk2 = _top16_rows(s2, True)
        cands = [v1[0] + v2]
        for a in range(1, SUBLANES):
            cands.append(r1[a] + v2[0:SUBLANES])
        sub_row = lax.broadcasted_iota(jnp.int32, r1[0].shape, 0)
        tail = r1[SUBLANES]
        for a in range(1, SUBLANES):
            tail = jnp.where(sub_row == a, r1[SUBLANES + a], tail)
        cands.append(tail + v2[0:1])
        top, _ = _top16_rows(jnp.concatenate(cands, axis=0), False)
        theta = top[PEER_TOPK - 1:PEER_TOPK]
        z = jnp.sum(jnp.exp(top - top[0:1]), axis=0, keepdims=True)
        dense_b = 4
        cnt = jnp.zeros_like(s1)
        for b in range(dense_b):
            cnt = cnt + jnp.where(s1 + v2[b:b + 1] >= theta, 1.0, 0.0)
        for a in range(3):
            extra = jnp.sum(jnp.where(v1[a] + v2[dense_b:] >= theta, 1.0, 0.0), axis=0, keepdims=True)
            cnt = cnt + jnp.where(s1 == v1[a], extra, 0.0)
        cnt_ref[hh] = jnp.where(s1 >= v1[PEER_TOPK - 1], cnt, 0.0)
        e1_ref[hh] = jnp.exp(s1 - v1[0])
        rk2_ref[hh] = rk2.astype(BF16)
        e2_ref[hh] = (jnp.exp(s2 - v2[0:1]) / z).astype(BF16)


def _mix(x, ys, ym, g1, sc2, sh2, gnm, wo1, wo2, l1g, l1b, wqt, k1, k2, tt):
    bsz, L, dm = x.shape
    n = bsz * L
    nl = L // tt
    full = lambda a: pl.BlockSpec(a.shape, lambda b, i: (0,) * a.ndim)
    tok = lambda w: pl.BlockSpec((1, tt, w), lambda b, i: (b, i, 0))
    vec = pl.BlockSpec((1, 1, dm), lambda b, i: (b, 0, 0))
    tab = pl.BlockSpec((PEER_HEADS, PEER_NKEYS, tt), lambda b, i: (0, 0, b * nl + i))
    tab_shape = jax.ShapeDtypeStruct((PEER_HEADS, PEER_NKEYS, n), F32)
    tab_bf16 = jax.ShapeDtypeStruct((PEER_HEADS, PEER_NKEYS, n), BF16)
    return pl.pallas_call(
        _mix_kernel,
        grid=(bsz, nl),
        in_specs=[tok(dm), tok(S5_WIDTH), tok(MLA_WIDTH), vec, vec, vec,
                  full(gnm), full(wo1), full(wo2), full(l1g), full(l1b), full(wqt), full(k1), full(k2)],
        out_specs=[tok(dm), pl.BlockSpec((dm, tt), lambda b, i: (0, b * nl + i)), tab, tab, tab, tab],
        out_shape=[jax.ShapeDtypeStruct((bsz, L, dm), F32), jax.ShapeDtypeStruct((dm, n), BF16),
                   tab_shape, tab_shape, tab_bf16, tab_bf16],
        compiler_params=_params(("parallel", "parallel")),
        name="mix",
    )(x, ys, ym, g1, sc2, sh2, gnm, wo1, wo2, l1g, l1b, wqt, k1, k2)


def _peer_kernel(h2t_ref, cnt_ref, e1_ref, rk2_ref, e2_ref, u_ref, vt_ref, x1_ref, g2_ref, l2g_ref, l2b_ref,
                 o_ref, acc, wa0, wa1, *, ni, sub, nq):
    eb = pl.program_id(1)
    ne = pl.num_programs(1) - 1
    dm = acc.shape[0]

    @pl.when(eb == 0)
    def _():
        acc[...] = jnp.zeros_like(acc)
        wa1[...] = jnp.zeros_like(wa1)

    def consume(wa_r, q, nq):
        rows = slice(q * (dm // nq), (q + 1) * (dm // nq))
        acc[rows, :] += jnp.dot(vt_ref[0, rows, :], wa_r[...], preferred_element_type=F32)

    def step(wa_w, wa_r):
        sc = None
        for ii in range(ni):
            i = eb * ni + ii
            if ii % sub == 0:
                sc = jnp.dot(u_ref[ii * PEER_NKEYS:(ii + sub) * PEER_NKEYS, :], h2t_ref[...],
                             preferred_element_type=F32)
            if ii % (ni // nq) == 0:
                consume(wa_r, ii // (ni // nq), nq)
            rows = slice((ii % sub) * PEER_NKEYS, (ii % sub + 1) * PEER_NKEYS)
            w = None
            for hh in range(PEER_HEADS):
                c = cnt_ref[hh, pl.ds(i, 1), :].astype(BF16)
                e = e1_ref[hh, pl.ds(i, 1), :].astype(BF16)
                term = jnp.where(rk2_ref[hh] < c, e2_ref[hh], jnp.zeros((), BF16)) * e
                w = term if w is None else w + term
            wa_w[ii * PEER_NKEYS:(ii + 1) * PEER_NKEYS, :] = w * _gelu(sc[rows].astype(BF16))

    @pl.when((eb % 2 == 0) & (eb < ne))
    def _():
        step(wa0, wa1)

    @pl.when(eb % 2 == 1)
    def _():
        step(wa1, wa0)

    @pl.when(eb == ne)
    def _():
        consume(wa1, 0, 1)
        yff = acc[...].T
        r = DEEPNORM_ALPHA * x1_ref[0] + (1.0 + g2_ref[0]) * yff
        mu = jnp.mean(r, axis=-1, keepdims=True)
        rc = r - mu
        var = jnp.mean(rc * rc, axis=-1, keepdims=True)
        o_ref[0] = rc * lax.rsqrt(var + NORM_EPS) * l2g_ref[...] + l2b_ref[...]


def _peer(h2t, cnt, e1, rk2, e2, ub, vt, x1, g2, l2g, l2b, tiles):
    bsz, L, dm = x1.shape
    tt, ni = tiles.peer, tiles.peer_ni
    nl = L // tt
    eb = ni * PEER_NKEYS
    tab = pl.BlockSpec((PEER_HEADS, PEER_NKEYS, tt), lambda t, e: (0, 0, t))
    full = lambda a: pl.BlockSpec(a.shape, lambda t, e: (0,) * a.ndim)
    ne = PEER_NKEYS // ni
    assert ne % 2 == 0 and ni % tiles.peer_sub == 0 and ni % tiles.peer_nq == 0
    return pl.pallas_call(
        functools.partial(_peer_kernel, ni=ni, sub=tiles.peer_sub, nq=tiles.peer_nq),
        grid=(bsz * nl, ne + 1),
        in_specs=[pl.BlockSpec((dm, tt), lambda t, e: (0, t)), tab, tab, tab, tab,
                  pl.BlockSpec((eb, dm), lambda t, e: (jnp.minimum(e, ne - 1), 0)),
                  pl.BlockSpec((1, dm, eb), lambda t, e: (jnp.maximum(e - 1, 0), 0, 0)),
                  pl.BlockSpec((1, tt, dm), lambda t, e: (t // nl, t % nl, 0)),
                  pl.BlockSpec((1, 1, dm), lambda t, e: (t // nl, 0, 0)),
                  full(l2g), full(l2b)],
        out_specs=pl.BlockSpec((1, tt, dm), lambda t, e: (t // nl, t % nl, 0)),
        out_shape=jax.ShapeDtypeStruct((bsz, L, dm), F32),
        scratch_shapes=[pltpu.VMEM((dm, tt), F32), pltpu.VMEM((eb, tt), BF16), pltpu.VMEM((eb, tt), BF16)],
        compiler_params=_params(("parallel", "arbitrary")),
        name="peer",
    )(h2t, cnt, e1, rk2, e2, ub, vt, x1, g2, l2g, l2b)


def _s5_discretize(lam_re, lam_im, log_dt, b_re, b_im, c_re, c_im):
    dt = jnp.exp(log_dt)[:, None]
    mag = jnp.exp(lam_re * dt)
    ab_re = mag * jnp.cos(lam_im * dt)
    ab_im = mag * jnp.sin(lam_im * dt)
    den = lam_re * lam_re + lam_im * lam_im
    nr = ab_re - 1.0
    coef_re = (nr * lam_re + ab_im * lam_im) / den
    coef_im = (ab_im * lam_re - nr * lam_im) / den
    bb_re = coef_re[..., None] * b_re - coef_im[..., None] * b_im
    bb_im = coef_re[..., None] * b_im + coef_im[..., None] * b_re
    nslab = S5_WIDTH // LANES
    gs = S5_GROUPS // nslab
    eye = jnp.eye(gs, dtype=F32)
    slab = lambda a: a.reshape((nslab, gs) + a.shape[1:])
    wb_re = jnp.einsum("sgph,gk->sghkp", slab(bb_re), eye).reshape(nslab, LANES, gs * S5_STATE)
    wb_im = jnp.einsum("sgph,gk->sghkp", slab(bb_im), eye).reshape(nslab, LANES, gs * S5_STATE)
    wb = jnp.concatenate([wb_re, wb_im], axis=2).astype(BF16)
    wc_re = jnp.einsum("sghp,gk->sgpkh", slab(c_re), eye).reshape(nslab, gs * S5_STATE, LANES)
    wc_im = jnp.einsum("sghp,gk->sgpkh", slab(c_im), eye).reshape(nslab, gs * S5_STATE, LANES)
    wc = jnp.concatenate([wc_re, -wc_im], axis=1).astype(BF16)
    return wb, wc, ab_re.reshape(1, S5_COLS), ab_im.reshape(1, S5_COLS)


def _pad_heads(w, width, offset):
    kdim = w.shape[0]
    w = w.reshape(kdim, MLA_HEADS, width)
    out = jnp.zeros((kdim, MLA_HEADS, HEAD_PAD), w.dtype)
    out = out.at[:, :, offset:offset + width].set(w)
    return out.reshape(kdim, MLA_HEADS * HEAD_PAD)


def _swap_halves(w):
    half = w.shape[-1] // 2
    return jnp.concatenate([w[..., half:], w[..., :half]], axis=-1)


def kernel(x, c, positions, w_ada, b_ada, w_in, s5_lambda_re, s5_lambda_im, s5_log_dt, s5_b_re, s5_b_im, s5_c_re, s5_c_im, s5_d, s5_w_glu, s5_b_glu, mla_q_norm, mla_w_uq, mla_kv_norm, mla_w_ukv, gn_s5, gn_mla, w_out, ln1_g, ln1_b, peer_w_query, peer_keys1, peer_keys2, peer_u, peer_v, ln2_g, ln2_b):
    bsz, L, dm = x.shape
    depth = w_ada.shape[0]
    tiles = _tiles(L)

    invf = (ROPE_THETA ** (-jnp.arange(0, MLA_ROPE, 2, dtype=F32) / MLA_ROPE))[:, None]
    posf = positions.astype(F32)[:, None, :]

    for l in range(depth):
        mod = _ada(c, w_ada[l], b_ada[l])
        sh1, sc1, g1, sh2, sc2, g2 = [m[:, None, :] for m in jnp.split(mod, 6, axis=-1)]

        wi = w_in[l]
        o1, o2, o3 = S5_WIDTH, S5_WIDTH + Q_LORA, S5_WIDTH + Q_LORA + KV_LORA
        w_kr = wi[:, o3:]
        zpad = lambda w: jnp.zeros((dm, HEAD_PAD), F32).at[:, MLA_NOPE:MLA_NOPE + MLA_ROPE].set(w)
        win = jnp.concatenate([wi[:, :o3], zpad(w_kr), zpad(_swap_halves(w_kr))], axis=1).astype(BF16)
        scale = (MLA_NOPE + MLA_ROPE) ** -0.5 * math.log2(math.e)
        wuq = (mla_w_uq[l] * mla_q_norm[l][:, None] * scale).reshape(Q_LORA, MLA_HEADS, MLA_NOPE + MLA_ROPE)
        wuq_main = wuq.reshape(Q_LORA, MLA_HEADS * (MLA_NOPE + MLA_ROPE))
        wqa = _pad_heads(wuq_main, MLA_NOPE + MLA_ROPE, 0).astype(BF16)
        wuq_sw = _swap_halves(wuq[:, :, MLA_NOPE:]).reshape(Q_LORA, MLA_HEADS * MLA_ROPE)
        wqb = _pad_heads(wuq_sw, MLA_ROPE, MLA_NOPE).astype(BF16)
        wukv = (mla_w_ukv[l] * mla_kv_norm[l][:, None]).reshape(KV_LORA, MLA_HEADS, MLA_NOPE + MLA_V)
        wk = _pad_heads(wukv[:, :, :MLA_NOPE].reshape(KV_LORA, MLA_HEADS * MLA_NOPE), MLA_NOPE, 0).astype(BF16)
        wv = wukv[:, :, MLA_NOPE:].reshape(KV_LORA, MLA_HEADS * MLA_V).T.astype(BF16)

        u, q, k, v = _proj(x, posf, sc1, sh1, win, wqa, wqb, wk, wv, invf, tiles.proj)

        wb, wc, are, aim = _s5_discretize(s5_lambda_re[l], s5_lambda_im[l], s5_log_dt[l], s5_b_re[l],
                                          s5_b_im[l], s5_c_re[l], s5_c_im[l])
        are = jnp.broadcast_to(are, (bsz, S5_COLS))
        aim = jnp.broadcast_to(aim, (bsz, S5_COLS))
        y_s5 = _s5(u, wb, wc, are, aim, s5_d[l][None], s5_w_glu[l].astype(BF16), s5_b_glu[l][None],
                   gn_s5[l][None], tiles.s5, tiles.scan_cols)

        y_mla = _attn(q, k, v, tiles.attn)

        wo = w_out[l].astype(BF16)
        wqt = peer_w_query[l].T.astype(BF16)
        x1, h2t, cnt, e1, rk2, e2 = _mix(x, y_s5, y_mla, g1, sc2, sh2, gn_mla[l][None], wo[:S5_WIDTH],
                                         wo[S5_WIDTH:], ln1_g[l][None], ln1_b[l][None], wqt,
                                         peer_keys1[l].astype(BF16), peer_keys2[l].astype(BF16), tiles.mix)

        eblk = tiles.peer_ni * PEER_NKEYS
        vt = peer_v[l].astype(BF16).reshape(PEER_EXPERTS // eblk, eblk, dm).transpose(0, 2, 1)
        x = _peer(h2t, cnt, e1, rk2, e2, peer_u[l].astype(BF16), vt, x1, g2,
                  ln2_g[l][None], ln2_b[l][None], tiles)
    return x
```

```python
import functools
import math
from typing import NamedTuple

import jax
import jax.numpy as jnp
from jax import lax
from jax.experimental import pallas as pl
from jax.experimental.pallas import tpu as pltpu

F32 = jnp.float32
BF16 = jnp.bfloat16

D_MODEL = 1024
CHUNK = 64
S5_WIDTH = 512
S5_GROUP = 16
S5_GROUPS = 32
S5_STATE = 64
S5_COLS = S5_GROUPS * S5_STATE
MLA_WIDTH = 512
MLA_HEADS = 8
MLA_NOPE = 64
MLA_ROPE = 32
MLA_V = 64
Q_LORA = 384
KV_LORA = 256
ROPE_THETA = 10000.0
HEAD_PAD = 128
PEER_HEADS = 8
PEER_NKEYS = 128
PEER_QDIM = 256
PEER_HALF = 128
PEER_TOPK = 16
PEER_EXPERTS = PEER_NKEYS * PEER_NKEYS
DEEPNORM_ALPHA = 2.0 ** 0.25
NORM_EPS = 1e-6
MASK_VALUE = -1e30
NEG_BIG = -3.0e38

LANES = 128
SUBLANES = 8
VMEM_LIMIT = 56 * 1024 * 1024


def _gelu(x):
    return 0.5 * x * (1.0 + jnp.tanh(0.7978845608028654 * (x + 0.044715 * (x * x * x))))


def _sigmoid(x):
    return 1.0 / (1.0 + jnp.exp(-x))


class _Tiles(NamedTuple):
    proj: int
    s5: int
    attn: int
    mix: int
    peer: int
    peer_ni: int
    peer_sub: int
    peer_nq: int
    scan_cols: int


def _tiles(seq_len):
    return _Tiles(proj=min(512, seq_len), s5=min(128, seq_len), attn=min(512, seq_len), mix=min(256, seq_len),
                  peer=min(512, seq_len), peer_ni=16, peer_sub=2, peer_nq=4, scan_cols=512)


def _params(sem):
    return pltpu.CompilerParams(dimension_semantics=sem, vmem_limit_bytes=VMEM_LIMIT)


def _ada_kernel(c_ref, w_ref, b_ref, o_ref):
    c = c_ref[...]
    ca = c * _sigmoid(c)
    o_ref[...] = jnp.dot(ca, w_ref[...], preferred_element_type=F32,
                         precision=lax.Precision.HIGHEST) + b_ref[...]


def _ada(c, w_ada, b_ada):
    bsz, dm = c.shape
    n = w_ada.shape[1]
    tn = dm
    return pl.pallas_call(
        _ada_kernel,
        grid=(n // tn,),
        in_specs=[pl.BlockSpec((bsz, dm), lambda j: (0, 0)),
                  pl.BlockSpec((dm, tn), lambda j: (0, j)),
                  pl.BlockSpec((1, tn), lambda j: (0, j))],
        out_specs=pl.BlockSpec((bsz, tn), lambda j: (0, j)),
        out_shape=jax.ShapeDtypeStruct((bsz, n), F32),
        compiler_params=_params(("arbitrary",)),
        name="ada",
    )(c, w_ada, b_ada.reshape(1, n))


def _proj_kernel(x_ref, pos_ref, sc_ref, sh_ref, win_ref, wqa_ref, wqb_ref, wk_ref, wv_ref,
                 invf_ref, u_ref, q_ref, k_ref, v_ref):
    x = x_ref[0]
    h = (x * (1.0 + sc_ref[0]) + sh_ref[0]).astype(BF16)
    proj = jnp.dot(h, win_ref[...], preferred_element_type=F32)
    o = S5_WIDTH
    u_ref[0] = proj[:, :o]
    cq = proj[:, o:o + Q_LORA]
    o += Q_LORA
    ckv = proj[:, o:o + KV_LORA]
    o += KV_LORA
    kra = proj[:, o:o + HEAD_PAD]
    krb = proj[:, o + HEAD_PAD:o + 2 * HEAD_PAD]
    ang = invf_ref[...] * pos_ref[0]
    cos_t = jnp.cos(ang)
    sin_t = jnp.sin(ang)
    tt = ang.shape[1]
    cs = jnp.concatenate([jnp.ones((MLA_NOPE, tt), F32), cos_t, cos_t,
                          jnp.zeros((HEAD_PAD - MLA_NOPE - MLA_ROPE, tt), F32)], axis=0).T
    sn = jnp.concatenate([jnp.zeros((MLA_NOPE, tt), F32), -sin_t, sin_t,
                          jnp.zeros((HEAD_PAD - MLA_NOPE - MLA_ROPE, tt), F32)], axis=0).T
    cqn = (cq * lax.rsqrt(jnp.mean(cq * cq, axis=-1, keepdims=True) + NORM_EPS)).astype(BF16)
    qa = jnp.dot(cqn, wqa_ref[...], preferred_element_type=F32)
    qb = jnp.dot(cqn, wqb_ref[...], preferred_element_type=F32)
    ckvn = (ckv * lax.rsqrt(jnp.mean(ckv * ckv, axis=-1, keepdims=True) + NORM_EPS)).astype(BF16)
    kn = jnp.dot(ckvn, wk_ref[...], preferred_element_type=F32)
    kp = kra * cs + krb * sn
    for hh in range(MLA_HEADS):
        sl = slice(hh * HEAD_PAD, (hh + 1) * HEAD_PAD)
        q_ref[0, :, sl] = (qa[:, sl] * cs + qb[:, sl] * sn).astype(BF16)
        k_ref[0, :, sl] = (kn[:, sl] + kp).astype(BF16)
    v_ref[0] = lax.dot_general(wv_ref[...], ckvn, (((1,), (1,)), ((), ())),
                               preferred_element_type=F32).astype(BF16)


def _proj(x, posf, sc1, sh1, win, wqa, wqb, wk, wv, invf, tt):
    bsz, L, dm = x.shape
    hp = MLA_HEADS * HEAD_PAD
    full = lambda a: pl.BlockSpec(a.shape, lambda b, i: (0,) * a.ndim)
    return pl.pallas_call(
        _proj_kernel,
        grid=(bsz, L // tt),
        in_specs=[pl.BlockSpec((1, tt, dm), lambda b, i: (b, i, 0)),
                  pl.BlockSpec((1, 1, tt), lambda b, i: (b, 0, i)),
                  pl.BlockSpec((1, 1, dm), lambda b, i: (b, 0, 0)),
                  pl.BlockSpec((1, 1, dm), lambda b, i: (b, 0, 0)),
                  full(win), full(wqa), full(wqb), full(wk), full(wv), full(invf)],
        out_specs=[pl.BlockSpec((1, tt, S5_WIDTH), lambda b, i: (b, i, 0)),
                   pl.BlockSpec((1, tt, hp), lambda b, i: (b, i, 0)),
                   pl.BlockSpec((1, tt, hp), lambda b, i: (b, i, 0)),
                   pl.BlockSpec((1, MLA_WIDTH, tt), lambda b, i: (b, 0, i))],
        out_shape=[jax.ShapeDtypeStruct((bsz, L, S5_WIDTH), F32),
                   jax.ShapeDtypeStruct((bsz, L, hp), BF16),
                   jax.ShapeDtypeStruct((bsz, L, hp), BF16),
                   jax.ShapeDtypeStruct((bsz, MLA_WIDTH, L), BF16)],
        compiler_params=_params(("parallel", "parallel")),
        name="proj",
    )(x, posf, sc1, sh1, win, wqa, wqb, wk, wv, invf)


def _s5_kernel(u_ref, wb_ref, wc_ref, are_ref, aim_ref, d_ref, wglu_ref, bglu_ref, gn_ref,
               y_ref, utm, st, yt, carry, *, tc, bsz, cb):
    @pl.when(pl.program_id(0) == 0)
    def _():
        carry[...] = jnp.zeros_like(carry)

    nslab = S5_WIDTH // LANES
    for b in range(bsz):
        for j in range(nslab):
            utm[j, pl.ds(b, tc, stride=bsz), :] = u_ref[b, :, j * LANES:(j + 1) * LANES]
    scols = S5_COLS // nslab
    us = [utm[j] for j in range(nslab)]
    for j in range(nslab):
        bu = jnp.dot(us[j].astype(BF16), wb_ref[j], preferred_element_type=F32)
        st[:, j * scols:(j + 1) * scols] = bu[:, :scols]
        st[:, S5_COLS + j * scols:S5_COLS + (j + 1) * scols] = bu[:, scols:]

    for c0 in range(0, S5_COLS, cb):
        cre = slice(c0, c0 + cb)
        cim = slice(S5_COLS + c0, S5_COLS + c0 + cb)
        are = are_ref[:, cre]
        aim = aim_ref[:, cre]

        def step(t, sc, cre=cre, cim=cim, are=are, aim=aim):
            sre, sim = sc
            r = pl.multiple_of(t * bsz, bsz)
            nre = are * sre - aim * sim + st[pl.ds(r, bsz), cre]
            nim = are * sim + aim * sre + st[pl.ds(r, bsz), cim]
            st[pl.ds(r, bsz), cre] = nre
            st[pl.ds(r, bsz), cim] = nim
            return nre, nim

        sre, sim = lax.fori_loop(0, tc, step, (carry[:, cre], carry[:, cim]), unroll=4)
        carry[:, cre] = sre
        carry[:, cim] = sim

    ys = []
    for j in range(nslab):
        s_re = st[:, j * scols:(j + 1) * scols].astype(BF16)
        s_im = st[:, S5_COLS + j * scols:S5_COLS + (j + 1) * scols].astype(BF16)
        ys.append(jnp.dot(s_re, wc_ref[j, :scols, :], preferred_element_type=F32)
                  + jnp.dot(s_im, wc_ref[j, scols:, :], preferred_element_type=F32)
                  + d_ref[:, j * LANES:(j + 1) * LANES] * us[j])
    y = jnp.concatenate(ys, axis=-1)
    y = _gelu(y)
    z = jnp.dot(y.astype(BF16), wglu_ref[...], preferred_element_type=F32) + bglu_ref[...]
    y = y * _sigmoid(z)
    y = y * lax.rsqrt(jnp.mean(y * y, axis=-1, keepdims=True) + NORM_EPS) * gn_ref[...]
    for j in range(nslab):
        yt[j] = y[:, j * LANES:(j + 1) * LANES]
    for b in range(bsz):
        for j in range(nslab):
            y_ref[b, :, j * LANES:(j + 1) * LANES] = yt[j, pl.ds(b, tc, stride=bsz), :].astype(BF16)


def _s5(u, wb, wc, are, aim, d, wglu, bglu, gn, tc, scan_cols):
    bsz, L, w = u.shape
    full = lambda a: pl.BlockSpec(a.shape, lambda i: (0,) * a.ndim)
    return pl.pallas_call(
        functools.partial(_s5_kernel, tc=tc, bsz=bsz, cb=scan_cols),
        grid=(L // tc,),
        in_specs=[pl.BlockSpec((bsz, tc, w), lambda i: (0, i, 0)),
                  full(wb), full(wc), full(are), full(aim), full(d), full(wglu), full(bglu), full(gn)],
        out_specs=pl.BlockSpec((bsz, tc, w), lambda i: (0, i, 0)),
        out_shape=jax.ShapeDtypeStruct((bsz, L, w), BF16),
        scratch_shapes=[pltpu.VMEM((w // LANES, tc * bsz, LANES), F32),
                        pltpu.VMEM((tc * bsz, 2 * S5_COLS), F32),
                        pltpu.VMEM((w // LANES, tc * bsz, LANES), F32),
                        pltpu.VMEM((bsz, 2 * S5_COLS), F32)],
        compiler_params=_params(("arbitrary",)),
        name="s5",
    )(u, wb, wc, are, aim, d, wglu, bglu, gn)


def _attn_kernel(q_ref, k_ref, vt_ref, o_ref, sa, sb, mxa, mxb, m_s, l_s, acc_s, *, tq):
    qi = pl.program_id(2)
    nt = (((1,), (1,)), ((), ()))
    qs = [q_ref[0, :, hh * HEAD_PAD:(hh + 1) * HEAD_PAD] for hh in range(2)]

    m_s[...] = jnp.full(m_s.shape, MASK_VALUE, F32)
    l_s[...] = jnp.zeros_like(l_s)
    acc_s[...] = jnp.zeros_like(acc_s)

    def scores_into(kb, sbuf, mxbuf):
        ks = pl.multiple_of(kb * tq, tq)
        for hh in range(2):
            st = lax.dot_general(k_ref[0, pl.ds(ks, tq), hh * HEAD_PAD:(hh + 1) * HEAD_PAD], qs[hh], nt,
                                 preferred_element_type=F32)
            sbuf[hh] = st
            mxbuf[hh] = jnp.max(st, axis=0, keepdims=True)

    def softmax_from(kb, sbuf, mxbuf, masked):
        ks = pl.multiple_of(kb * tq, tq)
        for hh in range(2):
            st = sbuf[hh]
            if masked:
                kc = lax.broadcasted_iota(jnp.int32, (tq, tq), 0) // CHUNK
                qc = lax.broadcasted_iota(jnp.int32, (tq, tq), 1) // CHUNK
                st = jnp.where(kc <= qc, st, MASK_VALUE)
                mx = jnp.max(st, axis=0, keepdims=True)
            else:
                mx = mxbuf[hh]
            vt = vt_ref[0, hh * MLA_V:(hh + 1) * MLA_V, pl.ds(ks, tq)]
            m = m_s[hh]
            m_new = jnp.maximum(m, mx)
            alpha = jnp.exp2(m - m_new)
            pt = jnp.exp2(st - m_new)
            l_s[hh] = alpha * l_s[hh] + jnp.sum(pt, axis=0, keepdims=True)
            acc_s[hh] = alpha * acc_s[hh] + jnp.dot(vt, pt.astype(BF16), preferred_element_type=F32)
            m_s[hh] = m_new

    def finish():
        o_ref[0] = jnp.concatenate([acc_s[0] / l_s[0], acc_s[1] / l_s[1]], axis=0).T.astype(BF16)

    scores_into(0, sa, mxa)

    def pair(j, carry):
        scores_into(2 * j + 1, sb, mxb)
        softmax_from(2 * j, sa, mxa, False)
        scores_into(2 * j + 2, sa, mxa)
        softmax_from(2 * j + 1, sb, mxb, False)
        return carry

    lax.fori_loop(0, qi // 2, pair, 0)

    @pl.when(qi % 2 == 0)
    def _():
        softmax_from(qi, sa, mxa, True)
        finish()

    @pl.when(qi % 2 == 1)
    def _():
        scores_into(qi, sb, mxb)
        softmax_from(qi - 1, sa, mxa, False)
        softmax_from(qi, sb, mxb, True)
        finish()


def _attn(q, k, vt, tq):
    bsz, L, hp = q.shape
    return pl.pallas_call(
        functools.partial(_attn_kernel, tq=tq),
        grid=(bsz, MLA_HEADS // 2, L // tq),
        in_specs=[pl.BlockSpec((1, tq, 2 * HEAD_PAD), lambda b, h, i: (b, i, h)),
                  pl.BlockSpec((1, L, 2 * HEAD_PAD), lambda b, h, i: (b, 0, h)),
                  pl.BlockSpec((1, 2 * MLA_V, L), lambda b, h, i: (b, h, 0))],
        out_specs=pl.BlockSpec((1, tq, 2 * MLA_V), lambda b, h, i: (b, i, h)),
        out_shape=jax.ShapeDtypeStruct((bsz, L, MLA_WIDTH), BF16),
        scratch_shapes=[pltpu.VMEM((2, tq, tq), F32), pltpu.VMEM((2, tq, tq), F32),
                        pltpu.VMEM((2, 1, tq), F32), pltpu.VMEM((2, 1, tq), F32),
                        pltpu.VMEM((2, 1, tq), F32), pltpu.VMEM((2, 1, tq), F32),
                        pltpu.VMEM((2, MLA_V, tq), F32)],
        compiler_params=_params(("parallel", "parallel", "arbitrary")),
        name="attn",
    )(q, k, vt)


def _top16_rows(s):
    work = s
    ridx = lax.broadcasted_iota(jnp.int32, (PEER_TOPK, s.shape[1]), 0)
    vals = jnp.zeros((PEER_TOPK, s.shape[1]), F32)
    for r in range(PEER_TOPK):
        m = jnp.max(work, axis=0, keepdims=True)
        work = jnp.where(work == m, NEG_BIG, work)
        vals = jnp.where(ridx == r, m, vals)
    return vals


def _sort16_pairs():
    n, pairs, p = PEER_TOPK, [], 1
    while p < n:
        k = p
        while k >= 1:
            for j in range(k % p, n - k, 2 * k):
                for i in range(min(k, n - j - k)):
                    if (i + j) // (2 * p) == (i + j + k) // (2 * p):
                        pairs.append((i + j, i + j + k))
            k //= 2
        p *= 2
    return pairs


def _top16_sorted(x):
    n = PEER_TOPK
    a = [x[SUBLANES * k:SUBLANES * (k + 1)] for k in range(n)]

    def exchange(i, j):
        a[i], a[j] = jnp.maximum(a[i], a[j]), jnp.minimum(a[i], a[j])

    for i, j in _sort16_pairs():
        exchange(i, j)
    shift = SUBLANES // 2
    while shift >= 1:
        a = [jnp.maximum(a[k], pltpu.roll(a[n - 1 - k], shift, 0)) for k in range(n)]
        d = n // 2
        while d >= 1:
            for i in range(n):
                if i & d == 0:
                    exchange(i, i + d)
            d //= 2
        shift //= 2
    return a


def _rank_among_sorted(x, top):
    n = PEER_TOPK

    def pick(values, bits):
        if not bits:
            return values[0]
        half = len(values) // 2
        return jnp.where(bits[0], pick(values[half:], bits[1:]), pick(values[:half], bits[1:]))

    out = []
    for k in range(x.shape[0] // SUBLANES):
        xk = x[SUBLANES * k:SUBLANES * (k + 1)]
        bits, count, step = [], None, n // 2
        while step >= 1:
            bit = pick(top[step - 1:n - 1:2 * step], bits) > xk
            term = jnp.where(bit, float(step), 0.0)
            count = term if count is None else count + term
            bits.append(bit)
            step //= 2
        out.append(jnp.where(top[n - 1] > xk, float(n), count))
    return jnp.concatenate(out, axis=0)


def _mix_kernel(x_ref, ys_ref, ym_ref, g1_ref, sc2_ref, sh2_ref, gnm_ref, wo1_ref, wo2_ref,
                l1g_ref, l1b_ref, wq_ref, k1_ref, k2_ref,
                x1_ref, h2t_ref, cnt_ref, e1_ref, rk2_ref, e2_ref):
    x = x_ref[0]
    ym = ym_ref[0].astype(F32)
    ym = (ym * lax.rsqrt(jnp.mean(ym * ym, axis=-1, keepdims=True) + NORM_EPS) * gnm_ref[...]).astype(BF16)
    ymix = (jnp.dot(ys_ref[0], wo1_ref[...], preferred_element_type=F32)
            + jnp.dot(ym, wo2_ref[...], preferred_element_type=F32))
    r = DEEPNORM_ALPHA * x + (1.0 + g1_ref[0]) * ymix
    mu = jnp.mean(r, axis=-1, keepdims=True)
    rc = r - mu
    var = jnp.mean(rc * rc, axis=-1, keepdims=True)
    x1 = rc * lax.rsqrt(var + NORM_EPS) * l1g_ref[...] + l1b_ref[...]
    x1_ref[0] = x1
    h2 = x1 * (1.0 + sc2_ref[0]) + sh2_ref[0]
    h2t = h2.T.astype(BF16)
    h2t_ref[...] = h2t
    qt = jnp.dot(wq_ref[...], h2t, preferred_element_type=F32)
    for hh in range(PEER_HEADS):
        q1 = qt[hh * PEER_QDIM:hh * PEER_QDIM + PEER_HALF]
        q2 = qt[hh * PEER_QDIM + PEER_HALF:(hh + 1) * PEER_QDIM]
        s1 = jnp.dot(k1_ref[hh], q1.astype(BF16), preferred_element_type=F32)
        s2 = jnp.dot(k2_ref[hh], q2.astype(BF16), preferred_element_type=F32)
        r1 = _top16_sorted(s1)
        v1 = [r[0:1] for r in r1]
        r2 = _top16_sorted(s2)
        rk2 = _rank_among_sorted(s2, r2)
        sub_row = lax.broadcasted_iota(jnp.int32, r1[0].shape, 0)

        def rows_from(rs):
            out = rs[0]
            for a in range(1, SUBLANES):
                out = jnp.where(sub_row == a, rs[a], out)
            return out

        v2 = jnp.concatenate([rows_from(r2[:SUBLANES]), rows_from(r2[SUBLANES:])], axis=0)
        cands = [v1[0] + v2]
        for a in range(1, SUBLANES):
            cands.append(r1[a] + v2[0:SUBLANES])
        cands.append(rows_from(r1[SUBLANES:]) + v2[0:1])
        top = _top16_rows(jnp.concatenate(cands, axis=0))
        theta = top[PEER_TOPK - 1:PEER_TOPK]
        z = jnp.sum(jnp.exp(top - top[0:1]), axis=0, keepdims=True)
        dense_b = 4
        cnt = jnp.zeros_like(s1)
        for b in range(dense_b):
            cnt = cnt + jnp.where(s1 + v2[b:b + 1] >= theta, 1.0, 0.0)
        for a in range(3):
            extra = jnp.sum(jnp.where(v1[a] + v2[dense_b:] >= theta, 1.0, 0.0), axis=0, keepdims=True)
            cnt = cnt + jnp.where(s1 == v1[a], extra, 0.0)
        cnt_ref[hh] = jnp.where(s1 >= v1[PEER_TOPK - 1], cnt, 0.0)
        e1_ref[hh] = jnp.exp(s1 - v1[0])
        rk2_ref[hh] = rk2.astype(BF16)
        e2_ref[hh] = (jnp.exp(s2 - v2[0:1]) / z).astype(BF16)


def _mix(x, ys, ym, g1, sc2, sh2, gnm, wo1, wo2, l1g, l1b, wqt, k1, k2, tt):
    bsz, L, dm = x.shape
    n = bsz * L
    nl = L // tt
    full = lambda a: pl.BlockSpec(a.shape, lambda b, i: (0,) * a.ndim)
    tok = lambda w: pl.BlockSpec((1, tt, w), lambda b, i: (b, i, 0))
    vec = pl.BlockSpec((1, 1, dm), lambda b, i: (b, 0, 0))
    tab = pl.BlockSpec((PEER_HEADS, PEER_NKEYS, tt), lambda b, i: (0, 0, b * nl + i))
    tab_shape = jax.ShapeDtypeStruct((PEER_HEADS, PEER_NKEYS, n), F32)
    tab_bf16 = jax.ShapeDtypeStruct((PEER_HEADS, PEER_NKEYS, n), BF16)
    return pl.pallas_call(
        _mix_kernel,
        grid=(bsz, nl),
        in_specs=[tok(dm), tok(S5_WIDTH), tok(MLA_WIDTH), vec, vec, vec,
                  full(gnm), full(wo1), full(wo2), full(l1g), full(l1b), full(wqt), full(k1), full(k2)],
        out_specs=[tok(dm), pl.BlockSpec((dm, tt), lambda b, i: (0, b * nl + i)), tab, tab, tab, tab],
        out_shape=[jax.ShapeDtypeStruct((bsz, L, dm), F32), jax.ShapeDtypeStruct((dm, n), BF16),
                   tab_shape, tab_shape, tab_bf16, tab_bf16],
        compiler_params=_params(("parallel", "parallel")),
        name="mix",
    )(x, ys, ym, g1, sc2, sh2, gnm, wo1, wo2, l1g, l1b, wqt, k1, k2)


def _peer_kernel(h2t_ref, cnt_ref, e1_ref, rk2_ref, e2_ref, u_ref, vt_ref, x1_ref, g2_ref, l2g_ref, l2b_ref,
                 o_ref, acc, wa0, wa1, *, ni, sub, nq):
    eb = pl.program_id(1)
    ne = pl.num_programs(1) - 1
    dm = acc.shape[0]

    @pl.when(eb == 0)
    def _():
        acc[...] = jnp.zeros_like(acc)
        wa1[...] = jnp.zeros_like(wa1)

    def consume(wa_r, q, nq):
        rows = slice(q * (dm // nq), (q + 1) * (dm // nq))
        acc[rows, :] += jnp.dot(vt_ref[0, rows, :], wa_r[...], preferred_element_type=F32)

    def step(wa_w, wa_r):
        sc = None
        for ii in range(ni):
            i = eb * ni + ii
            if ii % sub == 0:
                sc = jnp.dot(u_ref[ii * PEER_NKEYS:(ii + sub) * PEER_NKEYS, :], h2t_ref[...],
                             preferred_element_type=F32)
            if ii % (ni // nq) == 0:
                consume(wa_r, ii // (ni // nq), nq)
            rows = slice((ii % sub) * PEER_NKEYS, (ii % sub + 1) * PEER_NKEYS)
            w = None
            for hh in range(PEER_HEADS):
                c = cnt_ref[hh, pl.ds(i, 1), :].astype(BF16)
                e = e1_ref[hh, pl.ds(i, 1), :].astype(BF16)
                term = jnp.where(rk2_ref[hh] < c, e2_ref[hh], jnp.zeros((), BF16)) * e
                w = term if w is None else w + term
            wa_w[ii * PEER_NKEYS:(ii + 1) * PEER_NKEYS, :] = w * _gelu(sc[rows].astype(BF16))

    @pl.when((eb % 2 == 0) & (eb < ne))
    def _():
        step(wa0, wa1)

    @pl.when(eb % 2 == 1)
    def _():
        step(wa1, wa0)

    @pl.when(eb == ne)
    def _():
        consume(wa1, 0, 1)
        yff = acc[...].T
        r = DEEPNORM_ALPHA * x1_ref[0] + (1.0 + g2_ref[0]) * yff
        mu = jnp.mean(r, axis=-1, keepdims=True)
        rc = r - mu
        var = jnp.mean(rc * rc, axis=-1, keepdims=True)
        o_ref[0] = rc * lax.rsqrt(var + NORM_EPS) * l2g_ref[...] + l2b_ref[...]


def _peer(h2t, cnt, e1, rk2, e2, ub, vt, x1, g2, l2g, l2b, tiles):
    bsz, L, dm = x1.shape
    tt, ni = tiles.peer, tiles.peer_ni
    nl = L // tt
    eb = ni * PEER_NKEYS
    tab = pl.BlockSpec((PEER_HEADS, PEER_NKEYS, tt), lambda t, e: (0, 0, t))
    full = lambda a: pl.BlockSpec(a.shape, lambda t, e: (0,) * a.ndim)
    ne = PEER_NKEYS // ni
    assert ne % 2 == 0 and ni % tiles.peer_sub == 0 and ni % tiles.peer_nq == 0
    return pl.pallas_call(
        functools.partial(_peer_kernel, ni=ni, sub=tiles.peer_sub, nq=tiles.peer_nq),
        grid=(bsz * nl, ne + 1),
        in_specs=[pl.BlockSpec((dm, tt), lambda t, e: (0, t)), tab, tab, tab, tab,
                  pl.BlockSpec((eb, dm), lambda t, e: (jnp.minimum(e, ne - 1), 0)),
                  pl.BlockSpec((1, dm, eb), lambda t, e: (jnp.maximum(e - 1, 0), 0, 0)),
                  pl.BlockSpec((1, tt, dm), lambda t, e: (t // nl, t % nl, 0)),
                  pl.BlockSpec((1, 1, dm), lambda t, e: (t // nl, 0, 0)),
                  full(l2g), full(l2b)],
        out_specs=pl.BlockSpec((1, tt, dm), lambda t, e: (t // nl, t % nl, 0)),
        out_shape=jax.ShapeDtypeStruct((bsz, L, dm), F32),
        scratch_shapes=[pltpu.VMEM((dm, tt), F32), pltpu.VMEM((eb, tt), BF16), pltpu.VMEM((eb, tt), BF16)],
        compiler_params=_params(("parallel", "arbitrary")),
        name="peer",
    )(h2t, cnt, e1, rk2, e2, ub, vt, x1, g2, l2g, l2b)


def _s5_discretize(lam_re, lam_im, log_dt, b_re, b_im, c_re, c_im):
    dt = jnp.exp(log_dt)[:, None]
    mag = jnp.exp(lam_re * dt)
    ab_re = mag * jnp.cos(lam_im * dt)
    ab_im = mag * jnp.sin(lam_im * dt)
    den = lam_re * lam_re + lam_im * lam_im
    nr = ab_re - 1.0
    coef_re = (nr * lam_re + ab_im * lam_im) / den
    coef_im = (ab_im * lam_re - nr * lam_im) / den
    bb_re = coef_re[..., None] * b_re - coef_im[..., None] * b_im
    bb_im = coef_re[..., None] * b_im + coef_im[..., None] * b_re
    nslab = S5_WIDTH // LANES
    gs = S5_GROUPS // nslab
    eye = jnp.eye(gs, dtype=F32)
    slab = lambda a: a.reshape((nslab, gs) + a.shape[1:])
    wb_re = jnp.einsum("sgph,gk->sghkp", slab(bb_re), eye).reshape(nslab, LANES, gs * S5_STATE)
    wb_im = jnp.einsum("sgph,gk->sghkp", slab(bb_im), eye).reshape(nslab, LANES, gs * S5_STATE)
    wb = jnp.concatenate([wb_re, wb_im], axis=2).astype(BF16)
    wc_re = jnp.einsum("sghp,gk->sgpkh", slab(c_re), eye).reshape(nslab, gs * S5_STATE, LANES)
    wc_im = jnp.einsum("sghp,gk->sgpkh", slab(c_im), eye).reshape(nslab, gs * S5_STATE, LANES)
    wc = jnp.concatenate([wc_re, -wc_im], axis=1).astype(BF16)
    return wb, wc, ab_re.reshape(1, S5_COLS), ab_im.reshape(1, S5_COLS)


def _pad_heads(w, width, offset):
    kdim = w.shape[0]
    w = w.reshape(kdim, MLA_HEADS, width)
    out = jnp.zeros((kdim, MLA_HEADS, HEAD_PAD), w.dtype)
    out = out.at[:, :, offset:offset + width].set(w)
    return out.reshape(kdim, MLA_HEADS * HEAD_PAD)


def _swap_halves(w):
    half = w.shape[-1] // 2
    return jnp.concatenate([w[..., half:], w[..., :half]], axis=-1)


def kernel(x, c, positions, w_ada, b_ada, w_in, s5_lambda_re, s5_lambda_im, s5_log_dt, s5_b_re, s5_b_im, s5_c_re, s5_c_im, s5_d, s5_w_glu, s5_b_glu, mla_q_norm, mla_w_uq, mla_kv_norm, mla_w_ukv, gn_s5, gn_mla, w_out, ln1_g, ln1_b, peer_w_query, peer_keys1, peer_keys2, peer_u, peer_v, ln2_g, ln2_b):
    bsz, L, dm = x.shape
    depth = w_ada.shape[0]
    tiles = _tiles(L)

    invf = (ROPE_THETA ** (-jnp.arange(0, MLA_ROPE, 2, dtype=F32) / MLA_ROPE))[:, None]
    posf = positions.astype(F32)[:, None, :]

    for l in range(depth):
        mod = _ada(c, w_ada[l], b_ada[l])
        sh1, sc1, g1, sh2, sc2, g2 = [m[:, None, :] for m in jnp.split(mod, 6, axis=-1)]

        wi = w_in[l]
        o1, o2, o3 = S5_WIDTH, S5_WIDTH + Q_LORA, S5_WIDTH + Q_LORA + KV_LORA
        w_kr = wi[:, o3:]
        zpad = lambda w: jnp.zeros((dm, HEAD_PAD), F32).at[:, MLA_NOPE:MLA_NOPE + MLA_ROPE].set(w)
        win = jnp.concatenate([wi[:, :o3], zpad(w_kr), zpad(_swap_halves(w_kr))], axis=1).astype(BF16)
        scale = (MLA_NOPE + MLA_ROPE) ** -0.5 * math.log2(math.e)
        wuq = (mla_w_uq[l] * mla_q_norm[l][:, None] * scale).reshape(Q_LORA, MLA_HEADS, MLA_NOPE + MLA_ROPE)
        wuq_main = wuq.reshape(Q_LORA, MLA_HEADS * (MLA_NOPE + MLA_ROPE))
        wqa = _pad_heads(wuq_main, MLA_NOPE + MLA_ROPE, 0).astype(BF16)
        wuq_sw = _swap_halves(wuq[:, :, MLA_NOPE:]).reshape(Q_LORA, MLA_HEADS * MLA_ROPE)
        wqb = _pad_heads(wuq_sw, MLA_ROPE, MLA_NOPE).astype(BF16)
        wukv = (mla_w_ukv[l] * mla_kv_norm[l][:, None]).reshape(KV_LORA, MLA_HEADS, MLA_NOPE + MLA_V)
        wk = _pad_heads(wukv[:, :, :MLA_NOPE].reshape(KV_LORA, MLA_HEADS * MLA_NOPE), MLA_NOPE, 0).astype(BF16)
        wv = wukv[:, :, MLA_NOPE:].reshape(KV_LORA, MLA_HEADS * MLA_V).T.astype(BF16)

        u, q, k, v = _proj(x, posf, sc1, sh1, win, wqa, wqb, wk, wv, invf, tiles.proj)

        wb, wc, are, aim = _s5_discretize(s5_lambda_re[l], s5_lambda_im[l], s5_log_dt[l], s5_b_re[l],
                                          s5_b_im[l], s5_c_re[l], s5_c_im[l])
        are = jnp.broadcast_to(are, (bsz, S5_COLS))
        aim = jnp.broadcast_to(aim, (bsz, S5_COLS))
        y_s5 = _s5(u, wb, wc, are, aim, s5_d[l][None], s5_w_glu[l].astype(BF16), s5_b_glu[l][None],
                   gn_s5[l][None], tiles.s5, tiles.scan_cols)

        y_mla = _attn(q, k, v, tiles.attn)

        wo = w_out[l].astype(BF16)
        wqt = peer_w_query[l].T.astype(BF16)
        x1, h2t, cnt, e1, rk2, e2 = _mix(x, y_s5, y_mla, g1, sc2, sh2, gn_mla[l][None], wo[:S5_WIDTH],
                                         wo[S5_WIDTH:], ln1_g[l][None], ln1_b[l][None], wqt,
                                         peer_keys1[l].astype(BF16), peer_keys2[l].astype(BF16), tiles.mix)

        eblk = tiles.peer_ni * PEER_NKEYS
        vt = peer_v[l].astype(BF16).reshape(PEER_EXPERTS // eblk, eblk, dm).transpose(0, 2, 1)
        x = _peer(h2t, cnt, e1, rk2, e2, peer_u[l].astype(BF16), vt, x1, g2,
                  ln2_g[l][None], ln2_b[l][None], tiles)
    return x
```

```python
import functools
import math
from typing import NamedTuple

import jax
import jax.numpy as jnp
from jax import lax
from jax.experimental import pallas as pl
from jax.experimental.pallas import tpu as pltpu

F32 = jnp.float32
BF16 = jnp.bfloat16

D_MODEL = 1024
CHUNK = 64
S5_WIDTH = 512
S5_GROUP = 16
S5_GROUPS = 32
S5_STATE = 64
S5_COLS = S5_GROUPS * S5_STATE
MLA_WIDTH = 512
MLA_HEADS = 8
MLA_NOPE = 64
MLA_ROPE = 32
MLA_V = 64
Q_LORA = 384
KV_LORA = 256
ROPE_THETA = 10000.0
HEAD_PAD = 128
PEER_HEADS = 8
PEER_NKEYS = 128
PEER_QDIM = 256
PEER_HALF = 128
PEER_TOPK = 16
PEER_EXPERTS = PEER_NKEYS * PEER_NKEYS
DEEPNORM_ALPHA = 2.0 ** 0.25
NORM_EPS = 1e-6
MASK_VALUE = -1e30
NEG_BIG = -3.0e38

LANES = 128
SUBLANES = 8
VMEM_LIMIT = 56 * 1024 * 1024


def _gelu(x):
    return 0.5 * x * (1.0 + jnp.tanh(0.7978845608028654 * (x + 0.044715 * (x * x * x))))


def _sigmoid(x):
    return 1.0 / (1.0 + jnp.exp(-x))


class _Tiles(NamedTuple):
    proj: int
    s5: int
    attn: int
    mix: int
    peer: int
    peer_ni: int
    peer_sub: int
    peer_nq: int
    scan_cols: int


def _tiles(seq_len):
    return _Tiles(proj=min(512, seq_len), s5=min(128, seq_len), attn=min(512, seq_len), mix=min(256, seq_len),
                  peer=min(512, seq_len), peer_ni=16, peer_sub=2, peer_nq=4, scan_cols=512)


def _params(sem):
    return pltpu.CompilerParams(dimension_semantics=sem, vmem_limit_bytes=VMEM_LIMIT)


def _ada_kernel(c_ref, w_ref, b_ref, o_ref):
    c = c_ref[...]
    ca = c * _sigmoid(c)
    o_ref[...] = jnp.dot(ca, w_ref[...], preferred_element_type=F32,
                         precision=lax.Precision.HIGHEST) + b_ref[...]


def _ada(c, w_ada, b_ada):
    bsz, dm = c.shape
    n = w_ada.shape[1]
    tn = dm
    return pl.pallas_call(
        _ada_kernel,
        grid=(n // tn,),
        in_specs=[pl.BlockSpec((bsz, dm), lambda j: (0, 0)),
                  pl.BlockSpec((dm, tn), lambda j: (0, j)),
                  pl.BlockSpec((1, tn), lambda j: (0, j))],
        out_specs=pl.BlockSpec((bsz, tn), lambda j: (0, j)),
        out_shape=jax.ShapeDtypeStruct((bsz, n), F32),
        compiler_params=_params(("arbitrary",)),
        name="ada",
    )(c, w_ada, b_ada.reshape(1, n))


def _proj_kernel(x_ref, pos_ref, sc_ref, sh_ref, win_ref, wqa_ref, wqb_ref, wk_ref, wv_ref,
                 invf_ref, u_ref, q_ref, k_ref, v_ref):
    x = x_ref[0]
    h = (x * (1.0 + sc_ref[0]) + sh_ref[0]).astype(BF16)
    proj = jnp.dot(h, win_ref[...], preferred_element_type=F32)
    o = S5_WIDTH
    u_ref[0] = proj[:, :o]
    cq = proj[:, o:o + Q_LORA]
    o += Q_LORA
    ckv = proj[:, o:o + KV_LORA]
    o += KV_LORA
    kra = proj[:, o:o + HEAD_PAD]
    krb = proj[:, o + HEAD_PAD:o + 2 * HEAD_PAD]
    ang = invf_ref[...] * pos_ref[0]
    cos_t = jnp.cos(ang)
    sin_t = jnp.sin(ang)
    tt = ang.shape[1]
    cs = jnp.concatenate([jnp.ones((MLA_NOPE, tt), F32), cos_t, cos_t,
                          jnp.zeros((HEAD_PAD - MLA_NOPE - MLA_ROPE, tt), F32)], axis=0).T
    sn = jnp.concatenate([jnp.zeros((MLA_NOPE, tt), F32), -sin_t, sin_t,
                          jnp.zeros((HEAD_PAD - MLA_NOPE - MLA_ROPE, tt), F32)], axis=0).T
    cqn = (cq * lax.rsqrt(jnp.mean(cq * cq, axis=-1, keepdims=True) + NORM_EPS)).astype(BF16)
    qa = jnp.dot(cqn, wqa_ref[...], preferred_element_type=F32)
    qb = jnp.dot(cqn, wqb_ref[...], preferred_element_type=F32)
    ckvn = (ckv * lax.rsqrt(jnp.mean(ckv * ckv, axis=-1, keepdims=True) + NORM_EPS)).astype(BF16)
    kn = jnp.dot(ckvn, wk_ref[...], preferred_element_type=F32)
    kp = kra * cs + krb * sn
    for hh in range(MLA_HEADS):
        sl = slice(hh * HEAD_PAD, (hh + 1) * HEAD_PAD)
        q_ref[0, :, sl] = (qa[:, sl] * cs + qb[:, sl] * sn).astype(BF16)
        k_ref[0, :, sl] = (kn[:, sl] + kp).astype(BF16)
    v_ref[0] = lax.dot_general(wv_ref[...], ckvn, (((1,), (1,)), ((), ())),
                               preferred_element_type=F32).astype(BF16)


def _proj(x, posf, sc1, sh1, win, wqa, wqb, wk, wv, invf, tt):
    bsz, L, dm = x.shape
    hp = MLA_HEADS * HEAD_PAD
    full = lambda a: pl.BlockSpec(a.shape, lambda b, i: (0,) * a.ndim)
    return pl.pallas_call(
        _proj_kernel,
        grid=(bsz, L // tt),
        in_specs=[pl.BlockSpec((1, tt, dm), lambda b, i: (b, i, 0)),
                  pl.BlockSpec((1, 1, tt), lambda b, i: (b, 0, i)),
                  pl.BlockSpec((1, 1, dm), lambda b, i: (b, 0, 0)),
                  pl.BlockSpec((1, 1, dm), lambda b, i: (b, 0, 0)),
                  full(win), full(wqa), full(wqb), full(wk), full(wv), full(invf)],
        out_specs=[pl.BlockSpec((1, tt, S5_WIDTH), lambda b, i: (b, i, 0)),
                   pl.BlockSpec((1, tt, hp), lambda b, i: (b, i, 0)),
                   pl.BlockSpec((1, tt, hp), lambda b, i: (b, i, 0)),
                   pl.BlockSpec((1, MLA_WIDTH, tt), lambda b, i: (b, 0, i))],
        out_shape=[jax.ShapeDtypeStruct((bsz, L, S5_WIDTH), F32),
                   jax.ShapeDtypeStruct((bsz, L, hp), BF16),
                   jax.ShapeDtypeStruct((bsz, L, hp), BF16),
                   jax.ShapeDtypeStruct((bsz, MLA_WIDTH, L), BF16)],
        compiler_params=_params(("parallel", "parallel")),
        name="proj",
    )(x, posf, sc1, sh1, win, wqa, wqb, wk, wv, invf)


def _s5_kernel(u_ref, wb_ref, wc_ref, are_ref, aim_ref, d_ref, wglu_ref, bglu_ref, gn_ref,
               y_ref, utm, st, yt, carry, *, tc, bsz, cb):
    @pl.when(pl.program_id(0) == 0)
    def _():
        carry[...] = jnp.zeros_like(carry)

    nslab = S5_WIDTH // LANES
    for b in range(bsz):
        for j in range(nslab):
            utm[j, pl.ds(b, tc, stride=bsz), :] = u_ref[b, :, j * LANES:(j + 1) * LANES]
    scols = S5_COLS // nslab
    us = [utm[j] for j in range(nslab)]
    for j in range(nslab):
        bu = jnp.dot(us[j].astype(BF16), wb_ref[j], preferred_element_type=F32)
        st[:, j * scols:(j + 1) * scols] = bu[:, :scols]
        st[:, S5_COLS + j * scols:S5_COLS + (j + 1) * scols] = bu[:, scols:]

    for c0 in range(0, S5_COLS, cb):
        cre = slice(c0, c0 + cb)
        cim = slice(S5_COLS + c0, S5_COLS + c0 + cb)
        are = are_ref[:, cre]
        aim = aim_ref[:, cre]

        def step(t, sc, cre=cre, cim=cim, are=are, aim=aim):
            sre, sim = sc
            r = pl.multiple_of(t * bsz, bsz)
            nre = are * sre - aim * sim + st[pl.ds(r, bsz), cre]
            nim = are * sim + aim * sre + st[pl.ds(r, bsz), cim]
            st[pl.ds(r, bsz), cre] = nre
            st[pl.ds(r, bsz), cim] = nim
            return nre, nim

        sre, sim = lax.fori_loop(0, tc, step, (carry[:, cre], carry[:, cim]), unroll=4)
        carry[:, cre] = sre
        carry[:, cim] = sim

    ys = []
    for j in range(nslab):
        s_re = st[:, j * scols:(j + 1) * scols].astype(BF16)
        s_im = st[:, S5_COLS + j * scols:S5_COLS + (j + 1) * scols].astype(BF16)
        ys.append(jnp.dot(s_re, wc_ref[j, :scols, :], preferred_element_type=F32)
                  + jnp.dot(s_im, wc_ref[j, scols:, :], preferred_element_type=F32)
                  + d_ref[:, j * LANES:(j + 1) * LANES] * us[j])
    y = jnp.concatenate(ys, axis=-1)
    y = _gelu(y)
    z = jnp.dot(y.astype(BF16), wglu_ref[...], preferred_element_type=F32) + bglu_ref[...]
    y = y * _sigmoid(z)
    y = y * lax.rsqrt(jnp.mean(y * y, axis=-1, keepdims=True) + NORM_EPS) * gn_ref[...]
    for j in range(nslab):
        yt[j] = y[:, j * LANES:(j + 1) * LANES]
    for b in range(bsz):
        for j in range(nslab):
            y_ref[b, :, j * LANES:(j + 1) * LANES] = yt[j, pl.ds(b, tc, stride=bsz), :].astype(BF16)


def _s5(u, wb, wc, are, aim, d, wglu, bglu, gn, tc, scan_cols):
    bsz, L, w = u.shape
    full = lambda a: pl.BlockSpec(a.shape, lambda i: (0,) * a.ndim)
    return pl.pallas_call(
        functools.partial(_s5_kernel, tc=tc, bsz=bsz, cb=scan_cols),
        grid=(L // tc,),
        in_specs=[pl.BlockSpec((bsz, tc, w), lambda i: (0, i, 0)),
                  full(wb), full(wc), full(are), full(aim), full(d), full(wglu), full(bglu), full(gn)],
        out_specs=pl.BlockSpec((bsz, tc, w), lambda i: (0, i, 0)),
        out_shape=jax.ShapeDtypeStruct((bsz, L, w), BF16),
        scratch_shapes=[pltpu.VMEM((w // LANES, tc * bsz, LANES), F32),
                        pltpu.VMEM((tc * bsz, 2 * S5_COLS), F32),
                        pltpu.VMEM((w // LANES, tc * bsz, LANES), F32),
                        pltpu.VMEM((bsz, 2 * S5_COLS), F32)],
        compiler_params=_params(("arbitrary",)),
        name="s5",
    )(u, wb, wc, are, aim, d, wglu, bglu, gn)


def _attn_kernel(q_ref, k_ref, vt_ref, o_ref, sa, sb, mxa, mxb, m_s, l_s, acc_s, *, tq):
    qi = pl.program_id(2)
    nt = (((1,), (1,)), ((), ()))
    qs = [q_ref[0, :, hh * HEAD_PAD:(hh + 1) * HEAD_PAD] for hh in range(2)]

    m_s[...] = jnp.full(m_s.shape, MASK_VALUE, F32)
    l_s[...] = jnp.zeros_like(l_s)
    acc_s[...] = jnp.zeros_like(acc_s)

    def scores_into(kb, sbuf, mxbuf):
        ks = pl.multiple_of(kb * tq, tq)
        for hh in range(2):
            st = lax.dot_general(k_ref[0, pl.ds(ks, tq), hh * HEAD_PAD:(hh + 1) * HEAD_PAD], qs[hh], nt,
                                 preferred_element_type=F32)
            sbuf[hh] = st
            mxbuf[hh] = jnp.max(st, axis=0, keepdims=True)

    def softmax_from(kb, sbuf, mxbuf, masked):
        ks = pl.multiple_of(kb * tq, tq)
        for hh in range(2):
            st = sbuf[hh]
            if masked:
                kc = lax.broadcasted_iota(jnp.int32, (tq, tq), 0) // CHUNK
                qc = lax.broadcasted_iota(jnp.int32, (tq, tq), 1) // CHUNK
                st = jnp.where(kc <= qc, st, MASK_VALUE)
                mx = jnp.max(st, axis=0, keepdims=True)
            else:
                mx = mxbuf[hh]
            vt = vt_ref[0, hh * MLA_V:(hh + 1) * MLA_V, pl.ds(ks, tq)]
            m = m_s[hh]
            m_new = jnp.maximum(m, mx)
            alpha = jnp.exp2(m - m_new)
            pt = jnp.exp2(st - m_new)
            l_s[hh] = alpha * l_s[hh] + jnp.sum(pt, axis=0, keepdims=True)
            acc_s[hh] = alpha * acc_s[hh] + jnp.dot(vt, pt.astype(BF16), preferred_element_type=F32)
            m_s[hh] = m_new

    def finish():
        o_ref[0] = jnp.concatenate([acc_s[0] / l_s[0], acc_s[1] / l_s[1]], axis=0).T.astype(BF16)

    scores_into(0, sa, mxa)

    def pair(j, carry):
        scores_into(2 * j + 1, sb, mxb)
        softmax_from(2 * j, sa, mxa, False)
        scores_into(2 * j + 2, sa, mxa)
        softmax_from(2 * j + 1, sb, mxb, False)
        return carry

    lax.fori_loop(0, qi // 2, pair, 0)

    @pl.when(qi % 2 == 0)
    def _():
        softmax_from(qi, sa, mxa, True)
        finish()

    @pl.when(qi % 2 == 1)
    def _():
        scores_into(qi, sb, mxb)
        softmax_from(qi - 1, sa, mxa, False)
        softmax_from(qi, sb, mxb, True)
        finish()


def _attn(q, k, vt, tq):
    bsz, L, hp = q.shape
    return pl.pallas_call(
        functools.partial(_attn_kernel, tq=tq),
        grid=(bsz, MLA_HEADS // 2, L // tq),
        in_specs=[pl.BlockSpec((1, tq, 2 * HEAD_PAD), lambda b, h, i: (b, i, h)),
                  pl.BlockSpec((1, L, 2 * HEAD_PAD), lambda b, h, i: (b, 0, h)),
                  pl.BlockSpec((1, 2 * MLA_V, L), lambda b, h, i: (b, h, 0))],
        out_specs=pl.BlockSpec((1, tq, 2 * MLA_V), lambda b, h, i: (b, i, h)),
        out_shape=jax.ShapeDtypeStruct((bsz, L, MLA_WIDTH), BF16),
        scratch_shapes=[pltpu.VMEM((2, tq, tq), F32), pltpu.VMEM((2, tq, tq), F32),
                        pltpu.VMEM((2, 1, tq), F32), pltpu.VMEM((2, 1, tq), F32),
                        pltpu.VMEM((2, 1, tq), F32), pltpu.VMEM((2, 1, tq), F32),
                        pltpu.VMEM((2, MLA_V, tq), F32)],
        compiler_params=_params(("parallel", "parallel", "arbitrary")),
        name="attn",
    )(q, k, vt)


def _top16_rows(s):
    work = s
    ridx = lax.broadcasted_iota(jnp.int32, (PEER_TOPK, s.shape[1]), 0)
    vals = jnp.zeros((PEER_TOPK, s.shape[1]), F32)
    for r in range(PEER_TOPK):
        m = jnp.max(work, axis=0, keepdims=True)
        work = jnp.where(work == m, NEG_BIG, work)
        vals = jnp.where(ridx == r, m, vals)
    return vals


def _sort16_pairs():
    n, pairs, p = PEER_TOPK, [], 1
    while p < n:
        k = p
        while k >= 1:
            for j in range(k % p, n - k, 2 * k):
                for i in range(min(k, n - j - k)):
                    if (i + j) // (2 * p) == (i + j + k) // (2 * p):
                        pairs.append((i + j, i + j + k))
            k //= 2
        p *= 2
    return pairs


def _top16_sorted(x):
    n = PEER_TOPK
    a = [x[SUBLANES * k:SUBLANES * (k + 1)] for k in range(n)]

    def exchange(i, j):
        a[i], a[j] = jnp.maximum(a[i], a[j]), jnp.minimum(a[i], a[j])

    for i, j in _sort16_pairs():
        exchange(i, j)
    shift = SUBLANES // 2
    while shift >= 1:
        a = [jnp.maximum(a[k], pltpu.roll(a[n - 1 - k], shift, 0)) for k in range(n)]
        d = n // 2
        while d >= 1:
            for i in range(n):
                if i & d == 0:
                    exchange(i, i + d)
            d //= 2
        shift //= 2
    return a


def _rank_among_sorted(x, top):
    n = PEER_TOPK

    def pick(values, bits):
        if not bits:
            return values[0]
        half = len(values) // 2
        return jnp.where(bits[0], pick(values[half:], bits[1:]), pick(values[:half], bits[1:]))

    out = []
    for k in range(x.shape[0] // SUBLANES):
        xk = x[SUBLANES * k:SUBLANES * (k + 1)]
        bits, count, step = [], None, n // 2
        while step >= 1:
            bit = pick(top[step - 1:n - 1:2 * step], bits) > xk
            term = jnp.where(bit, float(step), 0.0)
            count = term if count is None else count + term
            bits.append(bit)
            step //= 2
        out.append(jnp.where(top[n - 1] > xk, float(n), count))
    return jnp.concatenate(out, axis=0)


def _mix_kernel(x_ref, ys_ref, ym_ref, g1_ref, sc2_ref, sh2_ref, gnm_ref, wo1_ref, wo2_ref,
                l1g_ref, l1b_ref, wq_ref, k1_ref, k2_ref,
                x1_ref, h2t_ref, cnt_ref, e1_ref, rk2_ref, e2_ref):
    x = x_ref[0]
    ym = ym_ref[0].astype(F32)
    ym = (ym * lax.rsqrt(jnp.mean(ym * ym, axis=-1, keepdims=True) + NORM_EPS) * gnm_ref[...]).astype(BF16)
    ymix = (jnp.dot(ys_ref[0], wo1_ref[...], preferred_element_type=F32)
            + jnp.dot(ym, wo2_ref[...], preferred_element_type=F32))
    r = DEEPNORM_ALPHA * x + (1.0 + g1_ref[0]) * ymix
    mu = jnp.mean(r, axis=-1, keepdims=True)
    rc = r - mu
    var = jnp.mean(rc * rc, axis=-1, keepdims=True)
    x1 = rc * lax.rsqrt(var + NORM_EPS) * l1g_ref[...] + l1b_ref[...]
    x1_ref[0] = x1
    h2 = x1 * (1.0 + sc2_ref[0]) + sh2_ref[0]
    h2t = h2.T.astype(BF16)
    h2t_ref[...] = h2t
    qt = jnp.dot(wq_ref[...], h2t, preferred_element_type=F32)
    for hh in range(PEER_HEADS):
        q1 = qt[hh * PEER_QDIM:hh * PEER_QDIM + PEER_HALF]
        q2 = qt[hh * PEER_QDIM + PEER_HALF:(hh + 1) * PEER_QDIM]
        s1 = jnp.dot(k1_ref[hh], q1.astype(BF16), preferred_element_type=F32)
        s2 = jnp.dot(k2_ref[hh], q2.astype(BF16), preferred_element_type=F32)
        r1 = _top16_sorted(s1)
        v1 = [r[0:1] for r in r1]
        r2 = _top16_sorted(s2)
        rk2 = _rank_among_sorted(s2, r2)
        sub_row = lax.broadcasted_iota(jnp.int32, r1[0].shape, 0)

        def rows_from(rs):
            out = rs[0]
            for a in range(1, SUBLANES):
                out = jnp.where(sub_row == a, rs[a], out)
            return out

        v2 = jnp.concatenate([rows_from(r2[:SUBLANES]), rows_from(r2[SUBLANES:])], axis=0)
        cands = [v1[0] + v2]
        for a in range(1, SUBLANES):
            cands.append(r1[a] + v2[0:SUBLANES])
        cands.append(rows_from(r1[SUBLANES:]) + v2[0:1])
        top = _top16_rows(jnp.concatenate(cands, axis=0))
        theta = top[PEER_TOPK - 1:PEER_TOPK]
        z = jnp.sum(jnp.exp(top - top[0:1]), axis=0, keepdims=True)
        dense_b = 4
        cnt = jnp.zeros_like(s1)
        for b in range(dense_b):
            cnt = cnt + jnp.where(s1 + v2[b:b + 1] >= theta, 1.0, 0.0)
        for a in range(3):
            extra = jnp.sum(jnp.where(v1[a] + v2[dense_b:] >= theta, 1.0, 0.0), axis=0, keepdims=True)
            cnt = cnt + jnp.where(s1 == v1[a], extra, 0.0)
        cnt_ref[hh] = jnp.where(s1 >= v1[PEER_TOPK - 1], cnt, 0.0).astype(BF16)
        e1_ref[hh] = jnp.exp(s1 - v1[0]).astype(BF16)
        rk2_ref[hh] = rk2.astype(BF16)
        e2_ref[hh] = (jnp.exp(s2 - v2[0:1]) / z).astype(BF16)


def _mix(x, ys, ym, g1, sc2, sh2, gnm, wo1, wo2, l1g, l1b, wqt, k1, k2, tt):
    bsz, L, dm = x.shape
    n = bsz * L
    nl = L // tt
    full = lambda a: pl.BlockSpec(a.shape, lambda b, i: (0,) * a.ndim)
    tok = lambda w: pl.BlockSpec((1, tt, w), lambda b, i: (b, i, 0))
    vec = pl.BlockSpec((1, 1, dm), lambda b, i: (b, 0, 0))
    tab = pl.BlockSpec((PEER_HEADS, PEER_NKEYS, tt), lambda b, i: (0, 0, b * nl + i))
    tab_bf16 = jax.ShapeDtypeStruct((PEER_HEADS, PEER_NKEYS, n), BF16)
    return pl.pallas_call(
        _mix_kernel,
        grid=(bsz, nl),
        in_specs=[tok(dm), tok(S5_WIDTH), tok(MLA_WIDTH), vec, vec, vec,
                  full(gnm), full(wo1), full(wo2), full(l1g), full(l1b), full(wqt), full(k1), full(k2)],
        out_specs=[tok(dm), pl.BlockSpec((dm, tt), lambda b, i: (0, b * nl + i)), tab, tab, tab, tab],
        out_shape=[jax.ShapeDtypeStruct((bsz, L, dm), F32), jax.ShapeDtypeStruct((dm, n), BF16),
                   tab_bf16, tab_bf16, tab_bf16, tab_bf16],
        compiler_params=_params(("parallel", "parallel")),
        name="mix",
    )(x, ys, ym, g1, sc2, sh2, gnm, wo1, wo2, l1g, l1b, wqt, k1, k2)


def _peer_kernel(h2t_ref, cnt_ref, e1_ref, rk2_ref, e2_ref, u_ref, vt_ref, x1_ref, g2_ref, l2g_ref, l2b_ref,
                 o_ref, acc, wa0, wa1, *, ni, sub, nq):
    eb = pl.program_id(1)
    ne = pl.num_programs(1) - 1
    dm = acc.shape[0]

    @pl.when(eb == 0)
    def _():
        acc[...] = jnp.zeros_like(acc)
        wa1[...] = jnp.zeros_like(wa1)

    def consume(wa_r, q, nq):
        rows = slice(q * (dm // nq), (q + 1) * (dm // nq))
        acc[rows, :] += jnp.dot(vt_ref[0, rows, :], wa_r[...], preferred_element_type=F32)

    def step(wa_w, wa_r):
        sc = None
        base = pl.multiple_of(eb * ni, ni)
        ctile = [cnt_ref[hh, pl.ds(base, ni), :] for hh in range(PEER_HEADS)]
        etile = [e1_ref[hh, pl.ds(base, ni), :] for hh in range(PEER_HEADS)]
        for ii in range(ni):
            if ii % sub == 0:
                sc = jnp.dot(u_ref[ii * PEER_NKEYS:(ii + sub) * PEER_NKEYS, :], h2t_ref[...],
                             preferred_element_type=F32)
            if ii % (ni // nq) == 0:
                consume(wa_r, ii // (ni // nq), nq)
            rows = slice((ii % sub) * PEER_NKEYS, (ii % sub + 1) * PEER_NKEYS)
            w = None
            for hh in range(PEER_HEADS):
                c = ctile[hh][ii:ii + 1]
                e = etile[hh][ii:ii + 1]
                term = jnp.where(rk2_ref[hh] < c, e2_ref[hh], jnp.zeros((), BF16)) * e
                w = term if w is None else w + term
            wa_w[ii * PEER_NKEYS:(ii + 1) * PEER_NKEYS, :] = w * _gelu(sc[rows].astype(BF16))

    @pl.when((eb % 2 == 0) & (eb < ne))
    def _():
        step(wa0, wa1)

    @pl.when(eb % 2 == 1)
    def _():
        step(wa1, wa0)

    @pl.when(eb == ne)
    def _():
        consume(wa1, 0, 1)
        yff = acc[...].T
        r = DEEPNORM_ALPHA * x1_ref[0] + (1.0 + g2_ref[0]) * yff
        mu = jnp.mean(r, axis=-1, keepdims=True)
        rc = r - mu
        var = jnp.mean(rc * rc, axis=-1, keepdims=True)
        o_ref[0] = rc * lax.rsqrt(var + NORM_EPS) * l2g_ref[...] + l2b_ref[...]


def _peer(h2t, cnt, e1, rk2, e2, ub, vt, x1, g2, l2g, l2b, tiles):
    bsz, L, dm = x1.shape
    tt, ni = tiles.peer, tiles.peer_ni
    nl = L // tt
    eb = ni * PEER_NKEYS
    tab = pl.BlockSpec((PEER_HEADS, PEER_NKEYS, tt), lambda t, e: (0, 0, t))
    full = lambda a: pl.BlockSpec(a.shape, lambda t, e: (0,) * a.ndim)
    ne = PEER_NKEYS // ni
    assert ne % 2 == 0 and ni % tiles.peer_sub == 0 and ni % tiles.peer_nq == 0
    assert ni % (2 * SUBLANES) == 0
    return pl.pallas_call(
        functools.partial(_peer_kernel, ni=ni, sub=tiles.peer_sub, nq=tiles.peer_nq),
        grid=(bsz * nl, ne + 1),
        in_specs=[pl.BlockSpec((dm, tt), lambda t, e: (0, t)), tab, tab, tab, tab,
                  pl.BlockSpec((eb, dm), lambda t, e: (jnp.minimum(e, ne - 1), 0)),
                  pl.BlockSpec((1, dm, eb), lambda t, e: (jnp.maximum(e - 1, 0), 0, 0)),
                  pl.BlockSpec((1, tt, dm), lambda t, e: (t // nl, t % nl, 0)),
                  pl.BlockSpec((1, 1, dm), lambda t, e: (t // nl, 0, 0)),
                  full(l2g), full(l2b)],
        out_specs=pl.BlockSpec((1, tt, dm), lambda t, e: (t // nl, t % nl, 0)),
        out_shape=jax.ShapeDtypeStruct((bsz, L, dm), F32),
        scratch_shapes=[pltpu.VMEM((dm, tt), F32), pltpu.VMEM((eb, tt), BF16), pltpu.VMEM((eb, tt), BF16)],
        compiler_params=_params(("parallel", "arbitrary")),
        name="peer",
    )(h2t, cnt, e1, rk2, e2, ub, vt, x1, g2, l2g, l2b)


def _s5_discretize(lam_re, lam_im, log_dt, b_re, b_im, c_re, c_im):
    dt = jnp.exp(log_dt)[:, None]
    mag = jnp.exp(lam_re * dt)
    ab_re = mag * jnp.cos(lam_im * dt)
    ab_im = mag * jnp.sin(lam_im * dt)
    den = lam_re * lam_re + lam_im * lam_im
    nr = ab_re - 1.0
    coef_re = (nr * lam_re + ab_im * lam_im) / den
    coef_im = (ab_im * lam_re - nr * lam_im) / den
    bb_re = coef_re[..., None] * b_re - coef_im[..., None] * b_im
    bb_im = coef_re[..., None] * b_im + coef_im[..., None] * b_re
    nslab = S5_WIDTH // LANES
    gs = S5_GROUPS // nslab
    eye = jnp.eye(gs, dtype=F32)
    slab = lambda a: a.reshape((nslab, gs) + a.shape[1:])
    wb_re = jnp.einsum("sgph,gk->sghkp", slab(bb_re), eye).reshape(nslab, LANES, gs * S5_STATE)
    wb_im = jnp.einsum("sgph,gk->sghkp", slab(bb_im), eye).reshape(nslab, LANES, gs * S5_STATE)
    wb = jnp.concatenate([wb_re, wb_im], axis=2).astype(BF16)
    wc_re = jnp.einsum("sghp,gk->sgpkh", slab(c_re), eye).reshape(nslab, gs * S5_STATE, LANES)
    wc_im = jnp.einsum("sghp,gk->sgpkh", slab(c_im), eye).reshape(nslab, gs * S5_STATE, LANES)
    wc = jnp.concatenate([wc_re, -wc_im], axis=1).astype(BF16)
    return wb, wc, ab_re.reshape(1, S5_COLS), ab_im.reshape(1, S5_COLS)


def _pad_heads(w, width, offset):
    kdim = w.shape[0]
    w = w.reshape(kdim, MLA_HEADS, width)
    out = jnp.zeros((kdim, MLA_HEADS, HEAD_PAD), w.dtype)
    out = out.at[:, :, offset:offset + width].set(w)
    return out.reshape(kdim, MLA_HEADS * HEAD_PAD)


def _swap_halves(w):
    half = w.shape[-1] // 2
    return jnp.concatenate([w[..., half:], w[..., :half]], axis=-1)


def kernel(x, c, positions, w_ada, b_ada, w_in, s5_lambda_re, s5_lambda_im, s5_log_dt, s5_b_re, s5_b_im, s5_c_re, s5_c_im, s5_d, s5_w_glu, s5_b_glu, mla_q_norm, mla_w_uq, mla_kv_norm, mla_w_ukv, gn_s5, gn_mla, w_out, ln1_g, ln1_b, peer_w_query, peer_keys1, peer_keys2, peer_u, peer_v, ln2_g, ln2_b):
    bsz, L, dm = x.shape
    depth = w_ada.shape[0]
    tiles = _tiles(L)

    invf = (ROPE_THETA ** (-jnp.arange(0, MLA_ROPE, 2, dtype=F32) / MLA_ROPE))[:, None]
    posf = positions.astype(F32)[:, None, :]

    for l in range(depth):
        mod = _ada(c, w_ada[l], b_ada[l])
        sh1, sc1, g1, sh2, sc2, g2 = [m[:, None, :] for m in jnp.split(mod, 6, axis=-1)]

        wi = w_in[l]
        o1, o2, o3 = S5_WIDTH, S5_WIDTH + Q_LORA, S5_WIDTH + Q_LORA + KV_LORA
        w_kr = wi[:, o3:]
        zpad = lambda w: jnp.zeros((dm, HEAD_PAD), F32).at[:, MLA_NOPE:MLA_NOPE + MLA_ROPE].set(w)
        win = jnp.concatenate([wi[:, :o3], zpad(w_kr), zpad(_swap_halves(w_kr))], axis=1).astype(BF16)
        scale = (MLA_NOPE + MLA_ROPE) ** -0.5 * math.log2(math.e)
        wuq = (mla_w_uq[l] * mla_q_norm[l][:, None] * scale).reshape(Q_LORA, MLA_HEADS, MLA_NOPE + MLA_ROPE)
        wuq_main = wuq.reshape(Q_LORA, MLA_HEADS * (MLA_NOPE + MLA_ROPE))
        wqa = _pad_heads(wuq_main, MLA_NOPE + MLA_ROPE, 0).astype(BF16)
        wuq_sw = _swap_halves(wuq[:, :, MLA_NOPE:]).reshape(Q_LORA, MLA_HEADS * MLA_ROPE)
        wqb = _pad_heads(wuq_sw, MLA_ROPE, MLA_NOPE).astype(BF16)
        wukv = (mla_w_ukv[l] * mla_kv_norm[l][:, None]).reshape(KV_LORA, MLA_HEADS, MLA_NOPE + MLA_V)
        wk = _pad_heads(wukv[:, :, :MLA_NOPE].reshape(KV_LORA, MLA_HEADS * MLA_NOPE), MLA_NOPE, 0).astype(BF16)
        wv = wukv[:, :, MLA_NOPE:].reshape(KV_LORA, MLA_HEADS * MLA_V).T.astype(BF16)

        u, q, k, v = _proj(x, posf, sc1, sh1, win, wqa, wqb, wk, wv, invf, tiles.proj)

        wb, wc, are, aim = _s5_discretize(s5_lambda_re[l], s5_lambda_im[l], s5_log_dt[l], s5_b_re[l],
                                          s5_b_im[l], s5_c_re[l], s5_c_im[l])
        are = jnp.broadcast_to(are, (bsz, S5_COLS))
        aim = jnp.broadcast_to(aim, (bsz, S5_COLS))
        y_s5 = _s5(u, wb, wc, are, aim, s5_d[l][None], s5_w_glu[l].astype(BF16), s5_b_glu[l][None],
                   gn_s5[l][None], tiles.s5, tiles.scan_cols)

        y_mla = _attn(q, k, v, tiles.attn)

        wo = w_out[l].astype(BF16)
        wqt = peer_w_query[l].T.astype(BF16)
        x1, h2t, cnt, e1, rk2, e2 = _mix(x, y_s5, y_mla, g1, sc2, sh2, gn_mla[l][None], wo[:S5_WIDTH],
                                         wo[S5_WIDTH:], ln1_g[l][None], ln1_b[l][None], wqt,
                                         peer_keys1[l].astype(BF16), peer_keys2[l].astype(BF16), tiles.mix)

        eblk = tiles.peer_ni * PEER_NKEYS
        vt = peer_v[l].astype(BF16).reshape(PEER_EXPERTS // eblk, eblk, dm).transpose(0, 2, 1)
        x = _peer(h2t, cnt, e1, rk2, e2, peer_u[l].astype(BF16), vt, x1, g2,
                  ln2_g[l][None], ln2_b[l][None], tiles)
    return x
```

```python
import functools
import math
from typing import NamedTuple

import jax
import jax.numpy as jnp
from jax import lax
from jax.experimental import pallas as pl
from jax.experimental.pallas import tpu as pltpu

F32 = jnp.float32
BF16 = jnp.bfloat16

D_MODEL = 1024
CHUNK = 64
S5_WIDTH = 512
S5_GROUP = 16
S5_GROUPS = 32
S5_STATE = 64
S5_COLS = S5_GROUPS * S5_STATE
MLA_WIDTH = 512
MLA_HEADS = 8
MLA_NOPE = 64
MLA_ROPE = 32
MLA_V = 64
Q_LORA = 384
KV_LORA = 256
ROPE_THETA = 10000.0
HEAD_PAD = 128
PEER_HEADS = 8
PEER_NKEYS = 128
PEER_QDIM = 256
PEER_HALF = 128
PEER_TOPK = 16
PEER_EXPERTS = PEER_NKEYS * PEER_NKEYS
DEEPNORM_ALPHA = 2.0 ** 0.25
NORM_EPS = 1e-6
MASK_VALUE = -1e30
NEG_BIG = -3.0e38

LANES = 128
SUBLANES = 8
VMEM_LIMIT = 56 * 1024 * 1024


def _gelu(x):
    return 0.5 * x * (1.0 + jnp.tanh(0.7978845608028654 * (x + 0.044715 * (x * x * x))))


def _sigmoid(x):
    return 1.0 / (1.0 + jnp.exp(-x))


class _Tiles(NamedTuple):
    proj: int
    s5: int
    attn: int
    mix: int
    peer: int
    peer_ni: int
    peer_sub: int
    peer_nq: int
    scan_cols: int


def _tiles(seq_len):
    return _Tiles(proj=min(512, seq_len), s5=min(128, seq_len), attn=min(512, seq_len), mix=min(256, seq_len),
                  peer=min(512, seq_len), peer_ni=16, peer_sub=2, peer_nq=4, scan_cols=512)


def _params(sem):
    return pltpu.CompilerParams(dimension_semantics=sem, vmem_limit_bytes=VMEM_LIMIT)


def _ada_kernel(c_ref, w_ref, b_ref, o_ref):
    c = c_ref[...]
    ca = c * _sigmoid(c)
    o_ref[...] = jnp.dot(ca, w_ref[...], preferred_element_type=F32,
                         precision=lax.Precision.HIGHEST) + b_ref[...]


def _ada(c, w_ada, b_ada):
    bsz, dm = c.shape
    n = w_ada.shape[1]
    tn = dm
    return pl.pallas_call(
        _ada_kernel,
        grid=(n // tn,),
        in_specs=[pl.BlockSpec((bsz, dm), lambda j: (0, 0)),
                  pl.BlockSpec((dm, tn), lambda j: (0, j)),
                  pl.BlockSpec((1, tn), lambda j: (0, j))],
        out_specs=pl.BlockSpec((bsz, tn), lambda j: (0, j)),
        out_shape=jax.ShapeDtypeStruct((bsz, n), F32),
        compiler_params=_params(("arbitrary",)),
        name="ada",
    )(c, w_ada, b_ada.reshape(1, n))


def _proj_kernel(x_ref, pos_ref, sc_ref, sh_ref, win_ref, wqa_ref, wqb_ref, wk_ref, wv_ref,
                 invf_ref, u_ref, q_ref, k_ref, v_ref):
    x = x_ref[0]
    h = (x * (1.0 + sc_ref[0]) + sh_ref[0]).astype(BF16)
    proj = jnp.dot(h, win_ref[...], preferred_element_type=F32)
    o = S5_WIDTH
    u_ref[0] = proj[:, :o]
    cq = proj[:, o:o + Q_LORA]
    o += Q_LORA
    ckv = proj[:, o:o + KV_LORA]
    o += KV_LORA
    kra = proj[:, o:o + HEAD_PAD]
    krb = proj[:, o + HEAD_PAD:o + 2 * HEAD_PAD]
    ang = invf_ref[...] * pos_ref[0]
    cos_t = jnp.cos(ang)
    sin_t = jnp.sin(ang)
    tt = ang.shape[1]
    cs = jnp.concatenate([jnp.ones((MLA_NOPE, tt), F32), cos_t, cos_t,
                          jnp.zeros((HEAD_PAD - MLA_NOPE - MLA_ROPE, tt), F32)], axis=0).T
    sn = jnp.concatenate([jnp.zeros((MLA_NOPE, tt), F32), -sin_t, sin_t,
                          jnp.zeros((HEAD_PAD - MLA_NOPE - MLA_ROPE, tt), F32)], axis=0).T
    cqn = (cq * lax.rsqrt(jnp.mean(cq * cq, axis=-1, keepdims=True) + NORM_EPS)).astype(BF16)
    qa = jnp.dot(cqn, wqa_ref[...], preferred_element_type=F32)
    qb = jnp.dot(cqn, wqb_ref[...], preferred_element_type=F32)
    ckvn = (ckv * lax.rsqrt(jnp.mean(ckv * ckv, axis=-1, keepdims=True) + NORM_EPS)).astype(BF16)
    kn = jnp.dot(ckvn, wk_ref[...], preferred_element_type=F32)
    kp = kra * cs + krb * sn
    for hh in range(MLA_HEADS):
        sl = slice(hh * HEAD_PAD, (hh + 1) * HEAD_PAD)
        q_ref[0, :, sl] = (qa[:, sl] * cs + qb[:, sl] * sn).astype(BF16)
        k_ref[0, :, sl] = (kn[:, sl] + kp).astype(BF16)
    v_ref[0] = lax.dot_general(wv_ref[...], ckvn, (((1,), (1,)), ((), ())),
                               preferred_element_type=F32).astype(BF16)


def _proj(x, posf, sc1, sh1, win, wqa, wqb, wk, wv, invf, tt):
    bsz, L, dm = x.shape
    hp = MLA_HEADS * HEAD_PAD
    full = lambda a: pl.BlockSpec(a.shape, lambda b, i: (0,) * a.ndim)
    return pl.pallas_call(
        _proj_kernel,
        grid=(bsz, L // tt),
        in_specs=[pl.BlockSpec((1, tt, dm), lambda b, i: (b, i, 0)),
                  pl.BlockSpec((1, 1, tt), lambda b, i: (b, 0, i)),
                  pl.BlockSpec((1, 1, dm), lambda b, i: (b, 0, 0)),
                  pl.BlockSpec((1, 1, dm), lambda b, i: (b, 0, 0)),
                  full(win), full(wqa), full(wqb), full(wk), full(wv), full(invf)],
        out_specs=[pl.BlockSpec((1, tt, S5_WIDTH), lambda b, i: (b, i, 0)),
                   pl.BlockSpec((1, tt, hp), lambda b, i: (b, i, 0)),
                   pl.BlockSpec((1, tt, hp), lambda b, i: (b, i, 0)),
                   pl.BlockSpec((1, MLA_WIDTH, tt), lambda b, i: (b, 0, i))],
        out_shape=[jax.ShapeDtypeStruct((bsz, L, S5_WIDTH), F32),
                   jax.ShapeDtypeStruct((bsz, L, hp), BF16),
                   jax.ShapeDtypeStruct((bsz, L, hp), BF16),
                   jax.ShapeDtypeStruct((bsz, MLA_WIDTH, L), BF16)],
        compiler_params=_params(("parallel", "parallel")),
        name="proj",
    )(x, posf, sc1, sh1, win, wqa, wqb, wk, wv, invf)


def _s5_kernel(u_ref, wb_ref, wc_ref, are_ref, aim_ref, d_ref, wglu_ref, bglu_ref, gn_ref,
               y_ref, utm, st, yt, carry, *, tc, bsz, cb):
    @pl.when(pl.program_id(0) == 0)
    def _():
        carry[...] = jnp.zeros_like(carry)

    nslab = S5_WIDTH // LANES
    for b in range(bsz):
        for j in range(nslab):
            utm[j, pl.ds(b, tc, stride=bsz), :] = u_ref[b, :, j * LANES:(j + 1) * LANES]
    scols = S5_COLS // nslab
    us = [utm[j] for j in range(nslab)]
    for j in range(nslab):
        bu = jnp.dot(us[j].astype(BF16), wb_ref[j], preferred_element_type=F32)
        st[:, j * scols:(j + 1) * scols] = bu[:, :scols]
        st[:, S5_COLS + j * scols:S5_COLS + (j + 1) * scols] = bu[:, scols:]

    for c0 in range(0, S5_COLS, cb):
        cre = slice(c0, c0 + cb)
        cim = slice(S5_COLS + c0, S5_COLS + c0 + cb)
        are = are_ref[:, cre]
        aim = aim_ref[:, cre]

        def step(t, sc, cre=cre, cim=cim, are=are, aim=aim):
            sre, sim = sc
            r = pl.multiple_of(t * bsz, bsz)
            nre = are * sre - aim * sim + st[pl.ds(r, bsz), cre]
            nim = are * sim + aim * sre + st[pl.ds(r, bsz), cim]
            st[pl.ds(r, bsz), cre] = nre
            st[pl.ds(r, bsz), cim] = nim
            return nre, nim

        sre, sim = lax.fori_loop(0, tc, step, (carry[:, cre], carry[:, cim]), unroll=4)
        carry[:, cre] = sre
        carry[:, cim] = sim

    ys = []
    for j in range(nslab):
        s_re = st[:, j * scols:(j + 1) * scols].astype(BF16)
        s_im = st[:, S5_COLS + j * scols:S5_COLS + (j + 1) * scols].astype(BF16)
        ys.append(jnp.dot(s_re, wc_ref[j, :scols, :], preferred_element_type=F32)
                  + jnp.dot(s_im, wc_ref[j, scols:, :], preferred_element_type=F32)
                  + d_ref[:, j * LANES:(j + 1) * LANES] * us[j])
    y = jnp.concatenate(ys, axis=-1)
    y = _gelu(y)
    z = jnp.dot(y.astype(BF16), wglu_ref[...], preferred_element_type=F32) + bglu_ref[...]
    y = y * _sigmoid(z)
    y = y * lax.rsqrt(jnp.mean(y * y, axis=-1, keepdims=True) + NORM_EPS) * gn_ref[...]
    for j in range(nslab):
        yt[j] = y[:, j * LANES:(j + 1) * LANES]
    for b in range(bsz):
        for j in range(nslab):
            y_ref[b, :, j * LANES:(j + 1) * LANES] = yt[j, pl.ds(b, tc, stride=bsz), :].astype(BF16)


def _s5(u, wb, wc, are, aim, d, wglu, bglu, gn, tc, scan_cols):
    bsz, L, w = u.shape
    full = lambda a: pl.BlockSpec(a.shape, lambda i: (0,) * a.ndim)
    return pl.pallas_call(
        functools.partial(_s5_kernel, tc=tc, bsz=bsz, cb=scan_cols),
        grid=(L // tc,),
        in_specs=[pl.BlockSpec((bsz, tc, w), lambda i: (0, i, 0)),
                  full(wb), full(wc), full(are), full(aim), full(d), full(wglu), full(bglu), full(gn)],
        out_specs=pl.BlockSpec((bsz, tc, w), lambda i: (0, i, 0)),
        out_shape=jax.ShapeDtypeStruct((bsz, L, w), BF16),
        scratch_shapes=[pltpu.VMEM((w // LANES, tc * bsz, LANES), F32),
                        pltpu.VMEM((tc * bsz, 2 * S5_COLS), F32),
                        pltpu.VMEM((w // LANES, tc * bsz, LANES), F32),
                        pltpu.VMEM((bsz, 2 * S5_COLS), F32)],
        compiler_params=_params(("arbitrary",)),
        name="s5",
    )(u, wb, wc, are, aim, d, wglu, bglu, gn)


def _attn_kernel(q_ref, k_ref, vt_ref, o_ref, sa, sb, mxa, mxb, m_s, l_s, acc_s, *, tq):
    qi = pl.program_id(2)
    nt = (((1,), (1,)), ((), ()))
    qs = [q_ref[0, :, hh * HEAD_PAD:(hh + 1) * HEAD_PAD] for hh in range(2)]

    m_s[...] = jnp.full(m_s.shape, MASK_VALUE, F32)
    l_s[...] = jnp.zeros_like(l_s)
    acc_s[...] = jnp.zeros_like(acc_s)

    def scores_into(kb, sbuf, mxbuf):
        ks = pl.multiple_of(kb * tq, tq)
        for hh in range(2):
            st = lax.dot_general(k_ref[0, pl.ds(ks, tq), hh * HEAD_PAD:(hh + 1) * HEAD_PAD], qs[hh], nt,
                                 preferred_element_type=F32)
            sbuf[hh] = st
            mxbuf[hh] = jnp.max(st, axis=0, keepdims=True)

    def softmax_from(kb, sbuf, mxbuf, masked):
        ks = pl.multiple_of(kb * tq, tq)
        for hh in range(2):
            st = sbuf[hh]
            if masked:
                kc = lax.broadcasted_iota(jnp.int32, (tq, tq), 0) // CHUNK
                qc = lax.broadcasted_iota(jnp.int32, (tq, tq), 1) // CHUNK
                st = jnp.where(kc <= qc, st, MASK_VALUE)
                mx = jnp.max(st, axis=0, keepdims=True)
            else:
                mx = mxbuf[hh]
            vt = vt_ref[0, hh * MLA_V:(hh + 1) * MLA_V, pl.ds(ks, tq)]
            m = m_s[hh]
            m_new = jnp.maximum(m, mx)
            alpha = jnp.exp2(m - m_new)
            pt = jnp.exp2(st - m_new)
            l_s[hh] = alpha * l_s[hh] + jnp.sum(pt, axis=0, keepdims=True)
            acc_s[hh] = alpha * acc_s[hh] + jnp.dot(vt, pt.astype(BF16), preferred_element_type=F32)
            m_s[hh] = m_new

    def finish():
        o_ref[0] = jnp.concatenate([acc_s[0] / l_s[0], acc_s[1] / l_s[1]], axis=0).T.astype(BF16)

    scores_into(0, sa, mxa)

    def pair(j, carry):
        scores_into(2 * j + 1, sb, mxb)
        softmax_from(2 * j, sa, mxa, False)
        scores_into(2 * j + 2, sa, mxa)
        softmax_from(2 * j + 1, sb, mxb, False)
        return carry

    lax.fori_loop(0, qi // 2, pair, 0)

    @pl.when(qi % 2 == 0)
    def _():
        softmax_from(qi, sa, mxa, True)
        finish()

    @pl.when(qi % 2 == 1)
    def _():
        scores_into(qi, sb, mxb)
        softmax_from(qi - 1, sa, mxa, False)
        softmax_from(qi, sb, mxb, True)
        finish()


def _attn(q, k, vt, tq):
    bsz, L, hp = q.shape
    return pl.pallas_call(
        functools.partial(_attn_kernel, tq=tq),
        grid=(bsz, MLA_HEADS // 2, L // tq),
        in_specs=[pl.BlockSpec((1, tq, 2 * HEAD_PAD), lambda b, h, i: (b, i, h)),
                  pl.BlockSpec((1, L, 2 * HEAD_PAD), lambda b, h, i: (b, 0, h)),
                  pl.BlockSpec((1, 2 * MLA_V, L), lambda b, h, i: (b, h, 0))],
        out_specs=pl.BlockSpec((1, tq, 2 * MLA_V), lambda b, h, i: (b, i, h)),
        out_shape=jax.ShapeDtypeStruct((bsz, L, MLA_WIDTH), BF16),
        scratch_shapes=[pltpu.VMEM((2, tq, tq), F32), pltpu.VMEM((2, tq, tq), F32),
                        pltpu.VMEM((2, 1, tq), F32), pltpu.VMEM((2, 1, tq), F32),
                        pltpu.VMEM((2, 1, tq), F32), pltpu.VMEM((2, 1, tq), F32),
                        pltpu.VMEM((2, MLA_V, tq), F32)],
        compiler_params=_params(("parallel", "parallel", "arbitrary")),
        name="attn",
    )(q, k, vt)


def _top16_rows(s):
    work = s
    ridx = lax.broadcasted_iota(jnp.int32, (PEER_TOPK, s.shape[1]), 0)
    vals = jnp.zeros((PEER_TOPK, s.shape[1]), F32)
    for r in range(PEER_TOPK):
        m = jnp.max(work, axis=0, keepdims=True)
        work = jnp.where(work == m, NEG_BIG, work)
        vals = jnp.where(ridx == r, m, vals)
    return vals


def _sort16_pairs():
    n, pairs, p = PEER_TOPK, [], 1
    while p < n:
        k = p
        while k >= 1:
            for j in range(k % p, n - k, 2 * k):
                for i in range(min(k, n - j - k)):
                    if (i + j) // (2 * p) == (i + j + k) // (2 * p):
                        pairs.append((i + j, i + j + k))
            k //= 2
        p *= 2
    return pairs


def _top16_sorted(x):
    n = PEER_TOPK
    a = [x[SUBLANES * k:SUBLANES * (k + 1)] for k in range(n)]

    def exchange(i, j):
        a[i], a[j] = jnp.maximum(a[i], a[j]), jnp.minimum(a[i], a[j])

    for i, j in _sort16_pairs():
        exchange(i, j)
    shift = SUBLANES // 2
    while shift >= 1:
        a = [jnp.maximum(a[k], pltpu.roll(a[n - 1 - k], shift, 0)) for k in range(n)]
        d = n // 2
        while d >= 1:
            for i in range(n):
                if i & d == 0:
                    exchange(i, i + d)
            d //= 2
        shift //= 2
    return a


def _rank_among_sorted(x, top):
    n = PEER_TOPK

    def pick(values, bits):
        if not bits:
            return values[0]
        half = len(values) // 2
        return jnp.where(bits[0], pick(values[half:], bits[1:]), pick(values[:half], bits[1:]))

    out = []
    for k in range(x.shape[0] // SUBLANES):
        xk = x[SUBLANES * k:SUBLANES * (k + 1)]
        bits, count, step = [], None, n // 2
        while step >= 1:
            bit = pick(top[step - 1:n - 1:2 * step], bits) > xk
            term = jnp.where(bit, float(step), 0.0)
            count = term if count is None else count + term
            bits.append(bit)
            step //= 2
        out.append(jnp.where(top[n - 1] > xk, float(n), count))
    return jnp.concatenate(out, axis=0)


def _mix_kernel(x_ref, ys_ref, ym_ref, g1_ref, sc2_ref, sh2_ref, gnm_ref, wo1_ref, wo2_ref,
                l1g_ref, l1b_ref, wq_ref, k1_ref, k2_ref,
                x1_ref, h2t_ref, cnt_ref, e1_ref, rk2_ref, e2_ref):
    x = x_ref[0]
    ym = ym_ref[0].astype(F32)
    ym = (ym * lax.rsqrt(jnp.mean(ym * ym, axis=-1, keepdims=True) + NORM_EPS) * gnm_ref[...]).astype(BF16)
    ymix = (jnp.dot(ys_ref[0], wo1_ref[...], preferred_element_type=F32)
            + jnp.dot(ym, wo2_ref[...], preferred_element_type=F32))
    r = DEEPNORM_ALPHA * x + (1.0 + g1_ref[0]) * ymix
    mu = jnp.mean(r, axis=-1, keepdims=True)
    rc = r - mu
    var = jnp.mean(rc * rc, axis=-1, keepdims=True)
    x1 = rc * lax.rsqrt(var + NORM_EPS) * l1g_ref[...] + l1b_ref[...]
    x1_ref[0] = x1
    h2 = x1 * (1.0 + sc2_ref[0]) + sh2_ref[0]
    h2t = h2.T.astype(BF16)
    h2t_ref[...] = h2t
    qt = jnp.dot(wq_ref[...], h2t, preferred_element_type=F32)
    for hh in range(PEER_HEADS):
        q1 = qt[hh * PEER_QDIM:hh * PEER_QDIM + PEER_HALF]
        q2 = qt[hh * PEER_QDIM + PEER_HALF:(hh + 1) * PEER_QDIM]
        s1 = jnp.dot(k1_ref[hh], q1.astype(BF16), preferred_element_type=F32)
        s2 = jnp.dot(k2_ref[hh], q2.astype(BF16), preferred_element_type=F32)
        r1 = _top16_sorted(s1)
        v1 = [r[0:1] for r in r1]
        r2 = _top16_sorted(s2)
        rk2 = _rank_among_sorted(s2, r2)
        sub_row = lax.broadcasted_iota(jnp.int32, r1[0].shape, 0)

        def rows_from(rs):
            out = rs[0]
            for a in range(1, SUBLANES):
                out = jnp.where(sub_row == a, rs[a], out)
            return out

        v2 = jnp.concatenate([rows_from(r2[:SUBLANES]), rows_from(r2[SUBLANES:])], axis=0)
        cands = [v1[0] + v2]
        for a in range(1, SUBLANES):
            cands.append(r1[a] + v2[0:SUBLANES])
        cands.append(rows_from(r1[SUBLANES:]) + v2[0:1])
        top = _top16_rows(jnp.concatenate(cands, axis=0))
        theta = top[PEER_TOPK - 1:PEER_TOPK]
        z = jnp.sum(jnp.exp(top - top[0:1]), axis=0, keepdims=True)
        dense_b = 4
        cnt = jnp.zeros_like(s1)
        for b in range(dense_b):
            cnt = cnt + jnp.where(s1 + v2[b:b + 1] >= theta, 1.0, 0.0)
        for a in range(3):
            extra = jnp.sum(jnp.where(v1[a] + v2[dense_b:] >= theta, 1.0, 0.0), axis=0, keepdims=True)
            cnt = cnt + jnp.where(s1 == v1[a], extra, 0.0)
        cnt_ref[hh] = jnp.where(s1 >= v1[PEER_TOPK - 1], cnt, 0.0).astype(BF16)
        e1_ref[hh] = jnp.exp(s1 - v1[0]).astype(BF16)
        rk2_ref[hh] = rk2.astype(BF16)
        e2_ref[hh] = (jnp.exp(s2 - v2[0:1]) / z).astype(BF16)


def _mix(x, ys, ym, g1, sc2, sh2, gnm, wo1, wo2, l1g, l1b, wqt, k1, k2, tt):
    bsz, L, dm = x.shape
    n = bsz * L
    nl = L // tt
    full = lambda a: pl.BlockSpec(a.shape, lambda b, i: (0,) * a.ndim)
    tok = lambda w: pl.BlockSpec((1, tt, w), lambda b, i: (b, i, 0))
    vec = pl.BlockSpec((1, 1, dm), lambda b, i: (b, 0, 0))
    tab = pl.BlockSpec((PEER_HEADS, PEER_NKEYS, tt), lambda b, i: (0, 0, b * nl + i))
    tab_bf16 = jax.ShapeDtypeStruct((PEER_HEADS, PEER_NKEYS, n), BF16)
    return pl.pallas_call(
        _mix_kernel,
        grid=(bsz, nl),
        in_specs=[tok(dm), tok(S5_WIDTH), tok(MLA_WIDTH), vec, vec, vec,
                  full(gnm), full(wo1), full(wo2), full(l1g), full(l1b), full(wqt), full(k1), full(k2)],
        out_specs=[tok(dm), pl.BlockSpec((dm, tt), lambda b, i: (0, b * nl + i)), tab, tab, tab, tab],
        out_shape=[jax.ShapeDtypeStruct((bsz, L, dm), F32), jax.ShapeDtypeStruct((dm, n), BF16),
                   tab_bf16, tab_bf16, tab_bf16, tab_bf16],
        compiler_params=_params(("parallel", "parallel")),
        name="mix",
    )(x, ys, ym, g1, sc2, sh2, gnm, wo1, wo2, l1g, l1b, wqt, k1, k2)


def _peer_kernel(h2t_ref, cnt_ref, e1_ref, rk2_ref, e2_ref, u_ref, vt_ref, x1_ref, g2_ref, l2g_ref, l2b_ref,
                 o_ref, acc, wa0, wa1, *, ni, sub, nq):
    eb = pl.program_id(1)
    ne = pl.num_programs(1) - 1
    dm = acc.shape[0]

    def consume(wa_r, q, nq):
        rows = slice(q * (dm // nq), (q + 1) * (dm // nq))
        acc[rows, :] += jnp.dot(vt_ref[0, rows, :], wa_r[...], preferred_element_type=F32)

    def step(wa_w, wa_r):
        sc = None
        base = pl.multiple_of(eb * ni, ni)
        ctile = [cnt_ref[hh, pl.ds(base, ni), :] for hh in range(PEER_HEADS)]
        etile = [e1_ref[hh, pl.ds(base, ni), :] for hh in range(PEER_HEADS)]
        for ii in range(ni):
            if ii % sub == 0:
                sc = jnp.dot(u_ref[ii * PEER_NKEYS:(ii + sub) * PEER_NKEYS, :], h2t_ref[...],
                             preferred_element_type=F32)
            if wa_r is not None and ii % (ni // nq) == 0:
                consume(wa_r, ii // (ni // nq), nq)
            rows = slice((ii % sub) * PEER_NKEYS, (ii % sub + 1) * PEER_NKEYS)
            w = None
            for hh in range(PEER_HEADS):
                c = ctile[hh][ii:ii + 1]
                e = etile[hh][ii:ii + 1]
                term = jnp.where(rk2_ref[hh] < c, e2_ref[hh], jnp.zeros((), BF16)) * e
                w = term if w is None else w + term
            wa_w[ii * PEER_NKEYS:(ii + 1) * PEER_NKEYS, :] = w * _gelu(sc[rows].astype(BF16))

    @pl.when(eb == 0)
    def _():
        acc[...] = jnp.zeros_like(acc)
        step(wa0, None)

    @pl.when((eb % 2 == 0) & (eb > 0) & (eb < ne))
    def _():
        step(wa0, wa1)

    @pl.when(eb % 2 == 1)
    def _():
        step(wa1, wa0)

    @pl.when(eb == ne)
    def _():
        consume(wa1, 0, 1)
        yff = acc[...].T
        r = DEEPNORM_ALPHA * x1_ref[0] + (1.0 + g2_ref[0]) * yff
        mu = jnp.mean(r, axis=-1, keepdims=True)
        rc = r - mu
        var = jnp.mean(rc * rc, axis=-1, keepdims=True)
        o_ref[0] = rc * lax.rsqrt(var + NORM_EPS) * l2g_ref[...] + l2b_ref[...]


def _peer(h2t, cnt, e1, rk2, e2, ub, vt, x1, g2, l2g, l2b, tiles):
    bsz, L, dm = x1.shape
    tt, ni = tiles.peer, tiles.peer_ni
    nl = L // tt
    eb = ni * PEER_NKEYS
    tab = pl.BlockSpec((PEER_HEADS, PEER_NKEYS, tt), lambda t, e: (0, 0, t))
    full = lambda a: pl.BlockSpec(a.shape, lambda t, e: (0,) * a.ndim)
    ne = PEER_NKEYS // ni
    assert ne % 2 == 0 and ni % tiles.peer_sub == 0 and ni % tiles.peer_nq == 0
    assert ni % (2 * SUBLANES) == 0
    return pl.pallas_call(
        functools.partial(_peer_kernel, ni=ni, sub=tiles.peer_sub, nq=tiles.peer_nq),
        grid=(bsz * nl, ne + 1),
        in_specs=[pl.BlockSpec((dm, tt), lambda t, e: (0, t)), tab, tab, tab, tab,
                  pl.BlockSpec((eb, dm), lambda t, e: (jnp.minimum(e, ne - 1), 0)),
                  pl.BlockSpec((1, dm, eb), lambda t, e: (jnp.maximum(e - 1, 0), 0, 0)),
                  pl.BlockSpec((1, tt, dm), lambda t, e: (t // nl, t % nl, 0)),
                  pl.BlockSpec((1, 1, dm), lambda t, e: (t // nl, 0, 0)),
                  full(l2g), full(l2b)],
        out_specs=pl.BlockSpec((1, tt, dm), lambda t, e: (t // nl, t % nl, 0)),
        out_shape=jax.ShapeDtypeStruct((bsz, L, dm), F32),
        scratch_shapes=[pltpu.VMEM((dm, tt), F32), pltpu.VMEM((eb, tt), BF16), pltpu.VMEM((eb, tt), BF16)],
        compiler_params=_params(("parallel", "arbitrary")),
        name="peer",
    )(h2t, cnt, e1, rk2, e2, ub, vt, x1, g2, l2g, l2b)


def _s5_discretize(lam_re, lam_im, log_dt, b_re, b_im, c_re, c_im):
    dt = jnp.exp(log_dt)[:, None]
    mag = jnp.exp(lam_re * dt)
    ab_re = mag * jnp.cos(lam_im * dt)
    ab_im = mag * jnp.sin(lam_im * dt)
    den = lam_re * lam_re + lam_im * lam_im
    nr = ab_re - 1.0
    coef_re = (nr * lam_re + ab_im * lam_im) / den
    coef_im = (ab_im * lam_re - nr * lam_im) / den
    bb_re = coef_re[..., None] * b_re - coef_im[..., None] * b_im
    bb_im = coef_re[..., None] * b_im + coef_im[..., None] * b_re
    nslab = S5_WIDTH // LANES
    gs = S5_GROUPS // nslab
    eye = jnp.eye(gs, dtype=F32)
    slab = lambda a: a.reshape((nslab, gs) + a.shape[1:])
    wb_re = jnp.einsum("sgph,gk->sghkp", slab(bb_re), eye).reshape(nslab, LANES, gs * S5_STATE)
    wb_im = jnp.einsum("sgph,gk->sghkp", slab(bb_im), eye).reshape(nslab, LANES, gs * S5_STATE)
    wb = jnp.concatenate([wb_re, wb_im], axis=2).astype(BF16)
    wc_re = jnp.einsum("sghp,gk->sgpkh", slab(c_re), eye).reshape(nslab, gs * S5_STATE, LANES)
    wc_im = jnp.einsum("sghp,gk->sgpkh", slab(c_im), eye).reshape(nslab, gs * S5_STATE, LANES)
    wc = jnp.concatenate([wc_re, -wc_im], axis=1).astype(BF16)
    return wb, wc, ab_re.reshape(1, S5_COLS), ab_im.reshape(1, S5_COLS)


def _pad_heads(w, width, offset):
    kdim = w.shape[0]
    w = w.reshape(kdim, MLA_HEADS, width)
    out = jnp.zeros((kdim, MLA_HEADS, HEAD_PAD), w.dtype)
    out = out.at[:, :, offset:offset + width].set(w)
    return out.reshape(kdim, MLA_HEADS * HEAD_PAD)


def _swap_halves(w):
    half = w.shape[-1] // 2
    return jnp.concatenate([w[..., half:], w[..., :half]], axis=-1)


def kernel(x, c, positions, w_ada, b_ada, w_in, s5_lambda_re, s5_lambda_im, s5_log_dt, s5_b_re, s5_b_im, s5_c_re, s5_c_im, s5_d, s5_w_glu, s5_b_glu, mla_q_norm, mla_w_uq, mla_kv_norm, mla_w_ukv, gn_s5, gn_mla, w_out, ln1_g, ln1_b, peer_w_query, peer_keys1, peer_keys2, peer_u, peer_v, ln2_g, ln2_b):
    bsz, L, dm = x.shape
    depth = w_ada.shape[0]
    tiles = _tiles(L)

    invf = (ROPE_THETA ** (-jnp.arange(0, MLA_ROPE, 2, dtype=F32) / MLA_ROPE))[:, None]
    posf = positions.astype(F32)[:, None, :]

    for l in range(depth):
        mod = _ada(c, w_ada[l], b_ada[l])
        sh1, sc1, g1, sh2, sc2, g2 = [m[:, None, :] for m in jnp.split(mod, 6, axis=-1)]

        wi = w_in[l]
        o1, o2, o3 = S5_WIDTH, S5_WIDTH + Q_LORA, S5_WIDTH + Q_LORA + KV_LORA
        w_kr = wi[:, o3:]
        zpad = lambda w: jnp.zeros((dm, HEAD_PAD), F32).at[:, MLA_NOPE:MLA_NOPE + MLA_ROPE].set(w)
        win = jnp.concatenate([wi[:, :o3], zpad(w_kr), zpad(_swap_halves(w_kr))], axis=1).astype(BF16)
        scale = (MLA_NOPE + MLA_ROPE) ** -0.5 * math.log2(math.e)
        wuq = (mla_w_uq[l] * mla_q_norm[l][:, None] * scale).reshape(Q_LORA, MLA_HEADS, MLA_NOPE + MLA_ROPE)
        wuq_main = wuq.reshape(Q_LORA, MLA_HEADS * (MLA_NOPE + MLA_ROPE))
        wqa = _pad_heads(wuq_main, MLA_NOPE + MLA_ROPE, 0).astype(BF16)
        wuq_sw = _swap_halves(wuq[:, :, MLA_NOPE:]).reshape(Q_LORA, MLA_HEADS * MLA_ROPE)
        wqb = _pad_heads(wuq_sw, MLA_ROPE, MLA_NOPE).astype(BF16)
        wukv = (mla_w_ukv[l] * mla_kv_norm[l][:, None]).reshape(KV_LORA, MLA_HEADS, MLA_NOPE + MLA_V)
        wk = _pad_heads(wukv[:, :, :MLA_NOPE].reshape(KV_LORA, MLA_HEADS * MLA_NOPE), MLA_NOPE, 0).astype(BF16)
        wv = wukv[:, :, MLA_NOPE:].reshape(KV_LORA, MLA_HEADS * MLA_V).T.astype(BF16)

        u, q, k, v = _proj(x, posf, sc1, sh1, win, wqa, wqb, wk, wv, invf, tiles.proj)

        wb, wc, are, aim = _s5_discretize(s5_lambda_re[l], s5_lambda_im[l], s5_log_dt[l], s5_b_re[l],
                                          s5_b_im[l], s5_c_re[l], s5_c_im[l])
        are = jnp.broadcast_to(are, (bsz, S5_COLS))
        aim = jnp.broadcast_to(aim, (bsz, S5_COLS))
        y_s5 = _s5(u, wb, wc, are, aim, s5_d[l][None], s5_w_glu[l].astype(BF16), s5_b_glu[l][None],
                   gn_s5[l][None], tiles.s5, tiles.scan_cols)

        y_mla = _attn(q, k, v, tiles.attn)

        wo = w_out[l].astype(BF16)
        wqt = peer_w_query[l].T.astype(BF16)
        x1, h2t, cnt, e1, rk2, e2 = _mix(x, y_s5, y_mla, g1, sc2, sh2, gn_mla[l][None], wo[:S5_WIDTH],
                                         wo[S5_WIDTH:], ln1_g[l][None], ln1_b[l][None], wqt,
                                         peer_keys1[l].astype(BF16), peer_keys2[l].astype(BF16), tiles.mix)

        eblk = tiles.peer_ni * PEER_NKEYS
        vt = peer_v[l].astype(BF16).reshape(PEER_EXPERTS // eblk, eblk, dm).transpose(0, 2, 1)
        x = _peer(h2t, cnt, e1, rk2, e2, peer_u[l].astype(BF16), vt, x1, g2,
                  ln2_g[l][None], ln2_b[l][None], tiles)
    return x
```

```python
import functools
import math
from typing import NamedTuple

import jax
import jax.numpy as jnp
from jax import lax
from jax.experimental import pallas as pl
from jax.experimental.pallas import tpu as pltpu

F32 = jnp.float32
BF16 = jnp.bfloat16

D_MODEL = 1024
CHUNK = 64
S5_WIDTH = 512
S5_GROUP = 16
S5_GROUPS = 32
S5_STATE = 64
S5_COLS = S5_GROUPS * S5_STATE
MLA_WIDTH = 512
MLA_HEADS = 8
MLA_NOPE = 64
MLA_ROPE = 32
MLA_V = 64
Q_LORA = 384
KV_LORA = 256
ROPE_THETA = 10000.0
HEAD_PAD = 128
KEY_CHUNKS = 2
PEER_HEADS = 8
PEER_NKEYS = 128
PEER_QDIM = 256
PEER_HALF = 128
PEER_TOPK = 16
PEER_EXPERTS = PEER_NKEYS * PEER_NKEYS
DEEPNORM_ALPHA = 2.0 ** 0.25
NORM_EPS = 1e-6
MASK_VALUE = -1e30
NEG_BIG = -3.0e38

LANES = 128
SUBLANES = 8
VMEM_LIMIT = 56 * 1024 * 1024


def _gelu(x):
    return 0.5 * x * (1.0 + jnp.tanh(0.7978845608028654 * (x + 0.044715 * (x * x * x))))


def _sigmoid(x):
    return 1.0 / (1.0 + jnp.exp(-x))


class _Tiles(NamedTuple):
    proj: int
    s5: int
    attn: int
    mix: int
    peer: int
    peer_ni: int
    peer_sub: int
    peer_nq: int
    scan_cols: int


def _tiles(seq_len):
    return _Tiles(proj=min(512, seq_len), s5=min(128, seq_len), attn=min(1024, seq_len), mix=min(256, seq_len),
                  peer=min(512, seq_len), peer_ni=16, peer_sub=2, peer_nq=4, scan_cols=512)


def _params(sem):
    return pltpu.CompilerParams(dimension_semantics=sem, vmem_limit_bytes=VMEM_LIMIT)


def _ada_kernel(c_ref, w_ref, b_ref, o_ref):
    c = c_ref[...]
    ca = c * _sigmoid(c)
    o_ref[...] = jnp.dot(ca, w_ref[...], preferred_element_type=F32,
                         precision=lax.Precision.HIGHEST) + b_ref[...]


def _ada(c, w_ada, b_ada):
    bsz, dm = c.shape
    n = w_ada.shape[1]
    tn = dm
    return pl.pallas_call(
        _ada_kernel,
        grid=(n // tn,),
        in_specs=[pl.BlockSpec((bsz, dm), lambda j: (0, 0)),
                  pl.BlockSpec((dm, tn), lambda j: (0, j)),
                  pl.BlockSpec((1, tn), lambda j: (0, j))],
        out_specs=pl.BlockSpec((bsz, tn), lambda j: (0, j)),
        out_shape=jax.ShapeDtypeStruct((bsz, n), F32),
        compiler_params=_params(("arbitrary",)),
        name="ada",
    )(c, w_ada, b_ada.reshape(1, n))


def _proj_kernel(x_ref, pos_ref, sc_ref, sh_ref, win_ref, wqa_ref, wqb_ref, wk_ref, wv_ref,
                 invf_ref, u_ref, q_ref, k_ref, v_ref):
    x = x_ref[0]
    h = (x * (1.0 + sc_ref[0]) + sh_ref[0]).astype(BF16)
    proj = jnp.dot(h, win_ref[...], preferred_element_type=F32)
    o = S5_WIDTH
    u_ref[0] = proj[:, :o]
    cq = proj[:, o:o + Q_LORA]
    o += Q_LORA
    ckv = proj[:, o:o + KV_LORA]
    o += KV_LORA
    kra = proj[:, o:o + HEAD_PAD]
    krb = proj[:, o + HEAD_PAD:o + 2 * HEAD_PAD]
    ang = invf_ref[...] * pos_ref[0]
    cos_t = jnp.cos(ang)
    sin_t = jnp.sin(ang)
    tt = ang.shape[1]
    cs = jnp.concatenate([jnp.ones((MLA_NOPE, tt), F32), cos_t, cos_t,
                          jnp.zeros((HEAD_PAD - MLA_NOPE - MLA_ROPE, tt), F32)], axis=0).T
    sn = jnp.concatenate([jnp.zeros((MLA_NOPE, tt), F32), -sin_t, sin_t,
                          jnp.zeros((HEAD_PAD - MLA_NOPE - MLA_ROPE, tt), F32)], axis=0).T
    cqn = (cq * lax.rsqrt(jnp.mean(cq * cq, axis=-1, keepdims=True) + NORM_EPS)).astype(BF16)
    qa = jnp.dot(cqn, wqa_ref[...], preferred_element_type=F32)
    qb = jnp.dot(cqn, wqb_ref[...], preferred_element_type=F32)
    ckvn = (ckv * lax.rsqrt(jnp.mean(ckv * ckv, axis=-1, keepdims=True) + NORM_EPS)).astype(BF16)
    kn = jnp.dot(ckvn, wk_ref[...], preferred_element_type=F32)
    kp = kra * cs + krb * sn
    for hh in range(MLA_HEADS):
        sl = slice(hh * HEAD_PAD, (hh + 1) * HEAD_PAD)
        q_ref[0, :, sl] = (qa[:, sl] * cs + qb[:, sl] * sn).astype(BF16)
        k_ref[0, :, sl] = (kn[:, sl] + kp).astype(BF16)
    v_ref[0] = lax.dot_general(wv_ref[...], ckvn, (((1,), (1,)), ((), ())),
                               preferred_element_type=F32).astype(BF16)


def _proj(x, posf, sc1, sh1, win, wqa, wqb, wk, wv, invf, tt):
    bsz, L, dm = x.shape
    hp = MLA_HEADS * HEAD_PAD
    full = lambda a: pl.BlockSpec(a.shape, lambda b, i: (0,) * a.ndim)
    return pl.pallas_call(
        _proj_kernel,
        grid=(bsz, L // tt),
        in_specs=[pl.BlockSpec((1, tt, dm), lambda b, i: (b, i, 0)),
                  pl.BlockSpec((1, 1, tt), lambda b, i: (b, 0, i)),
                  pl.BlockSpec((1, 1, dm), lambda b, i: (b, 0, 0)),
                  pl.BlockSpec((1, 1, dm), lambda b, i: (b, 0, 0)),
                  full(win), full(wqa), full(wqb), full(wk), full(wv), full(invf)],
        out_specs=[pl.BlockSpec((1, tt, S5_WIDTH), lambda b, i: (b, i, 0)),
                   pl.BlockSpec((1, tt, hp), lambda b, i: (b, i, 0)),
                   pl.BlockSpec((1, tt, hp), lambda b, i: (b, i, 0)),
                   pl.BlockSpec((1, MLA_WIDTH, tt), lambda b, i: (b, 0, i))],
        out_shape=[jax.ShapeDtypeStruct((bsz, L, S5_WIDTH), F32),
                   jax.ShapeDtypeStruct((bsz, L, hp), BF16),
                   jax.ShapeDtypeStruct((bsz, L, hp), BF16),
                   jax.ShapeDtypeStruct((bsz, MLA_WIDTH, L), BF16)],
        compiler_params=_params(("parallel", "parallel")),
        name="proj",
    )(x, posf, sc1, sh1, win, wqa, wqb, wk, wv, invf)


def _s5_kernel(u_ref, wb_ref, wc_ref, are_ref, aim_ref, d_ref, wglu_ref, bglu_ref, gn_ref,
               y_ref, utm, st, yt, carry, *, tc, bsz, cb):
    @pl.when(pl.program_id(0) == 0)
    def _():
        carry[...] = jnp.zeros_like(carry)

    nslab = S5_WIDTH // LANES
    for b in range(bsz):
        for j in range(nslab):
            utm[j, pl.ds(b, tc, stride=bsz), :] = u_ref[b, :, j * LANES:(j + 1) * LANES]
    scols = S5_COLS // nslab
    us = [utm[j] for j in range(nslab)]
    for j in range(nslab):
        bu = jnp.dot(us[j].astype(BF16), wb_ref[j], preferred_element_type=F32)
        st[:, j * scols:(j + 1) * scols] = bu[:, :scols]
        st[:, S5_COLS + j * scols:S5_COLS + (j + 1) * scols] = bu[:, scols:]

    for c0 in range(0, S5_COLS, cb):
        cre = slice(c0, c0 + cb)
        cim = slice(S5_COLS + c0, S5_COLS + c0 + cb)
        are = are_ref[:, cre]
        aim = aim_ref[:, cre]

        def step(t, sc, cre=cre, cim=cim, are=are, aim=aim):
            sre, sim = sc
            r = pl.multiple_of(t * bsz, bsz)
            nre = are * sre - aim * sim + st[pl.ds(r, bsz), cre]
            nim = are * sim + aim * sre + st[pl.ds(r, bsz), cim]
            st[pl.ds(r, bsz), cre] = nre
            st[pl.ds(r, bsz), cim] = nim
            return nre, nim

        sre, sim = lax.fori_loop(0, tc, step, (carry[:, cre], carry[:, cim]), unroll=4)
        carry[:, cre] = sre
        carry[:, cim] = sim

    ys = []
    for j in range(nslab):
        s_re = st[:, j * scols:(j + 1) * scols].astype(BF16)
        s_im = st[:, S5_COLS + j * scols:S5_COLS + (j + 1) * scols].astype(BF16)
        ys.append(jnp.dot(s_re, wc_ref[j, :scols, :], preferred_element_type=F32)
                  + jnp.dot(s_im, wc_ref[j, scols:, :], preferred_element_type=F32)
                  + d_ref[:, j * LANES:(j + 1) * LANES] * us[j])
    y = jnp.concatenate(ys, axis=-1)
    y = _gelu(y)
    z = jnp.dot(y.astype(BF16), wglu_ref[...], preferred_element_type=F32) + bglu_ref[...]
    y = y * _sigmoid(z)
    y = y * lax.rsqrt(jnp.mean(y * y, axis=-1, keepdims=True) + NORM_EPS) * gn_ref[...]
    for j in range(nslab):
        yt[j] = y[:, j * LANES:(j + 1) * LANES]
    for b in range(bsz):
        for j in range(nslab):
            y_ref[b, :, j * LANES:(j + 1) * LANES] = yt[j, pl.ds(b, tc, stride=bsz), :].astype(BF16)


def _s5(u, wb, wc, are, aim, d, wglu, bglu, gn, tc, scan_cols):
    bsz, L, w = u.shape
    full = lambda a: pl.BlockSpec(a.shape, lambda i: (0,) * a.ndim)
    return pl.pallas_call(
        functools.partial(_s5_kernel, tc=tc, bsz=bsz, cb=scan_cols),
        grid=(L // tc,),
        in_specs=[pl.BlockSpec((bsz, tc, w), lambda i: (0, i, 0)),
                  full(wb), full(wc), full(are), full(aim), full(d), full(wglu), full(bglu), full(gn)],
        out_specs=pl.BlockSpec((bsz, tc, w), lambda i: (0, i, 0)),
        out_shape=jax.ShapeDtypeStruct((bsz, L, w), BF16),
        scratch_shapes=[pltpu.VMEM((w // LANES, tc * bsz, LANES), F32),
                        pltpu.VMEM((tc * bsz, 2 * S5_COLS), F32),
                        pltpu.VMEM((w // LANES, tc * bsz, LANES), F32),
                        pltpu.VMEM((bsz, 2 * S5_COLS), F32)],
        compiler_params=_params(("arbitrary",)),
        name="s5",
    )(u, wb, wc, are, aim, d, wglu, bglu, gn)


def _attn_kernel(q_ref, k_ref, vt_ref, o_ref, sa, sb, mxa, mxb, m_s, l_s, acc_s, *, tq):
    qi = pl.program_id(2)
    nt = (((1,), (1,)), ((), ()))
    qs = [q_ref[0, :, hh * HEAD_PAD:(hh + 1) * HEAD_PAD] for hh in range(2)]

    m_s[...] = jnp.full(m_s.shape, MASK_VALUE, F32)
    l_s[...] = jnp.zeros_like(l_s)
    acc_s[...] = jnp.zeros_like(acc_s)

    def scores_into(kb, sbuf, mxbuf):
        ks = pl.multiple_of(kb * tq, tq)
        for hh in range(2):
            st = lax.dot_general(k_ref[0, pl.ds(ks, tq), hh * HEAD_PAD:(hh + 1) * HEAD_PAD], qs[hh], nt,
                                 preferred_element_type=F32)
            sbuf[hh] = st
            mxbuf[hh] = jnp.max(st, axis=0, keepdims=True)

    def softmax_from(kb, sbuf, mxbuf, masked):
        ks = pl.multiple_of(kb * tq, tq)
        ck = tq // KEY_CHUNKS

        def chunk(hh, c):
            st = sbuf[hh, c * ck:(c + 1) * ck, :]
            if masked:
                kc = (lax.broadcasted_iota(jnp.int32, (ck, tq), 0) + c * ck) // CHUNK
                qc = lax.broadcasted_iota(jnp.int32, (ck, tq), 1) // CHUNK
                st = jnp.where(kc <= qc, st, MASK_VALUE)
            return st

        for hh in range(2):
            if masked:
                mx = jnp.max(chunk(hh, 0), axis=0, keepdims=True)
                for c in range(1, KEY_CHUNKS):
                    mx = jnp.maximum(mx, jnp.max(chunk(hh, c), axis=0, keepdims=True))
            else:
                mx = mxbuf[hh]
            m = m_s[hh]
            m_new = jnp.maximum(m, mx)
            alpha = jnp.exp2(m - m_new)
            l_new = alpha * l_s[hh]
            acc_new = alpha * acc_s[hh]
            for c in range(KEY_CHUNKS):
                pt = jnp.exp2(chunk(hh, c) - m_new)
                vt = vt_ref[0, hh * MLA_V:(hh + 1) * MLA_V, pl.ds(pl.multiple_of(ks + c * ck, ck), ck)]
                l_new = l_new + jnp.sum(pt, axis=0, keepdims=True)
                acc_new = acc_new + jnp.dot(vt, pt.astype(BF16), preferred_element_type=F32)
            l_s[hh] = l_new
            acc_s[hh] = acc_new
            m_s[hh] = m_new

    def finish():
        o_ref[0] = jnp.concatenate([acc_s[0] / l_s[0], acc_s[1] / l_s[1]], axis=0).T.astype(BF16)

    scores_into(0, sa, mxa)

    def pair(j, carry):
        scores_into(2 * j + 1, sb, mxb)
        softmax_from(2 * j, sa, mxa, False)
        scores_into(2 * j + 2, sa, mxa)
        softmax_from(2 * j + 1, sb, mxb, False)
        return carry

    lax.fori_loop(0, qi // 2, pair, 0)

    @pl.when(qi % 2 == 0)
    def _():
        softmax_from(qi, sa, mxa, True)
        finish()

    @pl.when(qi % 2 == 1)
    def _():
        scores_into(qi, sb, mxb)
        softmax_from(qi - 1, sa, mxa, False)
        softmax_from(qi, sb, mxb, True)
        finish()


def _attn(q, k, vt, tq):
    bsz, L, hp = q.shape
    return pl.pallas_call(
        functools.partial(_attn_kernel, tq=tq),
        grid=(bsz, MLA_HEADS // 2, L // tq),
        in_specs=[pl.BlockSpec((1, tq, 2 * HEAD_PAD), lambda b, h, i: (b, i, h)),
                  pl.BlockSpec((1, L, 2 * HEAD_PAD), lambda b, h, i: (b, 0, h)),
                  pl.BlockSpec((1, 2 * MLA_V, L), lambda b, h, i: (b, h, 0))],
        out_specs=pl.BlockSpec((1, tq, 2 * MLA_V), lambda b, h, i: (b, i, h)),
        out_shape=jax.ShapeDtypeStruct((bsz, L, MLA_WIDTH), BF16),
        scratch_shapes=[pltpu.VMEM((2, tq, tq), F32), pltpu.VMEM((2, tq, tq), F32),
                        pltpu.VMEM((2, 1, tq), F32), pltpu.VMEM((2, 1, tq), F32),
                        pltpu.VMEM((2, 1, tq), F32), pltpu.VMEM((2, 1, tq), F32),
                        pltpu.VMEM((2, MLA_V, tq), F32)],
        compiler_params=_params(("parallel", "parallel", "arbitrary")),
        name="attn",
    )(q, k, vt)


def _top16_rows(s):
    work = s
    ridx = lax.broadcasted_iota(jnp.int32, (PEER_TOPK, s.shape[1]), 0)
    vals = jnp.zeros((PEER_TOPK, s.shape[1]), F32)
    for r in range(PEER_TOPK):
        m = jnp.max(work, axis=0, keepdims=True)
        work = jnp.where(work == m, NEG_BIG, work)
        vals = jnp.where(ridx == r, m, vals)
    return vals


def _sort16_pairs():
    n, pairs, p = PEER_TOPK, [], 1
    while p < n:
        k = p
        while k >= 1:
            for j in range(k % p, n - k, 2 * k):
                for i in range(min(k, n - j - k)):
                    if (i + j) // (2 * p) == (i + j + k) // (2 * p):
                        pairs.append((i + j, i + j + k))
            k //= 2
        p *= 2
    return pairs


def _top16_sorted(x):
    n = PEER_TOPK
    a = [x[SUBLANES * k:SUBLANES * (k + 1)] for k in range(n)]

    def exchange(i, j):
        a[i], a[j] = jnp.maximum(a[i], a[j]), jnp.minimum(a[i], a[j])

    for i, j in _sort16_pairs():
        exchange(i, j)
    shift = SUBLANES // 2
    while shift >= 1:
        a = [jnp.maximum(a[k], pltpu.roll(a[n - 1 - k], shift, 0)) for k in range(n)]
        d = n // 2
        while d >= 1:
            for i in range(n):
                if i & d == 0:
                    exchange(i, i + d)
            d //= 2
        shift //= 2
    return a


def _rank_among_sorted(x, top):
    n = PEER_TOPK

    def pick(values, bits):
        if not bits:
            return values[0]
        half = len(values) // 2
        return jnp.where(bits[0], pick(values[half:], bits[1:]), pick(values[:half], bits[1:]))

    out = []
    for k in range(x.shape[0] // SUBLANES):
        xk = x[SUBLANES * k:SUBLANES * (k + 1)]
        bits, count, step = [], None, n // 2
        while step >= 1:
            bit = pick(top[step - 1:n - 1:2 * step], bits) > xk
            term = jnp.where(bit, float(step), 0.0)
            count = term if count is None else count + term
            bits.append(bit)
            step //= 2
        out.append(jnp.where(top[n - 1] > xk, float(n), count))
    return jnp.concatenate(out, axis=0)


def _mix_kernel(x_ref, ys_ref, ym_ref, g1_ref, sc2_ref, sh2_ref, gnm_ref, wo1_ref, wo2_ref,
                l1g_ref, l1b_ref, wq_ref, k1_ref, k2_ref,
                x1_ref, h2t_ref, cnt_ref, e1_ref, rk2_ref, e2_ref):
    x = x_ref[0]
    ym = ym_ref[0].astype(F32)
    ym = (ym * lax.rsqrt(jnp.mean(ym * ym, axis=-1, keepdims=True) + NORM_EPS) * gnm_ref[...]).astype(BF16)
    ymix = (jnp.dot(ys_ref[0], wo1_ref[...], preferred_element_type=F32)
            + jnp.dot(ym, wo2_ref[...], preferred_element_type=F32))
    r = DEEPNORM_ALPHA * x + (1.0 + g1_ref[0]) * ymix
    mu = jnp.mean(r, axis=-1, keepdims=True)
    rc = r - mu
    var = jnp.mean(rc * rc, axis=-1, keepdims=True)
    x1 = rc * lax.rsqrt(var + NORM_EPS) * l1g_ref[...] + l1b_ref[...]
    x1_ref[0] = x1
    h2 = x1 * (1.0 + sc2_ref[0]) + sh2_ref[0]
    h2t = h2.T.astype(BF16)
    h2t_ref[...] = h2t
    qt = jnp.dot(wq_ref[...], h2t, preferred_element_type=F32)
    for hh in range(PEER_HEADS):
        q1 = qt[hh * PEER_QDIM:hh * PEER_QDIM + PEER_HALF]
        q2 = qt[hh * PEER_QDIM + PEER_HALF:(hh + 1) * PEER_QDIM]
        s1 = jnp.dot(k1_ref[hh], q1.astype(BF16), preferred_element_type=F32)
        s2 = jnp.dot(k2_ref[hh], q2.astype(BF16), preferred_element_type=F32)
        r1 = _top16_sorted(s1)
        v1 = [r[0:1] for r in r1]
        r2 = _top16_sorted(s2)
        rk2 = _rank_among_sorted(s2, r2)
        sub_row = lax.broadcasted_iota(jnp.int32, r1[0].shape, 0)

        def rows_from(rs):
            out = rs[0]
            for a in range(1, SUBLANES):
                out = jnp.where(sub_row == a, rs[a], out)
            return out

        v2 = jnp.concatenate([rows_from(r2[:SUBLANES]), rows_from(r2[SUBLANES:])], axis=0)
        cands = [v1[0] + v2]
        for a in range(1, SUBLANES):
            cands.append(r1[a] + v2[0:SUBLANES])
        cands.append(rows_from(r1[SUBLANES:]) + v2[0:1])
        top = _top16_rows(jnp.concatenate(cands, axis=0))
        theta = top[PEER_TOPK - 1:PEER_TOPK]
        z = jnp.sum(jnp.exp(top - top[0:1]), axis=0, keepdims=True)
        dense_b = 4
        cnt = jnp.zeros_like(s1)
        for b in range(dense_b):
            cnt = cnt + jnp.where(s1 + v2[b:b + 1] >= theta, 1.0, 0.0)
        for a in range(3):
            extra = jnp.sum(jnp.where(v1[a] + v2[dense_b:] >= theta, 1.0, 0.0), axis=0, keepdims=True)
            cnt = cnt + jnp.where(s1 == v1[a], extra, 0.0)
        cnt_ref[hh] = jnp.where(s1 >= v1[PEER_TOPK - 1], cnt, 0.0).astype(BF16)
        e1_ref[hh] = jnp.exp(s1 - v1[0]).astype(BF16)
        rk2_ref[hh] = rk2.astype(BF16)
        e2_ref[hh] = (jnp.exp(s2 - v2[0:1]) / z).astype(BF16)


def _mix(x, ys, ym, g1, sc2, sh2, gnm, wo1, wo2, l1g, l1b, wqt, k1, k2, tt):
    bsz, L, dm = x.shape
    n = bsz * L
    nl = L // tt
    full = lambda a: pl.BlockSpec(a.shape, lambda b, i: (0,) * a.ndim)
    tok = lambda w: pl.BlockSpec((1, tt, w), lambda b, i: (b, i, 0))
    vec = pl.BlockSpec((1, 1, dm), lambda b, i: (b, 0, 0))
    tab = pl.BlockSpec((PEER_HEADS, PEER_NKEYS, tt), lambda b, i: (0, 0, b * nl + i))
    tab_bf16 = jax.ShapeDtypeStruct((PEER_HEADS, PEER_NKEYS, n), BF16)
    return pl.pallas_call(
        _mix_kernel,
        grid=(bsz, nl),
        in_specs=[tok(dm), tok(S5_WIDTH), tok(MLA_WIDTH), vec, vec, vec,
                  full(gnm), full(wo1), full(wo2), full(l1g), full(l1b), full(wqt), full(k1), full(k2)],
        out_specs=[tok(dm), pl.BlockSpec((dm, tt), lambda b, i: (0, b * nl + i)), tab, tab, tab, tab],
        out_shape=[jax.ShapeDtypeStruct((bsz, L, dm), F32), jax.ShapeDtypeStruct((dm, n), BF16),
                   tab_bf16, tab_bf16, tab_bf16, tab_bf16],
        compiler_params=_params(("parallel", "parallel")),
        name="mix",
    )(x, ys, ym, g1, sc2, sh2, gnm, wo1, wo2, l1g, l1b, wqt, k1, k2)


def _peer_kernel(h2t_ref, cnt_ref, e1_ref, rk2_ref, e2_ref, u_ref, vt_ref, x1_ref, g2_ref, l2g_ref, l2b_ref,
                 o_ref, acc, wa0, wa1, *, ni, sub, nq):
    eb = pl.program_id(1)
    ne = pl.num_programs(1) - 1
    dm = acc.shape[0]

    def consume(wa_r, q, nq):
        rows = slice(q * (dm // nq), (q + 1) * (dm // nq))
        acc[rows, :] += jnp.dot(vt_ref[0, rows, :], wa_r[...], preferred_element_type=F32)

    def step(wa_w, wa_r):
        sc = None
        base = pl.multiple_of(eb * ni, ni)
        ctile = [cnt_ref[hh, pl.ds(base, ni), :] for hh in range(PEER_HEADS)]
        etile = [e1_ref[hh, pl.ds(base, ni), :] for hh in range(PEER_HEADS)]
        for ii in range(ni):
            if ii % sub == 0:
                sc = jnp.dot(u_ref[ii * PEER_NKEYS:(ii + sub) * PEER_NKEYS, :], h2t_ref[...],
                             preferred_element_type=F32)
            if wa_r is not None and ii % (ni // nq) == 0:
                consume(wa_r, ii // (ni // nq), nq)
            rows = slice((ii % sub) * PEER_NKEYS, (ii % sub + 1) * PEER_NKEYS)
            w = None
            for hh in range(PEER_HEADS):
                c = ctile[hh][ii:ii + 1]
                e = etile[hh][ii:ii + 1]
                term = jnp.where(rk2_ref[hh] < c, e2_ref[hh], jnp.zeros((), BF16)) * e
                w = term if w is None else w + term
            wa_w[ii * PEER_NKEYS:(ii + 1) * PEER_NKEYS, :] = w * _gelu(sc[rows].astype(BF16))

    @pl.when(eb == 0)
    def _():
        acc[...] = jnp.zeros_like(acc)
        step(wa0, None)

    @pl.when((eb % 2 == 0) & (eb > 0) & (eb < ne))
    def _():
        step(wa0, wa1)

    @pl.when(eb % 2 == 1)
    def _():
        step(wa1, wa0)

    @pl.when(eb == ne)
    def _():
        consume(wa1, 0, 1)
        yff = acc[...].T
        r = DEEPNORM_ALPHA * x1_ref[0] + (1.0 + g2_ref[0]) * yff
        mu = jnp.mean(r, axis=-1, keepdims=True)
        rc = r - mu
        var = jnp.mean(rc * rc, axis=-1, keepdims=True)
        o_ref[0] = rc * lax.rsqrt(var + NORM_EPS) * l2g_ref[...] + l2b_ref[...]


def _peer(h2t, cnt, e1, rk2, e2, ub, vt, x1, g2, l2g, l2b, tiles):
    bsz, L, dm = x1.shape
    tt, ni = tiles.peer, tiles.peer_ni
    nl = L // tt
    eb = ni * PEER_NKEYS
    tab = pl.BlockSpec((PEER_HEADS, PEER_NKEYS, tt), lambda t, e: (0, 0, t))
    full = lambda a: pl.BlockSpec(a.shape, lambda t, e: (0,) * a.ndim)
    ne = PEER_NKEYS // ni
    assert ne % 2 == 0 and ni % tiles.peer_sub == 0 and ni % tiles.peer_nq == 0
    assert ni % (2 * SUBLANES) == 0
    return pl.pallas_call(
        functools.partial(_peer_kernel, ni=ni, sub=tiles.peer_sub, nq=tiles.peer_nq),
        grid=(bsz * nl, ne + 1),
        in_specs=[pl.BlockSpec((dm, tt), lambda t, e: (0, t)), tab, tab, tab, tab,
                  pl.BlockSpec((eb, dm), lambda t, e: (jnp.minimum(e, ne - 1), 0)),
                  pl.BlockSpec((1, dm, eb), lambda t, e: (jnp.maximum(e - 1, 0), 0, 0)),
                  pl.BlockSpec((1, tt, dm), lambda t, e: (t // nl, t % nl, 0)),
                  pl.BlockSpec((1, 1, dm), lambda t, e: (t // nl, 0, 0)),
                  full(l2g), full(l2b)],
        out_specs=pl.BlockSpec((1, tt, dm), lambda t, e: (t // nl, t % nl, 0)),
        out_shape=jax.ShapeDtypeStruct((bsz, L, dm), F32),
        scratch_shapes=[pltpu.VMEM((dm, tt), F32), pltpu.VMEM((eb, tt), BF16), pltpu.VMEM((eb, tt), BF16)],
        compiler_params=_params(("parallel", "arbitrary")),
        name="peer",
    )(h2t, cnt, e1, rk2, e2, ub, vt, x1, g2, l2g, l2b)


def _s5_discretize(lam_re, lam_im, log_dt, b_re, b_im, c_re, c_im):
    dt = jnp.exp(log_dt)[:, None]
    mag = jnp.exp(lam_re * dt)
    ab_re = mag * jnp.cos(lam_im * dt)
    ab_im = mag * jnp.sin(lam_im * dt)
    den = lam_re * lam_re + lam_im * lam_im
    nr = ab_re - 1.0
    coef_re = (nr * lam_re + ab_im * lam_im) / den
    coef_im = (ab_im * lam_re - nr * lam_im) / den
    bb_re = coef_re[..., None] * b_re - coef_im[..., None] * b_im
    bb_im = coef_re[..., None] * b_im + coef_im[..., None] * b_re
    nslab = S5_WIDTH // LANES
    gs = S5_GROUPS // nslab
    eye = jnp.eye(gs, dtype=F32)
    slab = lambda a: a.reshape((nslab, gs) + a.shape[1:])
    wb_re = jnp.einsum("sgph,gk->sghkp", slab(bb_re), eye).reshape(nslab, LANES, gs * S5_STATE)
    wb_im = jnp.einsum("sgph,gk->sghkp", slab(bb_im), eye).reshape(nslab, LANES, gs * S5_STATE)
    wb = jnp.concatenate([wb_re, wb_im], axis=2).astype(BF16)
    wc_re = jnp.einsum("sghp,gk->sgpkh", slab(c_re), eye).reshape(nslab, gs * S5_STATE, LANES)
    wc_im = jnp.einsum("sghp,gk->sgpkh", slab(c_im), eye).reshape(nslab, gs * S5_STATE, LANES)
    wc = jnp.concatenate([wc_re, -wc_im], axis=1).astype(BF16)
    return wb, wc, ab_re.reshape(1, S5_COLS), ab_im.reshape(1, S5_COLS)


def _pad_heads(w, width, offset):
    kdim = w.shape[0]
    w = w.reshape(kdim, MLA_HEADS, width)
    out = jnp.zeros((kdim, MLA_HEADS, HEAD_PAD), w.dtype)
    out = out.at[:, :, offset:offset + width].set(w)
    return out.reshape(kdim, MLA_HEADS * HEAD_PAD)


def _swap_halves(w):
    half = w.shape[-1] // 2
    return jnp.concatenate([w[..., half:], w[..., :half]], axis=-1)


def kernel(x, c, positions, w_ada, b_ada, w_in, s5_lambda_re, s5_lambda_im, s5_log_dt, s5_b_re, s5_b_im, s5_c_re, s5_c_im, s5_d, s5_w_glu, s5_b_glu, mla_q_norm, mla_w_uq, mla_kv_norm, mla_w_ukv, gn_s5, gn_mla, w_out, ln1_g, ln1_b, peer_w_query, peer_keys1, peer_keys2, peer_u, peer_v, ln2_g, ln2_b):
    bsz, L, dm = x.shape
    depth = w_ada.shape[0]
    tiles = _tiles(L)

    invf = (ROPE_THETA ** (-jnp.arange(0, MLA_ROPE, 2, dtype=F32) / MLA_ROPE))[:, None]
    posf = positions.astype(F32)[:, None, :]

    for l in range(depth):
        mod = _ada(c, w_ada[l], b_ada[l])
        sh1, sc1, g1, sh2, sc2, g2 = [m[:, None, :] for m in jnp.split(mod, 6, axis=-1)]

        wi = w_in[l]
        o1, o2, o3 = S5_WIDTH, S5_WIDTH + Q_LORA, S5_WIDTH + Q_LORA + KV_LORA
        w_kr = wi[:, o3:]
        zpad = lambda w: jnp.zeros((dm, HEAD_PAD), F32).at[:, MLA_NOPE:MLA_NOPE + MLA_ROPE].set(w)
        win = jnp.concatenate([wi[:, :o3], zpad(w_kr), zpad(_swap_halves(w_kr))], axis=1).astype(BF16)
        scale = (MLA_NOPE + MLA_ROPE) ** -0.5 * math.log2(math.e)
        wuq = (mla_w_uq[l] * mla_q_norm[l][:, None] * scale).reshape(Q_LORA, MLA_HEADS, MLA_NOPE + MLA_ROPE)
        wuq_main = wuq.reshape(Q_LORA, MLA_HEADS * (MLA_NOPE + MLA_ROPE))
        wqa = _pad_heads(wuq_main, MLA_NOPE + MLA_ROPE, 0).astype(BF16)
        wuq_sw = _swap_halves(wuq[:, :, MLA_NOPE:]).reshape(Q_LORA, MLA_HEADS * MLA_ROPE)
        wqb = _pad_heads(wuq_sw, MLA_ROPE, MLA_NOPE).astype(BF16)
        wukv = (mla_w_ukv[l] * mla_kv_norm[l][:, None]).reshape(KV_LORA, MLA_HEADS, MLA_NOPE + MLA_V)
        wk = _pad_heads(wukv[:, :, :MLA_NOPE].reshape(KV_LORA, MLA_HEADS * MLA_NOPE), MLA_NOPE, 0).astype(BF16)
        wv = wukv[:, :, MLA_NOPE:].reshape(KV_LORA, MLA_HEADS * MLA_V).T.astype(BF16)

        u, q, k, v = _proj(x, posf, sc1, sh1, win, wqa, wqb, wk, wv, invf, tiles.proj)

        wb, wc, are, aim = _s5_discretize(s5_lambda_re[l], s5_lambda_im[l], s5_log_dt[l], s5_b_re[l],
                                          s5_b_im[l], s5_c_re[l], s5_c_im[l])
        are = jnp.broadcast_to(are, (bsz, S5_COLS))
        aim = jnp.broadcast_to(aim, (bsz, S5_COLS))
        y_s5 = _s5(u, wb, wc, are, aim, s5_d[l][None], s5_w_glu[l].astype(BF16), s5_b_glu[l][None],
                   gn_s5[l][None], tiles.s5, tiles.scan_cols)

        y_mla = _attn(q, k, v, tiles.attn)

        wo = w_out[l].astype(BF16)
        wqt = peer_w_query[l].T.astype(BF16)
        x1, h2t, cnt, e1, rk2, e2 = _mix(x, y_s5, y_mla, g1, sc2, sh2, gn_mla[l][None], wo[:S5_WIDTH],
                                         wo[S5_WIDTH:], ln1_g[l][None], ln1_b[l][None], wqt,
                                         peer_keys1[l].astype(BF16), peer_keys2[l].astype(BF16), tiles.mix)

        eblk = tiles.peer_ni * PEER_NKEYS
        vt = peer_v[l].astype(BF16).reshape(PEER_EXPERTS // eblk, eblk, dm).transpose(0, 2, 1)
        x = _peer(h2t, cnt, e1, rk2, e2, peer_u[l].astype(BF16), vt, x1, g2,
                  ln2_g[l][None], ln2_b[l][None], tiles)
    return x
```
